```python
import jax
import jax.numpy as jnp
from jax import lax
import numpy as np

D_MODEL = 1024
BATCH = 2
SEQ = 8192
DEPTH = 4
DEC_BATCH = 128
DEC_SEQ = 8
PAST_LEN = 2048
PAGE_SIZE = 128

HEAD_DIM = 64
ROPE_THETA = 10000.0
NORM_EPS = 1e-6
N_MIXERS = 3

DIL_GROUPS = ((128, 1), (512, 4), (2048, 16))
N_GROUPS_A = len(DIL_GROUPS)
HQ_A = 8
HKV_A = 2
Q_A = N_GROUPS_A * HQ_A * HEAD_DIM
KV_A = N_GROUPS_A * HKV_A * HEAD_DIM
OUT_A = HQ_A * HEAD_DIM
IN_A = Q_A + 2 * KV_A + OUT_A
A_Q_BLOCK = 128

HQ_B = 16
HKV_B = 4
MOBA_BLOCK = 256
MOBA_TOPK = 3
MOBA_Q_BLOCK = 32
Q_B = HQ_B * HEAD_DIM
KV_B = HKV_B * HEAD_DIM
OUT_B = Q_B
IN_B = Q_B + 2 * KV_B + OUT_B

HQ_C = 16
HKV_C = 4
N_IDX_HEADS = 8
IDX_DIM = 64
DSA_TOPK = 256
DSA_Q_BLOCK = 128
Q_C = HQ_C * HEAD_DIM
KV_C = HKV_C * HEAD_DIM
OUT_C = Q_C
IN_C = Q_C + 2 * KV_C + OUT_C + N_IDX_HEADS * IDX_DIM + IDX_DIM + N_IDX_HEADS

IN_WIDTH = (IN_A, IN_B, IN_C)
OUT_WIDTH = (OUT_A, OUT_B, OUT_C)
F32 = jnp.float32

kernel_name = 'hybrid_dilated_moba_dsa_decoder_step'


def rms_norm(x, gain):
    xf = x.astype(F32)
    y = xf * lax.rsqrt(jnp.mean(xf * xf, axis=-1, keepdims=True) + NORM_EPS)
    return (y * gain.astype(F32)).astype(x.dtype)


def rope(x, pos):
    half = x.shape[-1] // 2
    inv_freq = ROPE_THETA ** (-jnp.arange(half, dtype=F32) / half)
    ang = pos.astype(F32)[:, None] * inv_freq[None, :]
    shape = (1, pos.shape[0]) + (1,) * (x.ndim - 3) + (half,)
    cos = jnp.cos(ang).reshape(shape)
    sin = jnp.sin(ang).reshape(shape)
    xf = x.astype(F32)
    x1, x2 = xf[..., :half], xf[..., half:]
    return jnp.concatenate([x1 * cos - x2 * sin, x2 * cos + x1 * sin], axis=-1).astype(x.dtype)


def split_cols(z, sizes):
    cuts = [int(c) for c in np.cumsum(sizes)[:-1]]
    return jnp.split(z, cuts, axis=-1)


def gather_pages(pool, page_table):
    rows = pool[page_table]
    return rows.reshape((page_table.shape[0], page_table.shape[1] * pool.shape[1]) + pool.shape[2:])


def to_blocks(a, block):
    b, t = a.shape[:2]
    return a.reshape((b, t // block, block) + a.shape[2:]).swapaxes(0, 1)


def from_blocks(a):
    nb, b, blk = a.shape[:3]
    return a.swapaxes(0, 1).reshape((b, nb * blk) + a.shape[3:])


def project_a(h, pos, w_in, q_gain, k_gain):
    b, t, _ = h.shape
    z = jnp.einsum('btd,de->bte', h, w_in)
    q, k, v, gate = split_cols(z, (Q_A, KV_A, KV_A, OUT_A))
    q = q.reshape(b, t, N_GROUPS_A, HKV_A, HQ_A // HKV_A, HEAD_DIM)
    k = k.reshape(b, t, N_GROUPS_A, HKV_A, HEAD_DIM)
    v = v.reshape(b, t, N_GROUPS_A, HKV_A, HEAD_DIM)
    q = rope(rms_norm(q, q_gain), pos)
    k = rope(rms_norm(k, k_gain), pos)
    return q, k, v, gate


def dilated_group_attend(q, kc, vc, qidx, window, dilation):
    offs = dilation * jnp.arange(window // dilation + 1)
    idx = qidx[:, None] - offs[None, :]
    valid = idx >= 0
    idx = jnp.maximum(idx, 0)
    kg = jnp.take(kc, idx, axis=1)
    vg = jnp.take(vc, idx, axis=1)
    s = jnp.einsum('btgrd,btjgd->btgrj', q, kg, preferred_element_type=F32) * (HEAD_DIM ** -0.5)
    s = jnp.where(valid[None, :, None, None, :], s, -jnp.inf)
    m = jnp.max(s, axis=-1, keepdims=True)
    p = jnp.exp(s - m)
    den = jnp.sum(p, axis=-1)
    o = jnp.einsum('btgrj,btjgd->btgrd', p, vg, preferred_element_type=F32) / den[..., None]
    return o, m[..., 0] + jnp.log(den)


def dilated_mix(q, k_ctx, v_ctx, q_idx):
    outs, lses = [], []
    for g, (window, dilation) in enumerate(DIL_GROUPS):
        o, lse = dilated_group_attend(q[:, :, g], k_ctx[g], v_ctx[g], q_idx[g], window, dilation)
        outs.append(o)
        lses.append(lse)
    wgt = jax.nn.softmax(jnp.stack(lses, 0), axis=0)
    o = jnp.einsum('nbtgr,nbtgrd->btgrd', wgt, jnp.stack(outs, 0))
    return o.reshape(o.shape[:2] + (OUT_A,))


def mixer_a(hp, hs, states, w_in, q_gain, k_gain):
    tp, ts = hp.shape[1], hs.shape[1]
    qp, kp, vp, gp = project_a(hp, jnp.arange(tp), w_in, q_gain, k_gain)
    qs, ks, vs, gs = project_a(hs, PAST_LEN + jnp.arange(ts), w_in, q_gain, k_gain)
    kp_g = [kp[:, :, g] for g in range(N_GROUPS_A)]
    vp_g = [vp[:, :, g] for g in range(N_GROUPS_A)]

    def step(args):
        qb, ib = args
        return dilated_mix(qb, kp_g, vp_g, (ib,) * N_GROUPS_A)

    o_p = from_blocks(lax.map(step, (to_blocks(qp, A_Q_BLOCK), jnp.arange(tp).reshape(-1, A_Q_BLOCK))))
    ctxs = [jnp.concatenate([states[g], jnp.stack([ks[:, :, g], vs[:, :, g]], axis=2)], axis=1)
            for g in range(N_GROUPS_A)]
    q_idx_s = [states[g].shape[1] + jnp.arange(ts) for g in range(N_GROUPS_A)]
    o_s = dilated_mix(qs, [c[:, :, 0] for c in ctxs], [c[:, :, 1] for c in ctxs], q_idx_s)
    new_state = []
    for g, (window, _) in enumerate(DIL_GROUPS):
        new_state.append(jnp.stack([kp[:, :, g], vp[:, :, g]], axis=2)[:, tp - min(window, tp):])
        lc = ctxs[g].shape[1]
        new_state.append(ctxs[g][:, lc - min(window, lc):])
    return (o_p.astype(hp.dtype) * jax.nn.silu(gp), o_s.astype(hs.dtype) * jax.nn.silu(gs), new_state)


def project_heads(h, pos, w_in, q_gain, k_gain, hq, hkv, extra_sizes):
    b, t, _ = h.shape
    z = jnp.einsum('btd,de->bte', h, w_in)
    q, k, v, gate, *rest = split_cols(z, (hq * HEAD_DIM, hkv * HEAD_DIM, hkv * HEAD_DIM, hq * HEAD_DIM) + tuple(extra_sizes))
    q = rope(rms_norm(q.reshape(b, t, hkv, hq // hkv, HEAD_DIM), q_gain), pos)
    k = rope(rms_norm(k.reshape(b, t, hkv, HEAD_DIM), k_gain), pos)
    v = v.reshape(b, t, hkv, HEAD_DIM)
    return q, k, v, gate, rest


def moba_blocks(kc, vc):
    b, l, g, dh = kc.shape
    nbk = -(-l // MOBA_BLOCK)
    pad = nbk * MOBA_BLOCK - l

    def blocks(a):
        a = jnp.pad(a, ((0, 0), (0, pad), (0, 0), (0, 0)))
        return a.reshape(b, nbk, MOBA_BLOCK, g, dh).transpose(0, 3, 1, 2, 4)

    kb, vb = blocks(kc), blocks(vc)
    return kb, vb, jnp.mean(kb.astype(F32), axis=3)


def moba_attend(q, qpos, kb, vb, kmean):
    b, tq, g, r, dh = q.shape
    nbk = kb.shape[2]
    scale = HEAD_DIM ** -0.5
    own = qpos // MOBA_BLOCK
    bs = jnp.einsum('btgrd,bgnd->btgrn', q.astype(F32), kmean)
    past = jnp.arange(nbk)[None, :] < own[:, None]
    bs = jnp.where(past[None, :, None, None, :], bs, -jnp.inf)
    n_sel = min(MOBA_TOPK, nbk)
    _, sel = lax.top_k(bs, n_sel)
    sel_ok = sel < own[None, :, None, None, None]
    b_ix = jnp.arange(b)[:, None, None, None, None]
    g_ix = jnp.arange(g)[None, None, :, None, None]
    ks = kb[b_ix, g_ix, sel]
    vs = vb[b_ix, g_ix, sel]
    s_sel = jnp.einsum('btgrd,btgrjkd->btgrjk', q, ks, preferred_element_type=F32) * scale
    s_sel = jnp.where(sel_ok[..., None], s_sel, -jnp.inf).reshape(b, tq, g, r, n_sel * MOBA_BLOCK)
    ko = jnp.take(kb, own, axis=2)
    vo = jnp.take(vb, own, axis=2)
    s_own = jnp.einsum('btgrd,bgtkd->btgrk', q, ko, preferred_element_type=F32) * scale
    own_pos = own[:, None] * MOBA_BLOCK + jnp.arange(MOBA_BLOCK)[None, :]
    s_own = jnp.where((own_pos <= qpos[:, None])[None, :, None, None, :], s_own, -jnp.inf)
    p = jax.nn.softmax(jnp.concatenate([s_sel, s_own], axis=-1), axis=-1)
    p_sel = p[..., :n_sel * MOBA_BLOCK].reshape(b, tq, g, r, n_sel, MOBA_BLOCK)
    p_own = p[..., n_sel * MOBA_BLOCK:]
    o = (jnp.einsum('btgrjk,btgrjkd->btgrd', p_sel, vs, preferred_element_type=F32)
         + jnp.einsum('btgrk,bgtkd->btgrd', p_own, vo, preferred_element_type=F32))
    return o.reshape(b, tq, g * r * dh)


def mixer_b(hp, hs, states, w_in, q_gain, k_gain, page_table):
    cache_k, cache_v = states
    tp, ts = hp.shape[1], hs.shape[1]
    qp, kp, vp, gp, _ = project_heads(hp, jnp.arange(tp), w_in, q_gain, k_gain, HQ_B, HKV_B, ())
    pos_s = PAST_LEN + jnp.arange(ts)
    qs, ks, vs, gs, _ = project_heads(hs, pos_s, w_in, q_gain, k_gain, HQ_B, HKV_B, ())
    kb, vb, km = moba_blocks(kp, vp)

    def step_p(args):
        qb, pb = args
        return moba_attend(qb, pb, kb, vb, km)

    o_p = from_blocks(lax.map(step_p, (to_blocks(qp, MOBA_Q_BLOCK), jnp.arange(tp).reshape(-1, MOBA_Q_BLOCK))))
    kc = jnp.concatenate([gather_pages(cache_k, page_table), ks], axis=1)
    vc = jnp.concatenate([gather_pages(cache_v, page_table), vs], axis=1)
    kbs, vbs, kms = moba_blocks(kc, vc)

    def step_s(args):
        qq, kk, vv, mm = args
        return moba_attend(qq[None], pos_s, kk[None], vv[None], mm[None])[0]

    o_s = lax.map(step_s, (qs, kbs, vbs, kms))
    new_state = [kp, ks, vp, vs]
    return (o_p.astype(hp.dtype) * jax.nn.silu(gp), o_s.astype(hs.dtype) * jax.nn.silu(gs), new_state)


def dsa_attend(q, q_idx, w_idx, qpos, kc, vc, kidx_c, n_sel):
    b = q.shape[0]
    l = kc.shape[1]
    dots = jnp.einsum('bthd,bsd->bths', q_idx, kidx_c, preferred_element_type=F32) * (IDX_DIM ** -0.5)
    score = jnp.einsum('bths,bth->bts', jax.nn.relu(dots), w_idx.astype(F32)) * (N_IDX_HEADS ** -0.5)
    admissible = jnp.arange(l)[None, :] <= qpos[:, None]
    score = jnp.where(admissible[None], score, -jnp.inf)
    _, sel = lax.top_k(score, n_sel)
    ok = sel <= qpos[None, :, None]
    b_ix = jnp.arange(b)[:, None, None]
    ks = kc[b_ix, sel]
    vs = vc[b_ix, sel]
    s = jnp.einsum('btgrd,btjgd->btgrj', q, ks, preferred_element_type=F32) * (HEAD_DIM ** -0.5)
    s = jnp.where(ok[:, :, None, None, :], s, -jnp.inf)
    p = jax.nn.softmax(s, axis=-1)
    o = jnp.einsum('btgrj,btjgd->btgrd', p, vs, preferred_element_type=F32)
    return o.reshape(o.shape[:2] + (-1,))


def project_c(h, pos, w_in, q_gain, k_gain):
    b, t, _ = h.shape
    q, k, v, gate, rest = project_heads(h, pos, w_in, q_gain, k_gain, HQ_C, HKV_C,
                                        (N_IDX_HEADS * IDX_DIM, IDX_DIM, N_IDX_HEADS))
    qi, ki, wi = rest
    qi = rope(qi.reshape(b, t, N_IDX_HEADS, IDX_DIM), pos)
    ki = rope(ki, pos)
    return q, k, v, gate, qi, ki, wi


def mixer_c(hp, hs, states, w_in, q_gain, k_gain, page_table):
    cache_k, cache_v, cache_kidx = states
    tp, ts = hp.shape[1], hs.shape[1]
    qp, kp, vp, gp, qip, kip, wip = project_c(hp, jnp.arange(tp), w_in, q_gain, k_gain)
    pos_s = PAST_LEN + jnp.arange(ts)
    qs, ks, vs, gs, qis, kis, wis = project_c(hs, pos_s, w_in, q_gain, k_gain)
    n_sel_p = min(DSA_TOPK, tp // 4)

    def step_p(args):
        qb, qib, wib, pb = args
        return dsa_attend(qb, qib, wib, pb, kp, vp, kip, n_sel_p)

    o_p = from_blocks(lax.map(step_p, (to_blocks(qp, DSA_Q_BLOCK), to_blocks(qip, DSA_Q_BLOCK),
                                       to_blocks(wip, DSA_Q_BLOCK), jnp.arange(tp).reshape(-1, DSA_Q_BLOCK))))
    kc = jnp.concatenate([gather_pages(cache_k, page_table), ks], axis=1)
    vc = jnp.concatenate([gather_pages(cache_v, page_table), vs], axis=1)
    kic = jnp.concatenate([gather_pages(cache_kidx, page_table), kis], axis=1)
    n_sel_s = min(DSA_TOPK, kc.shape[1] // 4)
    o_s = dsa_attend(qs, qis, wis, pos_s, kc, vc, kic, n_sel_s)
    new_state = [kp, ks, vp, vs, kip, kis]
    return (o_p.astype(hp.dtype) * jax.nn.silu(gp), o_s.astype(hs.dtype) * jax.nn.silu(gs), new_state)


def setup_inputs(seed: int = 0) -> dict:
    key = jax.random.key(seed)
    keys = iter(jax.random.split(key, 8 + 8 * DEPTH))

    def normal(shape, scale=1.0):
        return scale * jax.random.normal(next(keys), shape, F32)

    n_pages = PAST_LEN // PAGE_SIZE
    n_used = DEC_BATCH * n_pages
    n_pool = n_used + max(1, n_used // 4)
    inp = {'x_prompt': normal((BATCH, SEQ, D_MODEL)), 'x_sample': normal((DEC_BATCH, DEC_SEQ, D_MODEL))}
    for i in range(DEPTH):
        kind = i % N_MIXERS
        if kind == 0:
            for window, _ in DIL_GROUPS:
                inp[f'state_l{i}_kv_w{window}'] = normal((DEC_BATCH, min(window, PAST_LEN), 2, HKV_A, HEAD_DIM))
        elif kind == 1:
            inp[f'cache_l{i}_k'] = normal((n_pool, PAGE_SIZE, HKV_B, HEAD_DIM))
            inp[f'cache_l{i}_v'] = normal((n_pool, PAGE_SIZE, HKV_B, HEAD_DIM))
        else:
            inp[f'cache_l{i}_k'] = normal((n_pool, PAGE_SIZE, HKV_C, HEAD_DIM))
            inp[f'cache_l{i}_v'] = normal((n_pool, PAGE_SIZE, HKV_C, HEAD_DIM))
            inp[f'cache_l{i}_kidx'] = normal((n_pool, PAGE_SIZE, IDX_DIM))
    perm = jax.random.permutation(next(keys), n_pool)
    inp['page_table'] = perm[:n_used].reshape(DEC_BATCH, n_pages).astype(jnp.int32)
    for i in range(DEPTH):
        kind = i % N_MIXERS
        inp[f'l{i}_norm'] = 1.0 + normal((D_MODEL,), 0.02)
        inp[f'l{i}_w_in'] = normal((D_MODEL, IN_WIDTH[kind]), D_MODEL ** -0.5)
        inp[f'l{i}_q_norm'] = 1.0 + normal((HEAD_DIM,), 0.02)
        inp[f'l{i}_k_norm'] = 1.0 + normal((HEAD_DIM,), 0.02)
        inp[f'l{i}_w_out'] = normal((OUT_WIDTH[kind], D_MODEL), OUT_WIDTH[kind] ** -0.5)
    return inp


def reference(x_prompt, x_sample,
              state_l0_kv_w128, state_l0_kv_w512, state_l0_kv_w2048,
              cache_l1_k, cache_l1_v,
              cache_l2_k, cache_l2_v, cache_l2_kidx,
              state_l3_kv_w128, state_l3_kv_w512, state_l3_kv_w2048,
              page_table,
              l0_norm, l0_w_in, l0_q_norm, l0_k_norm, l0_w_out,
              l1_norm, l1_w_in, l1_q_norm, l1_k_norm, l1_w_out,
              l2_norm, l2_w_in, l2_q_norm, l2_k_norm, l2_w_out,
              l3_norm, l3_w_in, l3_q_norm, l3_k_norm, l3_w_out):
    layer_states = (
        (state_l0_kv_w128, state_l0_kv_w512, state_l0_kv_w2048),
        (cache_l1_k, cache_l1_v),
        (cache_l2_k, cache_l2_v, cache_l2_kidx),
        (state_l3_kv_w128, state_l3_kv_w512, state_l3_kv_w2048),
    )
    layer_params = (
        (l0_norm, l0_w_in, l0_q_norm, l0_k_norm, l0_w_out),
        (l1_norm, l1_w_in, l1_q_norm, l1_k_norm, l1_w_out),
        (l2_norm, l2_w_in, l2_q_norm, l2_k_norm, l2_w_out),
        (l3_norm, l3_w_in, l3_q_norm, l3_k_norm, l3_w_out),
    )
    xp, xs = x_prompt, x_sample
    new_state = []
    for i in range(DEPTH):
        norm_g, w_in, q_gain, k_gain, w_out = layer_params[i]
        hp = rms_norm(xp, norm_g)
        hs = rms_norm(xs, norm_g)
        kind = i % N_MIXERS
        if kind == 0:
            bp, bs, st = mixer_a(hp, hs, layer_states[i], w_in, q_gain, k_gain)
        elif kind == 1:
            bp, bs, st = mixer_b(hp, hs, layer_states[i], w_in, q_gain, k_gain, page_table)
        else:
            bp, bs, st = mixer_c(hp, hs, layer_states[i], w_in, q_gain, k_gain, page_table)
        xp = xp + jnp.einsum('btf,fd->btd', bp, w_out).astype(xp.dtype)
        xs = xs + jnp.einsum('btf,fd->btd', bs, w_out).astype(xs.dtype)
        new_state.extend(st)
    return (xp, xs, *new_state)
```

```python
import functools

import jax
import jax.numpy as jnp
from jax import lax
from jax.experimental import pallas as pl
from jax.experimental.pallas import tpu as pltpu

F32 = jnp.float32
BF16 = jnp.bfloat16
I32 = jnp.int32

D_MODEL = 1024
PAST_LEN = 2048
PAGE_SIZE = 128
HEAD_DIM = 64
ROPE_THETA = 10000.0
NORM_EPS = 1e-6

DIL_GROUPS = ((128, 1), (512, 4), (2048, 16))
HQ_A, HKV_A = 8, 2
Q_A, KV_A, OUT_A = 1536, 384, 512

HQ_B, HKV_B = 16, 4
MOBA_BLOCK, MOBA_TOPK = 256, 3
Q_B, KV_B, OUT_B = 1024, 256, 1024

HQ_C, HKV_C = 16, 4
N_IDX_HEADS, IDX_DIM, DSA_TOPK = 8, 64, 256
Q_C, KV_C, OUT_C = 1024, 256, 1024
IN_C = Q_C + 2 * KV_C + OUT_C + N_IDX_HEADS * IDX_DIM + IDX_DIM + N_IDX_HEADS
IN_C_PAD = 3200

LANES = 128
PROJ_ROWS = 256
Q_TILE = 128
DSA_CHUNK = 512
VMEM_LIMIT = 56 * 1024 * 1024
NEG_BIG = -1e30
INT_MIN = -2147483648

_NT = (((1,), (1,)), ((), ()))


def _params(sem):
    return pltpu.CompilerParams(dimension_semantics=sem, vmem_limit_bytes=VMEM_LIMIT)


def _group_sum_matrix():
    r = lax.broadcasted_iota(I32, (LANES, LANES), 0) // HEAD_DIM
    c = lax.broadcasted_iota(I32, (LANES, LANES), 1) // HEAD_DIM
    return (r == c).astype(BF16)


def _rope(y, cos, sin):
    lane = lax.broadcasted_iota(I32, y.shape, 1)
    first_half = (lane & (HEAD_DIM // 2)) == 0
    partner = jnp.where(first_half, pltpu.roll(y, LANES - HEAD_DIM // 2, 1), pltpu.roll(y, HEAD_DIM // 2, 1))
    return y * cos + partner * sin


def _head_norm(x, gain, gmat):
    ss = x * x
    hi = ss.astype(BF16)
    lo = (ss - hi.astype(F32)).astype(BF16)
    gs = jnp.dot(hi, gmat, preferred_element_type=F32) + jnp.dot(lo, gmat, preferred_element_type=F32)
    return x * lax.rsqrt(gs * (1.0 / HEAD_DIM) + NORM_EPS) * gain


def _proj_body(kind, with_kmean, x_ref, g_ref, w_ref, cos_ref, sin_ref, qg_ref, kg_ref, *outs):
    x = x_ref[...]
    ms = jnp.mean(x * x, axis=-1, keepdims=True)
    h = (x * lax.rsqrt(ms + NORM_EPS) * g_ref[...]).astype(BF16)
    cos = cos_ref[...]
    sin = sin_ref[...]
    qg = qg_ref[...]
    kg = kg_ref[...]
    gmat = _group_sum_matrix()
    qw, kvw, gw = {"a": (Q_A, KV_A, OUT_A), "b": (Q_B, KV_B, OUT_B), "c": (Q_C, KV_C, OUT_C)}[kind]
    q_ref, k_ref, v_ref, k16_ref, v16_ref, sg_ref = outs[:6]
    rest = outs[6:]

    def seg(start, width):
        return jnp.dot(h, w_ref[:, start:start + width], preferred_element_type=F32)

    zq = seg(0, qw)
    for c in range(qw // LANES):
        y = _rope(_head_norm(zq[:, c * LANES:(c + 1) * LANES], qg, gmat), cos, sin)
        q_ref[:, c * LANES:(c + 1) * LANES] = y * (HEAD_DIM ** -0.5)
    zk = seg(qw, kvw)
    ksum = []
    for c in range(kvw // LANES):
        y = _rope(_head_norm(zk[:, c * LANES:(c + 1) * LANES], kg, gmat), cos, sin)
        k_ref[:, c * LANES:(c + 1) * LANES] = y
        k16_ref[:, c * LANES:(c + 1) * LANES] = y.astype(BF16)
        if with_kmean:
            ksum.append(jnp.sum(y, axis=0, keepdims=True) * (1.0 / MOBA_BLOCK))
    zv = seg(qw + kvw, kvw)
    v_ref[...] = zv
    v16_ref[...] = zv.astype(BF16)
    zg = seg(qw + 2 * kvw, gw)
    sg_ref[...] = zg / (1.0 + jnp.exp(-zg))
    nrest = 0
    if kind == "c":
        qi_ref, kiwi_ref, kiwi16_ref = rest[:3]
        nrest = 3
        base = qw + 2 * kvw + gw
        zi = seg(base, N_IDX_HEADS * IDX_DIM)
        for c in range(N_IDX_HEADS * IDX_DIM // LANES):
            qi_ref[:, c * LANES:(c + 1) * LANES] = _rope(zi[:, c * LANES:(c + 1) * LANES], cos, sin)
        zz = seg(base + N_IDX_HEADS * IDX_DIM, LANES)
        lane = lax.broadcasted_iota(I32, zz.shape, 1)
        kiwi = jnp.where(lane < IDX_DIM, _rope(zz, cos, sin), zz)
        kiwi_ref[...] = kiwi
        kiwi16_ref[...] = kiwi.astype(BF16)
    if with_kmean:
        km_ref = rest[nrest]
        for c in range(kvw // LANES):
            km_ref[0, :, c * LANES:(c + 1) * LANES] = ksum[c]


def _project(kind, x, gain, w16, cos, sin, qg, kg, with_kmean=False):
    n = x.shape[0]
    tm = PROJ_ROWS
    nt = cos.shape[0] // tm
    qw, kvw, gw = {"a": (Q_A, KV_A, OUT_A), "b": (Q_B, KV_B, OUT_B), "c": (Q_C, KV_C, OUT_C)}[kind]
    wp = w16.shape[1]

    def rows(width):
        return pl.BlockSpec((tm, width), lambda i: (i, 0))

    def const(shape):
        return pl.BlockSpec(shape, lambda i: (0,) * len(shape))

    out_shape = [jax.ShapeDtypeStruct((n, qw), F32), jax.ShapeDtypeStruct((n, kvw), F32),
                 jax.ShapeDtypeStruct((n, kvw), F32), jax.ShapeDtypeStruct((n, kvw), BF16),
                 jax.ShapeDtypeStruct((n, kvw), BF16), jax.ShapeDtypeStruct((n, gw), F32)]
    out_specs = [rows(qw), rows(kvw), rows(kvw), rows(kvw), rows(kvw), rows(gw)]
    if kind == "c":
        out_shape += [jax.ShapeDtypeStruct((n, N_IDX_HEADS * IDX_DIM), F32),
                      jax.ShapeDtypeStruct((n, LANES), F32), jax.ShapeDtypeStruct((n, LANES), BF16)]
        out_specs += [rows(N_IDX_HEADS * IDX_DIM), rows(LANES), rows(LANES)]
    if with_kmean:
        out_shape.append(jax.ShapeDtypeStruct((n // tm, 1, kvw), F32))
        out_specs.append(pl.BlockSpec((1, 1, kvw), lambda i: (i, 0, 0)))
    return pl.pallas_call(
        functools.partial(_proj_body, kind, with_kmean),
        grid=(n // tm,),
        in_specs=[rows(D_MODEL), const((1, D_MODEL)), const((D_MODEL, wp)),
                  pl.BlockSpec((tm, LANES), lambda i: (i % nt, 0)),
                  pl.BlockSpec((tm, LANES), lambda i: (i % nt, 0)),
                  const((1, LANES)), const((1, LANES))],
        out_specs=out_specs,
        out_shape=out_shape,
        compiler_params=_params(("parallel",)),
        name=f"proj_{kind}",
    )(x, gain, w16, cos, sin, qg, kg)


def _outproj_body(merge, *refs):
    if merge:
        o0, o1, o2, l0, l1, l2, sg_ref, x_ref, w_ref, y_ref = refs
        a, b, c = l0[...], l1[...], l2[...]
        m = jnp.maximum(jnp.maximum(a, b), c)
        ea, eb, ec = jnp.exp(a - m), jnp.exp(b - m), jnp.exp(c - m)
        o = (ea * o0[...] + eb * o1[...] + ec * o2[...]) / (ea + eb + ec)
    else:
        o_ref, sg_ref, x_ref, w_ref, y_ref = refs
        o = o_ref[...]
    g = (o * sg_ref[...]).astype(BF16)
    y_ref[...] = x_ref[...] + jnp.dot(g, w_ref[...], preferred_element_type=F32)


def _out_project(os_, lses, sg, x, w16):
    n = x.shape[0]
    f = sg.shape[1]
    tm = PROJ_ROWS
    merge = lses is not None
    ins = list(os_) + (list(lses) if merge else []) + [sg, x, w16]
    rows_f = pl.BlockSpec((tm, f), lambda i: (i, 0))
    in_specs = [rows_f] * (len(ins) - 2) + [pl.BlockSpec((tm, D_MODEL), lambda i: (i, 0)),
                                             pl.BlockSpec((f, D_MODEL), lambda i: (0, 0))]
    return pl.pallas_call(
        functools.partial(_outproj_body, merge),
        grid=(n // tm,),
        in_specs=in_specs,
        out_specs=pl.BlockSpec((tm, D_MODEL), lambda i: (i, 0)),
        out_shape=jax.ShapeDtypeStruct((n, D_MODEL), F32),
        compiler_params=_params(("parallel",)),
        name="out_proj_merge" if merge else "out_proj",
    )(*ins)


def _stack_heads(q, kvh, rep):
    return jnp.concatenate([q[:, (kvh * rep + r) * HEAD_DIM:(kvh * rep + r + 1) * HEAD_DIM] for r in range(rep)],
                           axis=0)


def _softmax_step(s, m, l, acc, v16):
    m_new = jnp.maximum(m, jnp.max(s, axis=1, keepdims=True))
    alpha = jnp.exp(m - m_new)
    p = jnp.exp(s - m_new)
    l_new = alpha * l + jnp.sum(p, axis=1, keepdims=True)
    acc_new = alpha * acc + jnp.dot(p.astype(BF16), v16, preferred_element_type=F32)
    return m_new, l_new, acc_new


def _dil_prompt_body(q_ref, kp_ref, kc_ref, vp_ref, vc_ref, o_ref, lse_ref):
    j = pl.program_id(2)
    tq = Q_TILE
    rep = HQ_A // HKV_A
    q = q_ref[0].astype(BF16)
    kk = jnp.concatenate([kp_ref[0], kc_ref[0]], axis=0)
    vv = jnp.concatenate([vp_ref[0], vc_ref[0]], axis=0)
    row = lax.broadcasted_iota(I32, (tq, 2 * tq), 0)
    col = lax.broadcasted_iota(I32, (tq, 2 * tq), 1)
    dist = row + tq - col
    valid = (dist >= 0) & (dist <= tq) & ((col >= tq) | (j > 0))
    valid = jnp.concatenate([valid] * rep, axis=0)
    for kvh in range(HKV_A):
        kh = kk[:, kvh * HEAD_DIM:(kvh + 1) * HEAD_DIM]
        vh = vv[:, kvh * HEAD_DIM:(kvh + 1) * HEAD_DIM]
        qh = _stack_heads(q, kvh, rep)
        s = lax.dot_general(qh, kh, _NT, preferred_element_type=F32)
        s = jnp.where(valid, s, -jnp.inf)
        m = jnp.max(s, axis=1, keepdims=True)
        p = jnp.exp(s - m)
        l = jnp.sum(p, axis=1, keepdims=True)
        o = jnp.dot(p.astype(BF16), vh, preferred_element_type=F32) / l
        lse = m + jnp.log(l)
        for r in range(rep):
            hd = (kvh * rep + r) * HEAD_DIM
            o_ref[0, :, hd:hd + HEAD_DIM] = o[r * tq:(r + 1) * tq]
            lse_ref[0, :, hd:hd + HEAD_DIM] = jnp.broadcast_to(lse[r * tq:(r + 1) * tq], (tq, HEAD_DIM))


def _dil_prompt(q, k16, v16, g, dil):
    b, t, _ = q.shape
    tr = t // dil
    qv = q.reshape(b, tr, dil * Q_A)
    kv = k16.reshape(b, tr, dil * KV_A)
    vv = v16.reshape(b, tr, dil * KV_A)
    ng = len(DIL_GROUPS)
    hw = HKV_A * HEAD_DIM
    cur = lambda bb, r, j: (bb, j, r * ng + g)
    prev = lambda bb, r, j: (bb, jnp.maximum(j - 1, 0), r * ng + g)
    o, lse = pl.pallas_call(
        _dil_prompt_body,
        grid=(b, dil, tr // Q_TILE),
        in_specs=[pl.BlockSpec((1, Q_TILE, OUT_A), cur),
                  pl.BlockSpec((1, Q_TILE, hw), prev), pl.BlockSpec((1, Q_TILE, hw), cur),
                  pl.BlockSpec((1, Q_TILE, hw), prev), pl.BlockSpec((1, Q_TILE, hw), cur)],
        out_specs=[pl.BlockSpec((1, Q_TILE, OUT_A), lambda bb, r, j: (bb, j, r))] * 2,
        out_shape=[jax.ShapeDtypeStruct((b, tr, dil * OUT_A), F32)] * 2,
        compiler_params=_params(("parallel", "parallel", "arbitrary")),
        name=f"dil_prompt_g{g}",
    )(qv, kv, kv, vv, vv)
    return o.reshape(b * t, OUT_A), lse.reshape(b * t, OUT_A)


def _dil_sample_body(win, dil, sb, q_ref, kn_ref, vn_ref, st_ref, o_ref, lse_ref, ns_ref, ctx_ref):
    rep = HQ_A // HKV_A
    hw = HKV_A * HEAD_DIM
    ts = q_ref.shape[1]
    lc = win + LANES
    ctx_ref[pl.ds(win, LANES), :] = jnp.zeros((LANES, 2 * hw), F32)
    row = lax.broadcasted_iota(I32, (rep * ts, lc), 0)
    col = lax.broadcasted_iota(I32, (rep * ts, lc), 1)
    dist = win + (row & (ts - 1)) - col
    valid = (dist >= 0) & (dist <= win) & ((dist & (dil - 1)) == 0)
    for s_i in range(sb):
        ctx_ref[pl.ds(0, win), :] = st_ref[s_i]
        ctx_ref[pl.ds(win, ts), 0:hw] = kn_ref[s_i]
        ctx_ref[pl.ds(win, ts), hw:2 * hw] = vn_ref[s_i]
        ns_ref[s_i] = ctx_ref[pl.ds(ts, win), :]
        kc = ctx_ref[:, 0:hw].astype(BF16)
        vc = ctx_ref[:, hw:2 * hw].astype(BF16)
        q = q_ref[s_i].astype(BF16)
        for kvh in range(HKV_A):
            kh = kc[:, kvh * HEAD_DIM:(kvh + 1) * HEAD_DIM]
            vh = vc[:, kvh * HEAD_DIM:(kvh + 1) * HEAD_DIM]
            qh = _stack_heads(q, kvh, rep)
            s = lax.dot_general(qh, kh, _NT, preferred_element_type=F32)
            s = jnp.where(valid, s, -jnp.inf)
            m = jnp.max(s, axis=1, keepdims=True)
            p = jnp.exp(s - m)
            l = jnp.sum(p, axis=1, keepdims=True)
            o = jnp.dot(p.astype(BF16), vh, preferred_element_type=F32) / l
            lse = m + jnp.log(l)
            for r in range(rep):
                hd = (kvh * rep + r) * HEAD_DIM
                o_ref[s_i, :, hd:hd + HEAD_DIM] = o[r * ts:(r + 1) * ts]
                lse_ref[s_i, :, hd:hd + HEAD_DIM] = jnp.broadcast_to(lse[r * ts:(r + 1) * ts], (ts, HEAD_DIM))


def _dil_sample(q, kn, vn, state, g, win, dil):
    db, ts, _ = q.shape
    hw = HKV_A * HEAD_DIM
    st = state.reshape(db, win, 2 * hw)
    sb = max(1, min(8, 1024 // win))
    o, lse, ns = pl.pallas_call(
        functools.partial(_dil_sample_body, win, dil, sb),
        grid=(db // sb,),
        in_specs=[pl.BlockSpec((sb, ts, OUT_A), lambda i: (i, 0, g)),
                  pl.BlockSpec((sb, ts, hw), lambda i: (i, 0, g)),
                  pl.BlockSpec((sb, ts, hw), lambda i: (i, 0, g)),
                  pl.BlockSpec((sb, win, 2 * hw), lambda i: (i, 0, 0))],
        out_specs=[pl.BlockSpec((sb, ts, OUT_A), lambda i: (i, 0, 0)),
                   pl.BlockSpec((sb, ts, OUT_A), lambda i: (i, 0, 0)),
                   pl.BlockSpec((sb, win, 2 * hw), lambda i: (i, 0, 0))],
        out_shape=[jax.ShapeDtypeStruct((db, ts, OUT_A), F32), jax.ShapeDtypeStruct((db, ts, OUT_A), F32),
                   jax.ShapeDtypeStruct((db, win, 2 * hw), F32)],
        scratch_shapes=[pltpu.VMEM((win + LANES, 2 * hw), F32)],
        compiler_params=_params(("arbitrary",)),
        name=f"dil_sample_g{g}",
    )(q, kn, vn, st)
    return o.reshape(db * ts, OUT_A), lse.reshape(db * ts, OUT_A), ns.reshape(state.shape)


def _top_blocks(bs, n_valid, topk):
    lane = lax.broadcasted_iota(I32, bs.shape, 1)
    nb = bs.shape[1]
    work = jnp.where(lane < n_valid, bs, -jnp.inf)
    chosen = jnp.zeros(bs.shape, jnp.bool_)
    for _ in range(topk):
        m = jnp.max(work, axis=1, keepdims=True)
        idx = jnp.min(jnp.where(work == m, lane, nb), axis=1, keepdims=True)
        pick = lane == idx
        chosen = chosen | pick
        work = jnp.where(pick, -jnp.inf, work)
    return chosen & (lane < n_valid)


def _moba_prompt_body(q_ref, k_ref, v_ref, km_ref, o_ref):
    i = pl.program_id(1)
    tq = Q_TILE
    rep = HQ_B // HKV_B
    nb = km_ref.shape[1]
    own = (i * tq) // MOBA_BLOCK
    q = q_ref[0]
    km = km_ref[0]
    own0 = pl.multiple_of(own * MOBA_BLOCK, MOBA_BLOCK)
    row = lax.broadcasted_iota(I32, (rep * tq, MOBA_BLOCK), 0)
    col = lax.broadcasted_iota(I32, (rep * tq, MOBA_BLOCK), 1)
    causal = own0 + col <= i * tq + (row & (tq - 1))
    blk_row = lax.broadcasted_iota(I32, (nb, MOBA_BLOCK), 0)
    for kvh in range(HKV_B):
        hs = slice(kvh * HEAD_DIM, (kvh + 1) * HEAD_DIM)
        q4 = _stack_heads(q, kvh, rep)
        bs = lax.dot_general(q4, km[:, hs], _NT, precision=lax.Precision.HIGHEST, preferred_element_type=F32)
        chosen = _top_blocks(bs, own, MOBA_TOPK)
        selb = jnp.where(chosen, 0.0, NEG_BIG).astype(BF16)
        q16 = q4.astype(BF16)
        s = lax.dot_general(q16, k_ref[0, pl.ds(own0, MOBA_BLOCK), hs], _NT, preferred_element_type=F32)
        s = jnp.where(causal, s, -jnp.inf)
        m = jnp.max(s, axis=1, keepdims=True)
        p = jnp.exp(s - m)
        l = jnp.sum(p, axis=1, keepdims=True)
        acc = jnp.dot(p.astype(BF16), v_ref[0, pl.ds(own0, MOBA_BLOCK), hs], preferred_element_type=F32)

        def body(n, carry):
            m, l, acc = carry
            n0 = pl.multiple_of(n * MOBA_BLOCK, MOBA_BLOCK)
            onehot = (blk_row == n).astype(BF16)
            s = (lax.dot_general(q16, k_ref[0, pl.ds(n0, MOBA_BLOCK), hs], _NT, preferred_element_type=F32)
                 + jnp.dot(selb, onehot, preferred_element_type=F32))
            return _softmax_step(s, m, l, acc, v_ref[0, pl.ds(n0, MOBA_BLOCK), hs])

        m, l, acc = lax.fori_loop(0, own, body, (m, l, acc))
        o = acc / l
        for r in range(rep):
            hd = (kvh * rep + r) * HEAD_DIM
            o_ref[0, :, hd:hd + HEAD_DIM] = o[r * tq:(r + 1) * tq]


def _moba_prompt(q, k16, v16, kmean):
    b, t, _ = q.shape
    nb = t // MOBA_BLOCK
    o = pl.pallas_call(
        _moba_prompt_body,
        grid=(b, t // Q_TILE),
        in_specs=[pl.BlockSpec((1, Q_TILE, Q_B), lambda bb, i: (bb, i, 0)),
                  pl.BlockSpec((1, t, KV_B), lambda bb, i: (bb, 0, 0)),
                  pl.BlockSpec((1, t, KV_B), lambda bb, i: (bb, 0, 0)),
                  pl.BlockSpec((1, nb, KV_B), lambda bb, i: (bb, 0, 0))],
        out_specs=pl.BlockSpec((1, Q_TILE, OUT_B), lambda bb, i: (bb, i, 0)),
        out_shape=jax.ShapeDtypeStruct((b, t, OUT_B), F32),
        compiler_params=_params(("parallel", "arbitrary")),
        name="moba_prompt",
    )(q, k16, v16, kmean)
    return o.reshape(b * t, OUT_B)


def _block_diag_queries(q, n_kv, rep):
    ts = q.shape[0]
    rows = []
    for kvh in range(n_kv):
        for r in range(rep):
            hd = (kvh * rep + r) * HEAD_DIM
            parts = []
            if kvh > 0:
                parts.append(jnp.zeros((ts, kvh * HEAD_DIM), q.dtype))
            parts.append(q[:, hd:hd + HEAD_DIM])
            if kvh < n_kv - 1:
                parts.append(jnp.zeros((ts, (n_kv - 1 - kvh) * HEAD_DIM), q.dtype))
            rows.append(jnp.concatenate(parts, axis=1))
    return jnp.concatenate(rows, axis=0)


def _unstack_block_diag(o_all, n_kv, rep, ts):
    parts = []
    for kvh in range(n_kv):
        for r in range(rep):
            r0 = (kvh * rep + r) * ts
            parts.append(o_all[r0:r0 + ts, kvh * HEAD_DIM:(kvh + 1) * HEAD_DIM])
    return jnp.concatenate(parts, axis=1)


def _pad_rows(x, rows):
    return jnp.concatenate([x, jnp.zeros((rows - x.shape[0], x.shape[1]), x.dtype)], axis=0)


def _moba_sample_body(n_pages, pt_ref, q_ref, kn_ref, vn_ref, *refs):
    kp = refs[:n_pages]
    vp = refs[n_pages:2 * n_pages]
    o_ref = refs[2 * n_pages]
    rep = HQ_B // HKV_B
    ts = q_ref.shape[1]
    nrows = HQ_B * ts
    ppb = MOBA_BLOCK // PAGE_SIZE
    n_past = n_pages // ppb
    qf = _block_diag_queries(q_ref[0], HKV_B, rep)
    qf16 = qf.astype(BF16)
    kms = []
    for n in range(n_past):
        tot = kp[n * ppb][0]
        for j in range(1, ppb):
            tot = tot + kp[n * ppb + j][0]
        kms.append(jnp.sum(tot, axis=0, keepdims=True) * (1.0 / MOBA_BLOCK))
    km = _pad_rows(jnp.concatenate(kms, axis=0), LANES)
    bs = lax.dot_general(qf, km, _NT, precision=lax.Precision.HIGHEST, preferred_element_type=F32)
    chosen = _top_blocks(bs, n_past, MOBA_TOPK)
    selb = jnp.where(chosen, 0.0, NEG_BIG)
    bias = jnp.concatenate([jnp.broadcast_to(selb[:, n:n + 1], (nrows, MOBA_BLOCK)) for n in range(n_past)], axis=1)
    s_c = jnp.concatenate([lax.dot_general(qf16, kp[p][0].astype(BF16), _NT, preferred_element_type=F32)
                           for p in range(n_pages)], axis=1) + bias
    kn16 = _pad_rows(kn_ref[0], LANES).astype(BF16)
    vn16 = _pad_rows(vn_ref[0], LANES).astype(BF16)
    s_n = lax.dot_general(qf16, kn16, _NT, preferred_element_type=F32)
    row = lax.broadcasted_iota(I32, (nrows, LANES), 0)
    col = lax.broadcasted_iota(I32, (nrows, LANES), 1)
    s_n = jnp.where(col <= (row & (ts - 1)), s_n, -jnp.inf)
    m = jnp.maximum(jnp.max(s_c, axis=1, keepdims=True), jnp.max(s_n, axis=1, keepdims=True))
    p_c = jnp.exp(s_c - m)
    p_n = jnp.exp(s_n - m)
    l = jnp.sum(p_c, axis=1, keepdims=True) + jnp.sum(p_n, axis=1, keepdims=True)
    acc = jnp.dot(p_n.astype(BF16), vn16, preferred_element_type=F32)
    p16 = p_c.astype(BF16)
    for p in range(n_pages):
        acc = acc + jnp.dot(p16[:, p * PAGE_SIZE:(p + 1) * PAGE_SIZE], vp[p][0].astype(BF16),
                            preferred_element_type=F32)
    o_ref[0] = _unstack_block_diag(acc / l, HKV_B, rep, ts)


def _page_specs(n_pages, width):
    return [pl.BlockSpec((1, PAGE_SIZE, width), functools.partial(lambda p, i, pt: (pt[i * n_pages + p], 0, 0), p))
            for p in range(n_pages)]


def _moba_sample(q, kn, vn, cache_k, cache_v, page_table):
    db, ts, _ = q.shape
    n_pages = page_table.shape[1]
    n_pool = cache_k.shape[0]
    ck = cache_k.reshape(n_pool, PAGE_SIZE, KV_B)
    cv = cache_v.reshape(n_pool, PAGE_SIZE, KV_B)
    seq = lambda w: pl.BlockSpec((1, ts, w), lambda i, pt: (i, 0, 0))
    o = pl.pallas_call(
        functools.partial(_moba_sample_body, n_pages),
        grid_spec=pltpu.PrefetchScalarGridSpec(
            num_scalar_prefetch=1,
            grid=(db,),
            in_specs=[seq(Q_B), seq(KV_B), seq(KV_B)] + _page_specs(n_pages, KV_B) * 2,
            out_specs=seq(OUT_B),
        ),
        out_shape=jax.ShapeDtypeStruct((db, ts, OUT_B), F32),
        compiler_params=_params(("arbitrary",)),
        name="moba_sample",
    )(page_table.reshape(-1), q, kn, vn, *([ck] * n_pages), *([cv] * n_pages))
    return o.reshape(db * ts, OUT_B)


def _sortable(score):
    bits = pltpu.bitcast(score + 0.0, I32)
    return jnp.where(bits >= 0, bits, bits ^ 0x7FFFFFFF)


def _kth_largest_key(count_ge, rows, k):
    def step(b, t):
        cand = jnp.where(b == 0, jnp.zeros_like(t), t | jnp.left_shift(jnp.int32(1), 31 - b))
        return jnp.where(count_ge(cand) >= k, cand, t)

    return lax.fori_loop(0, 32, step, jnp.full((rows, 1), INT_MIN, I32))


def _index_scores(qi8, wv, ki16, rows):
    dots = lax.dot_general(qi8, ki16, _NT, preferred_element_type=F32)
    sc = jnp.maximum(dots[0:rows], 0.0) * wv[:, 0:1]
    for h in range(1, N_IDX_HEADS):
        sc = sc + jnp.maximum(dots[h * rows:(h + 1) * rows], 0.0) * wv[:, h:h + 1]
    return sc


def _dsa_prompt_body(q_ref, k_ref, v_ref, qi_ref, w_ref, ki_ref, o_ref, key_ref, bias_ref, j_ref):
    i = pl.program_id(1)
    tq = Q_TILE
    ck = DSA_CHUNK
    rep = HQ_C // HKV_C
    t_all = key_ref.shape[1]
    nch = ((i + 1) * tq + ck - 1) // ck
    qpos = i * tq + lax.broadcasted_iota(I32, (tq, 1), 0)
    qi = qi_ref[0].astype(BF16)
    qi8 = jnp.concatenate([qi[:, h * IDX_DIM:(h + 1) * IDX_DIM] for h in range(N_IDX_HEADS)], axis=0)
    wv = w_ref[0][:, IDX_DIM:IDX_DIM + N_IDX_HEADS] * (IDX_DIM ** -0.5 * N_IDX_HEADS ** -0.5)
    lane_c = lax.broadcasted_iota(I32, (tq, ck), 1)

    def scores(c, _):
        off = pl.multiple_of(c * ck, ck)
        sc = _index_scores(qi8, wv, ki_ref[0, pl.ds(off, ck), 0:IDX_DIM], tq)
        key_ref[:, pl.ds(off, ck)] = jnp.where(off + lane_c <= qpos, _sortable(sc), INT_MIN)
        return 0

    lax.fori_loop(0, nch, scores, 0)

    def fold(x):
        out = x[:, 0:LANES]
        for a in range(1, ck // LANES):
            out = out + x[:, a * LANES:(a + 1) * LANES]
        return out

    def count(pred):
        def body(c, acc):
            off = pl.multiple_of(c * ck, ck)
            return acc + fold(pred(key_ref[:, pl.ds(off, ck)], off + lane_c).astype(I32))
        return jnp.sum(lax.fori_loop(0, nch, body, jnp.zeros((tq, LANES), I32)), axis=1, keepdims=True)

    thr = _kth_largest_key(lambda cand: count(lambda kx, cx: kx >= cand), tq, DSA_TOPK)
    real = thr > INT_MIN
    need = DSA_TOPK - count(lambda kx, cx: kx > thr)
    n_tie = count(lambda kx, cx: (kx == thr) & real)
    j_ref[...] = jnp.full((tq, 1), t_all, I32)

    @pl.when(jnp.max(jnp.where(n_tie > need, 1, 0)) > 0)
    def _():
        def step(b, jc):
            cand = jc + jnp.left_shift(jnp.int32(1), 13 - b)
            below = count(lambda kx, cx: (kx == thr) & real & (cx < cand))
            return jnp.where(below < need, cand, jc)
        j_ref[...] = lax.fori_loop(0, 14, step, jnp.zeros((tq, 1), I32))

    jmax = j_ref[...]

    def write_bias(c, _):
        off = pl.multiple_of(c * ck, ck)
        kx = key_ref[:, pl.ds(off, ck)]
        keep = (kx > thr) | ((kx == thr) & real & (off + lane_c <= jmax))
        bias_ref[:, pl.ds(off, ck)] = jnp.where(keep, 0.0, NEG_BIG)
        return 0

    lax.fori_loop(0, nch, write_bias, 0)

    q = q_ref[0].astype(BF16)
    for kvh in range(HKV_C):
        hs = slice(kvh * HEAD_DIM, (kvh + 1) * HEAD_DIM)
        q4 = _stack_heads(q, kvh, rep)

        def body(c, carry):
            off = pl.multiple_of(c * ck, ck)
            b1 = bias_ref[:, pl.ds(off, ck)]
            s = (lax.dot_general(q4, k_ref[0, pl.ds(off, ck), hs], _NT, preferred_element_type=F32)
                 + jnp.concatenate([b1] * rep, axis=0))
            return _softmax_step(s, *carry, v_ref[0, pl.ds(off, ck), hs])

        init = (jnp.full((rep * tq, 1), NEG_BIG, F32), jnp.zeros((rep * tq, 1), F32),
                jnp.zeros((rep * tq, HEAD_DIM), F32))
        m, l, acc = lax.fori_loop(0, nch, body, init)
        o = acc / l
        for r in range(rep):
            hd = (kvh * rep + r) * HEAD_DIM
            o_ref[0, :, hd:hd + HEAD_DIM] = o[r * tq:(r + 1) * tq]


def _dsa_prompt(q, k16, v16, qi, kiwi, kiwi16):
    b, t, _ = q.shape
    o = pl.pallas_call(
        _dsa_prompt_body,
        grid=(b, t // Q_TILE),
        in_specs=[pl.BlockSpec((1, Q_TILE, Q_C), lambda bb, i: (bb, i, 0)),
                  pl.BlockSpec((1, t, KV_C), lambda bb, i: (bb, 0, 0)),
                  pl.BlockSpec((1, t, KV_C), lambda bb, i: (bb, 0, 0)),
                  pl.BlockSpec((1, Q_TILE, N_IDX_HEADS * IDX_DIM), lambda bb, i: (bb, i, 0)),
                  pl.BlockSpec((1, Q_TILE, LANES), lambda bb, i: (bb, i, 0)),
                  pl.BlockSpec((1, t, LANES), lambda bb, i: (bb, 0, 0))],
        out_specs=pl.BlockSpec((1, Q_TILE, OUT_C), lambda bb, i: (bb, i, 0)),
        out_shape=jax.ShapeDtypeStruct((b, t, OUT_C), F32),
        scratch_shapes=[pltpu.VMEM((Q_TILE, t), I32), pltpu.VMEM((Q_TILE, t), F32), pltpu.VMEM((Q_TILE, 1), I32)],
        compiler_params=_params(("parallel", "arbitrary")),
        name="dsa_prompt",
    )(q, k16, v16, qi, kiwi, kiwi16)
    return o.reshape(b * t, OUT_C)


def _dsa_sample_body(n_pages, pt_ref, q_ref, kn_ref, vn_ref, qi_ref, kiwi_ref, *refs):
    kp = refs[:n_pages]
    vp = refs[n_pages:2 * n_pages]
    ip = refs[2 * n_pages:3 * n_pages]
    o_ref = refs[3 * n_pages]
    rep = HQ_C // HKV_C
    ts = q_ref.shape[1]
    nrows = HQ_C * ts
    n_cache = n_pages * PAGE_SIZE
    kiwi = kiwi_ref[0]
    qi = qi_ref[0].astype(BF16)
    qi8 = jnp.concatenate([qi[:, h * IDX_DIM:(h + 1) * IDX_DIM] for h in range(N_IDX_HEADS)], axis=0)
    wv = kiwi[:, IDX_DIM:IDX_DIM + N_IDX_HEADS] * (IDX_DIM ** -0.5 * N_IDX_HEADS ** -0.5)
    ki_all = jnp.concatenate([ip[p][0].astype(BF16) for p in range(n_pages)]
                             + [_pad_rows(kiwi[:, 0:IDX_DIM], LANES).astype(BF16)], axis=0)
    sc = _index_scores(qi8, wv, ki_all, ts)
    col = lax.broadcasted_iota(I32, sc.shape, 1)
    trow = lax.broadcasted_iota(I32, sc.shape, 0)
    adm = (col < n_cache) | (col - n_cache <= trow)
    keys = jnp.where(adm, _sortable(sc), INT_MIN)

    def count(mask):
        return jnp.sum(mask.astype(I32), axis=1, keepdims=True)

    n_sel = min(DSA_TOPK, (n_cache + ts) // 4)
    thr = _kth_largest_key(lambda cand: count(keys >= cand), ts, n_sel)
    real = thr > INT_MIN
    need = n_sel - count(keys > thr)
    tie = (keys == thr) & real

    def step(b, jc):
        cand = jc + jnp.left_shift(jnp.int32(1), 13 - b)
        return jnp.where(count(tie & (col < cand)) < need, cand, jc)

    jmax = lax.fori_loop(0, 14, step, jnp.zeros((ts, 1), I32))
    keep = (keys > thr) | (tie & (col <= jmax))
    bias = jnp.where(keep, 0.0, NEG_BIG)
    bias = jnp.concatenate([bias] * HQ_C, axis=0)

    qf16 = _block_diag_queries(q_ref[0], HKV_C, rep).astype(BF16)
    kn16 = _pad_rows(kn_ref[0], LANES).astype(BF16)
    vn16 = _pad_rows(vn_ref[0], LANES).astype(BF16)
    s = jnp.concatenate([lax.dot_general(qf16, kp[p][0].astype(BF16), _NT, preferred_element_type=F32)
                         for p in range(n_pages)]
                        + [lax.dot_general(qf16, kn16, _NT, preferred_element_type=F32)], axis=1) + bias
    m = jnp.max(s, axis=1, keepdims=True)
    p_all = jnp.exp(s - m)
    l = jnp.sum(p_all, axis=1, keepdims=True)
    p16 = p_all.astype(BF16)
    acc = jnp.dot(p16[:, n_cache:n_cache + LANES], vn16, preferred_element_type=F32)
    for p in range(n_pages):
        acc = acc + jnp.dot(p16[:, p * PAGE_SIZE:(p + 1) * PAGE_SIZE], vp[p][0].astype(BF16),
                            preferred_element_type=F32)
    o_ref[0] = _unstack_block_diag(acc / l, HKV_C, rep, ts)


def _dsa_sample(q, kn, vn, qi, kiwi, cache_k, cache_v, cache_ki, page_table):
    db, ts, _ = q.shape
    n_pages = page_table.shape[1]
    n_pool = cache_k.shape[0]
    ck = cache_k.reshape(n_pool, PAGE_SIZE, KV_C)
    cv = cache_v.reshape(n_pool, PAGE_SIZE, KV_C)
    seq = lambda w: pl.BlockSpec((1, ts, w), lambda i, pt: (i, 0, 0))
    o = pl.pallas_call(
        functools.partial(_dsa_sample_body, n_pages),
        grid_spec=pltpu.PrefetchScalarGridSpec(
            num_scalar_prefetch=1,
            grid=(db,),
            in_specs=[seq(Q_C), seq(KV_C), seq(KV_C), seq(N_IDX_HEADS * IDX_DIM), seq(LANES)]
            + _page_specs(n_pages, KV_C) * 2 + _page_specs(n_pages, IDX_DIM),
            out_specs=seq(OUT_C),
        ),
        out_shape=jax.ShapeDtypeStruct((db, ts, OUT_C), F32),
        compiler_params=_params(("arbitrary",)),
        name="dsa_sample",
    )(page_table.reshape(-1), q, kn, vn, qi, kiwi,
      *([ck] * n_pages), *([cv] * n_pages), *([cache_ki] * n_pages))
    return o.reshape(db * ts, OUT_C)


def _rope_tables(pos):
    half = HEAD_DIM // 2
    inv_freq = ROPE_THETA ** (-jnp.arange(half, dtype=F32) / half)
    ang = pos.astype(F32)[:, None] * inv_freq[None, :]
    c, s = jnp.cos(ang), jnp.sin(ang)
    return jnp.tile(c, (1, LANES // half)), jnp.concatenate([-s, s] * (LANES // HEAD_DIM), axis=1)


def _gain128(g):
    return jnp.tile(g.astype(F32), LANES // HEAD_DIM).reshape(1, LANES)


def _layer_a(xp, xs, shp, shs, states, params, tabs_p, tabs_s):
    norm_g, w_in, q_gain, k_gain, w_out = params
    b, t = shp
    db, ts = shs
    w16 = w_in.astype(BF16)
    args = (norm_g.reshape(1, -1), w16)
    gains = (_gain128(q_gain), _gain128(k_gain))
    qp, kp, vp, kp16, vp16, sgp = _project("a", xp, *args, *tabs_p, *gains)
    qs, ks, vs, _, _, sgs = _project("a", xs, *args, *tabs_s, *gains)
    hw = HKV_A * HEAD_DIM
    o_p, l_p, o_s, l_s, new_state = [], [], [], [], []
    for g, (win, dil) in enumerate(DIL_GROUPS):
        o, lse = _dil_prompt(qp.reshape(b, t, Q_A), kp16.reshape(b, t, KV_A), vp16.reshape(b, t, KV_A), g, dil)
        o_p.append(o)
        l_p.append(lse)
        o, lse, ns = _dil_sample(qs.reshape(db, ts, Q_A), ks.reshape(db, ts, KV_A), vs.reshape(db, ts, KV_A),
                                 states[g], g, states[g].shape[1], dil)
        o_s.append(o)
        l_s.append(lse)
        wk = min(win, t)
        kg = kp.reshape(b, t, KV_A)[:, t - wk:, g * hw:(g + 1) * hw].reshape(b, wk, 1, HKV_A, HEAD_DIM)
        vg = vp.reshape(b, t, KV_A)[:, t - wk:, g * hw:(g + 1) * hw].reshape(b, wk, 1, HKV_A, HEAD_DIM)
        new_state.append(jnp.concatenate([kg, vg], axis=2))
        new_state.append(ns)
    w_out16 = w_out.astype(BF16)
    return _out_project(o_p, l_p, sgp, xp, w_out16), _out_project(o_s, l_s, sgs, xs, w_out16), new_state


def _layer_b(xp, xs, shp, shs, states, params, tabs_p, tabs_s, page_table):
    norm_g, w_in, q_gain, k_gain, w_out = params
    cache_k, cache_v = states
    b, t = shp
    db, ts = shs
    w16 = w_in.astype(BF16)
    args = (norm_g.reshape(1, -1), w16)
    gains = (_gain128(q_gain), _gain128(k_gain))
    qp, kp, vp, kp16, vp16, sgp, kmean = _project("b", xp, *args, *tabs_p, *gains, with_kmean=True)
    qs, ks, vs, _, _, sgs = _project("b", xs, *args, *tabs_s, *gains)
    o_p = _moba_prompt(qp.reshape(b, t, Q_B), kp16.reshape(b, t, KV_B), vp16.reshape(b, t, KV_B),
                       kmean.reshape(b, t // MOBA_BLOCK, KV_B))
    o_s = _moba_sample(qs.reshape(db, ts, Q_B), ks.reshape(db, ts, KV_B), vs.reshape(db, ts, KV_B),
                       cache_k, cache_v, page_table)
    w_out16 = w_out.astype(BF16)
    new_state = [kp.reshape(b, t, HKV_B, HEAD_DIM), ks.reshape(db, ts, HKV_B, HEAD_DIM),
                 vp.reshape(b, t, HKV_B, HEAD_DIM), vs.reshape(db, ts, HKV_B, HEAD_DIM)]
    return _out_project([o_p], None, sgp, xp, w_out16), _out_project([o_s], None, sgs, xs, w_out16), new_state


def _layer_c(xp, xs, shp, shs, states, params, tabs_p, tabs_s, page_table):
    norm_g, w_in, q_gain, k_gain, w_out = params
    cache_k, cache_v, cache_ki = states
    b, t = shp
    db, ts = shs
    w16 = jnp.pad(w_in, ((0, 0), (0, IN_C_PAD - IN_C))).astype(BF16)
    args = (norm_g.reshape(1, -1), w16)
    gains = (_gain128(q_gain), _gain128(k_gain))
    qp, kp, vp, kp16, vp16, sgp, qip, kiwip, kiwip16 = _project("c", xp, *args, *tabs_p, *gains)
    qs, ks, vs, _, _, sgs, qis, kiwis, _ = _project("c", xs, *args, *tabs_s, *gains)
    o_p = _dsa_prompt(qp.reshape(b, t, Q_C), kp16.reshape(b, t, KV_C), vp16.reshape(b, t, KV_C),
                      qip.reshape(b, t, -1), kiwip.reshape(b, t, LANES), kiwip16.reshape(b, t, LANES))
    o_s = _dsa_sample(qs.reshape(db, ts, Q_C), ks.reshape(db, ts, KV_C), vs.reshape(db, ts, KV_C),
                      qis.reshape(db, ts, -1), kiwis.reshape(db, ts, LANES), cache_k, cache_v, cache_ki, page_table)
    w_out16 = w_out.astype(BF16)
    new_state = [kp.reshape(b, t, HKV_C, HEAD_DIM), ks.reshape(db, ts, HKV_C, HEAD_DIM),
                 vp.reshape(b, t, HKV_C, HEAD_DIM), vs.reshape(db, ts, HKV_C, HEAD_DIM),
                 kiwip.reshape(b, t, LANES)[:, :, :IDX_DIM], kiwis.reshape(db, ts, LANES)[:, :, :IDX_DIM]]
    return _out_project([o_p], None, sgp, xp, w_out16), _out_project([o_s], None, sgs, xs, w_out16), new_state


def kernel(x_prompt, x_sample, state_l0_kv_w128, state_l0_kv_w512, state_l0_kv_w2048, cache_l1_k, cache_l1_v, cache_l2_k, cache_l2_v, cache_l2_kidx, state_l3_kv_w128, state_l3_kv_w512, state_l3_kv_w2048, page_table, l0_norm, l0_w_in, l0_q_norm, l0_k_norm, l0_w_out, l1_norm, l1_w_in, l1_q_norm, l1_k_norm, l1_w_out, l2_norm, l2_w_in, l2_q_norm, l2_k_norm, l2_w_out, l3_norm, l3_w_in, l3_q_norm, l3_k_norm, l3_w_out):
    b, t, _ = x_prompt.shape
    db, ts, _ = x_sample.shape
    assert t % (16 * Q_TILE) == 0 and t % DSA_CHUNK == 0 and (db * ts) % PROJ_ROWS == 0 and PROJ_ROWS % ts == 0
    tabs_p = _rope_tables(jnp.arange(t))
    tabs_s = _rope_tables(PAST_LEN + (jnp.arange(PROJ_ROWS) % ts))
    xp = x_prompt.reshape(b * t, D_MODEL)
    xs = x_sample.reshape(db * ts, D_MODEL)
    states = ((state_l0_kv_w128, state_l0_kv_w512, state_l0_kv_w2048), (cache_l1_k, cache_l1_v),
              (cache_l2_k, cache_l2_v, cache_l2_kidx), (state_l3_kv_w128, state_l3_kv_w512, state_l3_kv_w2048))
    params = ((l0_norm, l0_w_in, l0_q_norm, l0_k_norm, l0_w_out), (l1_norm, l1_w_in, l1_q_norm, l1_k_norm, l1_w_out),
              (l2_norm, l2_w_in, l2_q_norm, l2_k_norm, l2_w_out), (l3_norm, l3_w_in, l3_q_norm, l3_k_norm, l3_w_out))
    new_state = []
    for i in range(4):
        common = (xp, xs, (b, t), (db, ts), states[i], params[i], tabs_p, tabs_s)
        if i % 3 == 0:
            xp, xs, st = _layer_a(*common)
        elif i % 3 == 1:
            xp, xs, st = _layer_b(*common, page_table)
        else:
            xp, xs, st = _layer_c(*common, page_table)
        new_state.extend(st)
    return (xp.reshape(b, t, D_MODEL), xs.reshape(db, ts, D_MODEL), *new_state)
```

```python
import functools

import jax
import jax.numpy as jnp
from jax import lax
from jax.experimental import pallas as pl
from jax.experimental.pallas import tpu as pltpu

F32 = jnp.float32
BF16 = jnp.bfloat16
I32 = jnp.int32

D_MODEL = 1024
PAST_LEN = 2048
PAGE_SIZE = 128
HEAD_DIM = 64
ROPE_THETA = 10000.0
NORM_EPS = 1e-6

DIL_GROUPS = ((128, 1), (512, 4), (2048, 16))
HQ_A, HKV_A = 8, 2
Q_A, KV_A, OUT_A = 1536, 384, 512

HQ_B, HKV_B = 16, 4
MOBA_BLOCK, MOBA_TOPK = 256, 3
Q_B, KV_B, OUT_B = 1024, 256, 1024

HQ_C, HKV_C = 16, 4
N_IDX_HEADS, IDX_DIM, DSA_TOPK = 8, 64, 256
Q_C, KV_C, OUT_C = 1024, 256, 1024
IN_C = Q_C + 2 * KV_C + OUT_C + N_IDX_HEADS * IDX_DIM + IDX_DIM + N_IDX_HEADS
IN_C_PAD = 3200

LANES = 128
PROJ_ROWS = 256
Q_TILE = 128
DSA_CHUNK = 512
VMEM_LIMIT = 56 * 1024 * 1024
NEG_BIG = -1e30
LOG2E = 1.4426950408889634
VT_ROWS = 80
INT_MIN = -2147483648

_NT = (((1,), (1,)), ((), ()))


def _params(sem):
    return pltpu.CompilerParams(dimension_semantics=sem, vmem_limit_bytes=VMEM_LIMIT)


def _group_sum_matrix():
    r = lax.broadcasted_iota(I32, (LANES, LANES), 0) // HEAD_DIM
    c = lax.broadcasted_iota(I32, (LANES, LANES), 1) // HEAD_DIM
    return (r == c).astype(BF16)


def _rope(y, cos, sin):
    lane = lax.broadcasted_iota(I32, y.shape, 1)
    first_half = (lane & (HEAD_DIM // 2)) == 0
    partner = jnp.where(first_half, pltpu.roll(y, LANES - HEAD_DIM // 2, 1), pltpu.roll(y, HEAD_DIM // 2, 1))
    return y * cos + partner * sin


def _head_norm(x, gain, gmat):
    ss = x * x
    hi = ss.astype(BF16)
    lo = (ss - hi.astype(F32)).astype(BF16)
    gs = jnp.dot(hi, gmat, preferred_element_type=F32) + jnp.dot(lo, gmat, preferred_element_type=F32)
    return x * lax.rsqrt(gs * (1.0 / HEAD_DIM) + NORM_EPS) * gain


def _proj_body(kind, aug_tiles, x_ref, g_ref, w_ref, cos_ref, sin_ref, qg_ref, kg_ref, *outs):
    x = x_ref[...]
    ms = jnp.mean(x * x, axis=-1, keepdims=True)
    h = (x * lax.rsqrt(ms + NORM_EPS) * g_ref[...]).astype(BF16)
    cos = cos_ref[...]
    sin = sin_ref[...]
    qg = qg_ref[...]
    kg = kg_ref[...]
    gmat = _group_sum_matrix()
    qw, kvw, gw = {"a": (Q_A, KV_A, OUT_A), "b": (Q_B, KV_B, OUT_B), "c": (Q_C, KV_C, OUT_C)}[kind]
    q_ref, k_ref, v_ref, k16_ref, v16_ref, sg_ref = outs[:6]
    rest = outs[6:]
    moba = aug_tiles and kind == "b"
    if aug_tiles:
        kaug_ref, vt_ref = rest[-2:]

    def seg(start, width):
        return jnp.dot(h, w_ref[:, start:start + width], preferred_element_type=F32)

    zq = seg(0, qw)
    for c in range(qw // LANES):
        y = _rope(_head_norm(zq[:, c * LANES:(c + 1) * LANES], qg, gmat), cos, sin)
        q_ref[:, c * LANES:(c + 1) * LANES] = y * (HEAD_DIM ** -0.5)
    zk = seg(qw, kvw)
    ksum = []
    for c in range(kvw // LANES):
        y = _rope(_head_norm(zk[:, c * LANES:(c + 1) * LANES], kg, gmat), cos, sin)
        k_ref[:, c * LANES:(c + 1) * LANES] = y
        k16_ref[:, c * LANES:(c + 1) * LANES] = y.astype(BF16)
        if moba:
            ksum.append(jnp.sum(y, axis=0, keepdims=True) * (1.0 / MOBA_BLOCK))
        if aug_tiles:
            lane = lax.broadcasted_iota(I32, y.shape, 1)
            hot = HEAD_DIM + pl.program_id(0) % aug_tiles if moba else -1
            onehot = (lane == hot).astype(F32)
            kaug_ref[:, 2 * c * LANES:(2 * c + 1) * LANES] = jnp.where(lane < HEAD_DIM, y, onehot).astype(BF16)
            kaug_ref[:, (2 * c + 1) * LANES:(2 * c + 2) * LANES] = jnp.where(
                lane < HEAD_DIM, pltpu.roll(y, HEAD_DIM, 1), onehot).astype(BF16)
    zv = seg(qw + kvw, kvw)
    v_ref[...] = zv
    v16_ref[...] = zv.astype(BF16)
    if aug_tiles:
        zvt = zv.T
        tail = (lax.broadcasted_iota(I32, (VT_ROWS - HEAD_DIM, zvt.shape[1]), 0) == 0).astype(BF16)
        for kvh in range(kvw // HEAD_DIM):
            vt_ref[0, kvh * VT_ROWS:kvh * VT_ROWS + HEAD_DIM, :] = zvt[kvh * HEAD_DIM:(kvh + 1) * HEAD_DIM].astype(BF16)
            vt_ref[0, kvh * VT_ROWS + HEAD_DIM:(kvh + 1) * VT_ROWS, :] = tail
    zg = seg(qw + 2 * kvw, gw)
    sg_ref[...] = zg / (1.0 + jnp.exp(-zg))
    nrest = 0
    if kind == "c":
        qi_ref, kiwi_ref, kiwi16_ref = rest[:3]
        nrest = 3
        base = qw + 2 * kvw + gw
        zi = seg(base, N_IDX_HEADS * IDX_DIM)
        for c in range(N_IDX_HEADS * IDX_DIM // LANES):
            qi_ref[:, c * LANES:(c + 1) * LANES] = _rope(zi[:, c * LANES:(c + 1) * LANES], cos, sin)
        zz = seg(base + N_IDX_HEADS * IDX_DIM, LANES)
        lane = lax.broadcasted_iota(I32, zz.shape, 1)
        kiwi = jnp.where(lane < IDX_DIM, _rope(zz, cos, sin), zz)
        kiwi_ref[...] = kiwi
        kiwi16_ref[...] = kiwi.astype(BF16)
    if moba:
        km_ref = rest[0]
        for c in range(kvw // LANES):
            km_ref[0, :, c * LANES:(c + 1) * LANES] = ksum[c]


def _project(kind, x, gain, w16, cos, sin, qg, kg, aug=False):
    n = x.shape[0]
    tm = PROJ_ROWS
    nt = cos.shape[0] // tm
    qw, kvw, gw = {"a": (Q_A, KV_A, OUT_A), "b": (Q_B, KV_B, OUT_B), "c": (Q_C, KV_C, OUT_C)}[kind]
    wp = w16.shape[1]

    def rows(width):
        return pl.BlockSpec((tm, width), lambda i: (i, 0))

    def const(shape):
        return pl.BlockSpec(shape, lambda i: (0,) * len(shape))

    out_shape = [jax.ShapeDtypeStruct((n, qw), F32), jax.ShapeDtypeStruct((n, kvw), F32),
                 jax.ShapeDtypeStruct((n, kvw), F32), jax.ShapeDtypeStruct((n, kvw), BF16),
                 jax.ShapeDtypeStruct((n, kvw), BF16), jax.ShapeDtypeStruct((n, gw), F32)]
    out_specs = [rows(qw), rows(kvw), rows(kvw), rows(kvw), rows(kvw), rows(gw)]
    if kind == "c":
        out_shape += [jax.ShapeDtypeStruct((n, N_IDX_HEADS * IDX_DIM), F32),
                      jax.ShapeDtypeStruct((n, LANES), F32), jax.ShapeDtypeStruct((n, LANES), BF16)]
        out_specs += [rows(N_IDX_HEADS * IDX_DIM), rows(LANES), rows(LANES)]
    if aug:
        n_kv = kvw // HEAD_DIM
        if kind == "b":
            out_shape.append(jax.ShapeDtypeStruct((n // tm, 1, kvw), F32))
            out_specs.append(pl.BlockSpec((1, 1, kvw), lambda i: (i, 0, 0)))
        out_shape += [jax.ShapeDtypeStruct((n, n_kv * LANES), BF16),
                      jax.ShapeDtypeStruct((n // (nt * tm), n_kv * VT_ROWS, nt * tm), BF16)]
        out_specs += [rows(n_kv * LANES), pl.BlockSpec((1, n_kv * VT_ROWS, tm), lambda i: (i // nt, 0, i % nt))]
    return pl.pallas_call(
        functools.partial(_proj_body, kind, nt if aug else 0),
        grid=(n // tm,),
        in_specs=[rows(D_MODEL), const((1, D_MODEL)), const((D_MODEL, wp)),
                  pl.BlockSpec((tm, LANES), lambda i: (i % nt, 0)),
                  pl.BlockSpec((tm, LANES), lambda i: (i % nt, 0)),
                  const((1, LANES)), const((1, LANES))],
        out_specs=out_specs,
        out_shape=out_shape,
        compiler_params=_params(("parallel",)),
        name=f"proj_{kind}",
    )(x, gain, w16, cos, sin, qg, kg)


def _outproj_body(merge, *refs):
    if merge:
        o0, o1, o2, l0, l1, l2, sg_ref, x_ref, w_ref, y_ref = refs
        a, b, c = l0[...], l1[...], l2[...]
        m = jnp.maximum(jnp.maximum(a, b), c)
        ea, eb, ec = jnp.exp(a - m), jnp.exp(b - m), jnp.exp(c - m)
        o = (ea * o0[...] + eb * o1[...] + ec * o2[...]) / (ea + eb + ec)
    else:
        o_ref, sg_ref, x_ref, w_ref, y_ref = refs
        o = o_ref[...]
    g = (o * sg_ref[...]).astype(BF16)
    y_ref[...] = x_ref[...] + jnp.dot(g, w_ref[...], preferred_element_type=F32)


def _out_project(os_, lses, sg, x, w16):
    n = x.shape[0]
    f = sg.shape[1]
    tm = PROJ_ROWS
    merge = lses is not None
    ins = list(os_) + (list(lses) if merge else []) + [sg, x, w16]
    rows_f = pl.BlockSpec((tm, f), lambda i: (i, 0))
    in_specs = [rows_f] * (len(ins) - 2) + [pl.BlockSpec((tm, D_MODEL), lambda i: (i, 0)),
                                             pl.BlockSpec((f, D_MODEL), lambda i: (0, 0))]
    return pl.pallas_call(
        functools.partial(_outproj_body, merge),
        grid=(n // tm,),
        in_specs=in_specs,
        out_specs=pl.BlockSpec((tm, D_MODEL), lambda i: (i, 0)),
        out_shape=jax.ShapeDtypeStruct((n, D_MODEL), F32),
        compiler_params=_params(("parallel",)),
        name="out_proj_merge" if merge else "out_proj",
    )(*ins)


def _stack_heads(q, kvh, rep):
    return jnp.concatenate([q[:, (kvh * rep + r) * HEAD_DIM:(kvh * rep + r + 1) * HEAD_DIM] for r in range(rep)],
                           axis=0)


def _softmax_step(s, m, l, acc, v16):
    m_new = jnp.maximum(m, jnp.max(s, axis=1, keepdims=True))
    alpha = jnp.exp(m - m_new)
    p = jnp.exp(s - m_new)
    l_new = alpha * l + jnp.sum(p, axis=1, keepdims=True)
    acc_new = alpha * acc + jnp.dot(p.astype(BF16), v16, preferred_element_type=F32)
    return m_new, l_new, acc_new


def _dil_prompt_body(q_ref, kp_ref, kc_ref, vp_ref, vc_ref, o_ref, lse_ref):
    j = pl.program_id(2)
    tq = Q_TILE
    rep = HQ_A // HKV_A
    q = q_ref[0].astype(BF16)
    kk = jnp.concatenate([kp_ref[0], kc_ref[0]], axis=0)
    vv = jnp.concatenate([vp_ref[0], vc_ref[0]], axis=0)
    row = lax.broadcasted_iota(I32, (tq, 2 * tq), 0)
    col = lax.broadcasted_iota(I32, (tq, 2 * tq), 1)
    dist = row + tq - col
    valid = (dist >= 0) & (dist <= tq) & ((col >= tq) | (j > 0))
    valid = jnp.concatenate([valid] * rep, axis=0)
    for kvh in range(HKV_A):
        kh = kk[:, kvh * HEAD_DIM:(kvh + 1) * HEAD_DIM]
        vh = vv[:, kvh * HEAD_DIM:(kvh + 1) * HEAD_DIM]
        qh = _stack_heads(q, kvh, rep)
        s = lax.dot_general(qh, kh, _NT, preferred_element_type=F32)
        s = jnp.where(valid, s, -jnp.inf)
        m = jnp.max(s, axis=1, keepdims=True)
        p = jnp.exp(s - m)
        l = jnp.sum(p, axis=1, keepdims=True)
        o = jnp.dot(p.astype(BF16), vh, preferred_element_type=F32) / l
        lse = m + jnp.log(l)
        for r in range(rep):
            hd = (kvh * rep + r) * HEAD_DIM
            o_ref[0, :, hd:hd + HEAD_DIM] = o[r * tq:(r + 1) * tq]
            lse_ref[0, :, hd:hd + HEAD_DIM] = jnp.broadcast_to(lse[r * tq:(r + 1) * tq], (tq, HEAD_DIM))


def _dil_prompt(q, k16, v16, g, dil):
    b, t, _ = q.shape
    tr = t // dil
    qv = q.reshape(b, tr, dil * Q_A)
    kv = k16.reshape(b, tr, dil * KV_A)
    vv = v16.reshape(b, tr, dil * KV_A)
    ng = len(DIL_GROUPS)
    hw = HKV_A * HEAD_DIM
    cur = lambda bb, r, j: (bb, j, r * ng + g)
    prev = lambda bb, r, j: (bb, jnp.maximum(j - 1, 0), r * ng + g)
    o, lse = pl.pallas_call(
        _dil_prompt_body,
        grid=(b, dil, tr // Q_TILE),
        in_specs=[pl.BlockSpec((1, Q_TILE, OUT_A), cur),
                  pl.BlockSpec((1, Q_TILE, hw), prev), pl.BlockSpec((1, Q_TILE, hw), cur),
                  pl.BlockSpec((1, Q_TILE, hw), prev), pl.BlockSpec((1, Q_TILE, hw), cur)],
        out_specs=[pl.BlockSpec((1, Q_TILE, OUT_A), lambda bb, r, j: (bb, j, r))] * 2,
        out_shape=[jax.ShapeDtypeStruct((b, tr, dil * OUT_A), F32)] * 2,
        compiler_params=_params(("parallel", "parallel", "arbitrary")),
        name=f"dil_prompt_g{g}",
    )(qv, kv, kv, vv, vv)
    return o.reshape(b * t, OUT_A), lse.reshape(b * t, OUT_A)


def _dil_sample_body(win, dil, sb, q_ref, kn_ref, vn_ref, st_ref, o_ref, lse_ref, ns_ref, ctx_ref):
    rep = HQ_A // HKV_A
    hw = HKV_A * HEAD_DIM
    ts = q_ref.shape[1]
    lc = win + LANES
    ctx_ref[pl.ds(win, LANES), :] = jnp.zeros((LANES, 2 * hw), F32)
    row = lax.broadcasted_iota(I32, (rep * ts, lc), 0)
    col = lax.broadcasted_iota(I32, (rep * ts, lc), 1)
    dist = win + (row & (ts - 1)) - col
    valid = (dist >= 0) & (dist <= win) & ((dist & (dil - 1)) == 0)
    for s_i in range(sb):
        ctx_ref[pl.ds(0, win), :] = st_ref[s_i]
        ctx_ref[pl.ds(win, ts), 0:hw] = kn_ref[s_i]
        ctx_ref[pl.ds(win, ts), hw:2 * hw] = vn_ref[s_i]
        ns_ref[s_i] = ctx_ref[pl.ds(ts, win), :]
        kc = ctx_ref[:, 0:hw].astype(BF16)
        vc = ctx_ref[:, hw:2 * hw].astype(BF16)
        q = q_ref[s_i].astype(BF16)
        for kvh in range(HKV_A):
            kh = kc[:, kvh * HEAD_DIM:(kvh + 1) * HEAD_DIM]
            vh = vc[:, kvh * HEAD_DIM:(kvh + 1) * HEAD_DIM]
            qh = _stack_heads(q, kvh, rep)
            s = lax.dot_general(qh, kh, _NT, preferred_element_type=F32)
            s = jnp.where(valid, s, -jnp.inf)
            m = jnp.max(s, axis=1, keepdims=True)
            p = jnp.exp(s - m)
            l = jnp.sum(p, axis=1, keepdims=True)
            o = jnp.dot(p.astype(BF16), vh, preferred_element_type=F32) / l
            lse = m + jnp.log(l)
            for r in range(rep):
                hd = (kvh * rep + r) * HEAD_DIM
                o_ref[s_i, :, hd:hd + HEAD_DIM] = o[r * ts:(r + 1) * ts]
                lse_ref[s_i, :, hd:hd + HEAD_DIM] = jnp.broadcast_to(lse[r * ts:(r + 1) * ts], (ts, HEAD_DIM))


def _dil_sample(q, kn, vn, state, g, win, dil):
    db, ts, _ = q.shape
    hw = HKV_A * HEAD_DIM
    st = state.reshape(db, win, 2 * hw)
    sb = max(1, min(8, 1024 // win))
    o, lse, ns = pl.pallas_call(
        functools.partial(_dil_sample_body, win, dil, sb),
        grid=(db // sb,),
        in_specs=[pl.BlockSpec((sb, ts, OUT_A), lambda i: (i, 0, g)),
                  pl.BlockSpec((sb, ts, hw), lambda i: (i, 0, g)),
                  pl.BlockSpec((sb, ts, hw), lambda i: (i, 0, g)),
                  pl.BlockSpec((sb, win, 2 * hw), lambda i: (i, 0, 0))],
        out_specs=[pl.BlockSpec((sb, ts, OUT_A), lambda i: (i, 0, 0)),
                   pl.BlockSpec((sb, ts, OUT_A), lambda i: (i, 0, 0)),
                   pl.BlockSpec((sb, win, 2 * hw), lambda i: (i, 0, 0))],
        out_shape=[jax.ShapeDtypeStruct((db, ts, OUT_A), F32), jax.ShapeDtypeStruct((db, ts, OUT_A), F32),
                   jax.ShapeDtypeStruct((db, win, 2 * hw), F32)],
        scratch_shapes=[pltpu.VMEM((win + LANES, 2 * hw), F32)],
        compiler_params=_params(("arbitrary",)),
        name=f"dil_sample_g{g}",
    )(q, kn, vn, st)
    return o.reshape(db * ts, OUT_A), lse.reshape(db * ts, OUT_A), ns.reshape(state.shape)


def _top_blocks(bs, n_valid, topk):
    lane = lax.broadcasted_iota(I32, bs.shape, 1)
    nb = bs.shape[1]
    work = jnp.where(lane < n_valid, bs, -jnp.inf)
    chosen = jnp.zeros(bs.shape, jnp.bool_)
    for _ in range(topk):
        m = jnp.max(work, axis=1, keepdims=True)
        idx = jnp.min(jnp.where(work == m, lane, nb), axis=1, keepdims=True)
        pick = lane == idx
        chosen = chosen | pick
        work = jnp.where(pick, -jnp.inf, work)
    return chosen & (lane < n_valid)


def _top_blocks_t(bs, n_valid, topk):
    row = lax.broadcasted_iota(I32, bs.shape, 0)
    rowf = row.astype(F32)
    work = jnp.where(row < n_valid, bs, -jnp.inf)
    chosen = jnp.zeros(bs.shape, jnp.bool_)
    for _ in range(topk):
        m = jnp.max(work, axis=0, keepdims=True)
        idx = jnp.min(jnp.where(work == m, rowf, float(bs.shape[0])), axis=0, keepdims=True)
        pick = rowf == idx
        chosen = chosen | pick
        work = jnp.where(pick, -jnp.inf, work)
    return chosen & (row < n_valid)


def _moba_prompt_body(q_ref, k_ref, v_ref, km_ref, o_ref):
    i = pl.program_id(1)
    tq = Q_TILE
    rep = HQ_B // HKV_B
    nb = km_ref.shape[1]
    nr = rep * tq
    own = (i * tq) // MOBA_BLOCK
    own0 = pl.multiple_of(own * MOBA_BLOCK, MOBA_BLOCK)
    qt = q_ref[0].T
    km = km_ref[0]
    krow = lax.broadcasted_iota(I32, (MOBA_BLOCK, nr), 0)
    qcol = lax.broadcasted_iota(I32, (MOBA_BLOCK, nr), 1)
    causal = own0 + krow <= i * tq + (qcol & (tq - 1))
    brow = lax.broadcasted_iota(I32, (HEAD_DIM, nr), 0)
    qas, init = [], []
    for kvh in range(HKV_B):
        hs = slice(kvh * HEAD_DIM, (kvh + 1) * HEAD_DIM)
        ls = slice(kvh * LANES, (kvh + 1) * LANES)
        vs = slice(kvh * VT_ROWS, (kvh + 1) * VT_ROWS)
        q4t = jnp.concatenate([qt[(kvh * rep + r) * HEAD_DIM:(kvh * rep + r + 1) * HEAD_DIM] for r in range(rep)],
                              axis=1)
        bst = jnp.dot(km[:, hs], q4t, precision=lax.Precision.HIGHEST, preferred_element_type=F32)
        if nb < HEAD_DIM:
            bst = jnp.concatenate([bst, jnp.zeros((HEAD_DIM - nb, nr), F32)], axis=0)
        chosen = _top_blocks_t(bst, own, MOBA_TOPK) | (brow == own)
        qa = jnp.concatenate([q4t * LOG2E, jnp.where(chosen, 0.0, NEG_BIG)], axis=0).astype(BF16)
        qas.append(qa)
        st = jnp.dot(k_ref[0, pl.ds(own0, MOBA_BLOCK), ls], qa, preferred_element_type=F32)
        st = jnp.where(causal, st, -jnp.inf)
        m = jnp.max(st, axis=0, keepdims=True)
        p = jnp.exp2(st - m)
        init.append((m, jnp.dot(v_ref[0, vs, pl.ds(own0, MOBA_BLOCK)], p.astype(BF16), preferred_element_type=F32)))

    def body(n, carry):
        n0 = pl.multiple_of(n * MOBA_BLOCK, MOBA_BLOCK)

        def scores(kvh):
            return jnp.dot(k_ref[0, pl.ds(n0, MOBA_BLOCK), kvh * LANES:(kvh + 1) * LANES], qas[kvh],
                           preferred_element_type=F32)

        def update(kvh, st):
            m, acc = carry[kvh]
            m_new = jnp.maximum(m, jnp.max(st, axis=0, keepdims=True))
            p = jnp.exp2(st - m_new)
            pv = jnp.dot(v_ref[0, kvh * VT_ROWS:(kvh + 1) * VT_ROWS, pl.ds(n0, MOBA_BLOCK)], p.astype(BF16),
                         preferred_element_type=F32)
            return m_new, jnp.exp2(m - m_new) * acc + pv

        out = []
        st = scores(0)
        for kvh in range(HKV_B):
            nxt = scores(kvh + 1) if kvh + 1 < HKV_B else None
            out.append(update(kvh, st))
            st = nxt
        return tuple(out)

    res = lax.fori_loop(0, own, body, tuple(init))
    heads = []
    for kvh in range(HKV_B):
        acc = res[kvh][1]
        ot = acc[0:HEAD_DIM] / acc[HEAD_DIM:HEAD_DIM + 1]
        heads += [ot[:, r * tq:(r + 1) * tq] for r in range(rep)]
    o_ref[0] = jnp.concatenate(heads, axis=0).T


def _moba_prompt(q, kaug16, vt16, kmean):
    b, t, _ = q.shape
    nb = t // MOBA_BLOCK
    o = pl.pallas_call(
        _moba_prompt_body,
        grid=(b, t // Q_TILE),
        in_specs=[pl.BlockSpec((1, Q_TILE, Q_B), lambda bb, i: (bb, i, 0)),
                  pl.BlockSpec((1, t, HKV_B * LANES), lambda bb, i: (bb, 0, 0)),
                  pl.BlockSpec((1, HKV_B * VT_ROWS, t), lambda bb, i: (bb, 0, 0)),
                  pl.BlockSpec((1, nb, KV_B), lambda bb, i: (bb, 0, 0))],
        out_specs=pl.BlockSpec((1, Q_TILE, OUT_B), lambda bb, i: (bb, i, 0)),
        out_shape=jax.ShapeDtypeStruct((b, t, OUT_B), F32),
        compiler_params=_params(("parallel", "arbitrary")),
        name="moba_prompt",
    )(q, kaug16, vt16, kmean)
    return o.reshape(b * t, OUT_B)


def _block_diag_queries(q, n_kv, rep):
    ts = q.shape[0]
    rows = []
    for kvh in range(n_kv):
        for r in range(rep):
            hd = (kvh * rep + r) * HEAD_DIM
            parts = []
            if kvh > 0:
                parts.append(jnp.zeros((ts, kvh * HEAD_DIM), q.dtype))
            parts.append(q[:, hd:hd + HEAD_DIM])
            if kvh < n_kv - 1:
                parts.append(jnp.zeros((ts, (n_kv - 1 - kvh) * HEAD_DIM), q.dtype))
            rows.append(jnp.concatenate(parts, axis=1))
    return jnp.concatenate(rows, axis=0)


def _unstack_block_diag(o_all, n_kv, rep, ts):
    parts = []
    for kvh in range(n_kv):
        for r in range(rep):
            r0 = (kvh * rep + r) * ts
            parts.append(o_all[r0:r0 + ts, kvh * HEAD_DIM:(kvh + 1) * HEAD_DIM])
    return jnp.concatenate(parts, axis=1)


def _pad_rows(x, rows):
    return jnp.concatenate([x, jnp.zeros((rows - x.shape[0], x.shape[1]), x.dtype)], axis=0)


def _moba_sample_body(n_pages, pt_ref, q_ref, kn_ref, vn_ref, *refs):
    kp = refs[:n_pages]
    vp = refs[n_pages:2 * n_pages]
    o_ref = refs[2 * n_pages]
    rep = HQ_B // HKV_B
    ts = q_ref.shape[1]
    nrows = HQ_B * ts
    ppb = MOBA_BLOCK // PAGE_SIZE
    n_past = n_pages // ppb
    qf = _block_diag_queries(q_ref[0], HKV_B, rep)
    qf16 = qf.astype(BF16)
    kms = []
    for n in range(n_past):
        tot = kp[n * ppb][0]
        for j in range(1, ppb):
            tot = tot + kp[n * ppb + j][0]
        kms.append(jnp.sum(tot, axis=0, keepdims=True) * (1.0 / MOBA_BLOCK))
    km = _pad_rows(jnp.concatenate(kms, axis=0), LANES)
    bs = lax.dot_general(qf, km, _NT, precision=lax.Precision.HIGHEST, preferred_element_type=F32)
    chosen = _top_blocks(bs, n_past, MOBA_TOPK)
    selb = jnp.where(chosen, 0.0, NEG_BIG)
    bias = jnp.concatenate([jnp.broadcast_to(selb[:, n:n + 1], (nrows, MOBA_BLOCK)) for n in range(n_past)], axis=1)
    s_c = jnp.concatenate([lax.dot_general(qf16, kp[p][0].astype(BF16), _NT, preferred_element_type=F32)
                           for p in range(n_pages)], axis=1) + bias
    kn16 = _pad_rows(kn_ref[0], LANES).astype(BF16)
    vn16 = _pad_rows(vn_ref[0], LANES).astype(BF16)
    s_n = lax.dot_general(qf16, kn16, _NT, preferred_element_type=F32)
    row = lax.broadcasted_iota(I32, (nrows, LANES), 0)
    col = lax.broadcasted_iota(I32, (nrows, LANES), 1)
    s_n = jnp.where(col <= (row & (ts - 1)), s_n, -jnp.inf)
    m = jnp.maximum(jnp.max(s_c, axis=1, keepdims=True), jnp.max(s_n, axis=1, keepdims=True))
    p_c = jnp.exp(s_c - m)
    p_n = jnp.exp(s_n - m)
    l = jnp.sum(p_c, axis=1, keepdims=True) + jnp.sum(p_n, axis=1, keepdims=True)
    acc = jnp.dot(p_n.astype(BF16), vn16, preferred_element_type=F32)
    p16 = p_c.astype(BF16)
    for p in range(n_pages):
        acc = acc + jnp.dot(p16[:, p * PAGE_SIZE:(p + 1) * PAGE_SIZE], vp[p][0].astype(BF16),
                            preferred_element_type=F32)
    o_ref[0] = _unstack_block_diag(acc / l, HKV_B, rep, ts)


def _page_specs(n_pages, width):
    return [pl.BlockSpec((1, PAGE_SIZE, width), functools.partial(lambda p, i, pt: (pt[i * n_pages + p], 0, 0), p))
            for p in range(n_pages)]


def _moba_sample(q, kn, vn, cache_k, cache_v, page_table):
    db, ts, _ = q.shape
    n_pages = page_table.shape[1]
    n_pool = cache_k.shape[0]
    ck = cache_k.reshape(n_pool, PAGE_SIZE, KV_B)
    cv = cache_v.reshape(n_pool, PAGE_SIZE, KV_B)
    seq = lambda w: pl.BlockSpec((1, ts, w), lambda i, pt: (i, 0, 0))
    o = pl.pallas_call(
        functools.partial(_moba_sample_body, n_pages),
        grid_spec=pltpu.PrefetchScalarGridSpec(
            num_scalar_prefetch=1,
            grid=(db,),
            in_specs=[seq(Q_B), seq(KV_B), seq(KV_B)] + _page_specs(n_pages, KV_B) * 2,
            out_specs=seq(OUT_B),
        ),
        out_shape=jax.ShapeDtypeStruct((db, ts, OUT_B), F32),
        compiler_params=_params(("arbitrary",)),
        name="moba_sample",
    )(page_table.reshape(-1), q, kn, vn, *([ck] * n_pages), *([cv] * n_pages))
    return o.reshape(db * ts, OUT_B)


def _sortable(score):
    bits = pltpu.bitcast(score + 0.0, I32)
    return jnp.where(bits >= 0, bits, bits ^ 0x7FFFFFFF)


def _kth_largest_key(count_ge, rows, k):
    def step(b, t):
        cand = jnp.where(b == 0, jnp.zeros_like(t), t | jnp.left_shift(jnp.int32(1), 31 - b))
        return jnp.where(count_ge(cand) >= k, cand, t)

    return lax.fori_loop(0, 32, step, jnp.full((rows, 1), INT_MIN, I32))


def _index_scores(qi8, wv, ki16, rows):
    dots = lax.dot_general(qi8, ki16, _NT, preferred_element_type=F32)
    sc = jnp.maximum(dots[0:rows], 0.0) * wv[:, 0:1]
    for h in range(1, N_IDX_HEADS):
        sc = sc + jnp.maximum(dots[h * rows:(h + 1) * rows], 0.0) * wv[:, h:h + 1]
    return sc


def _dsa_prompt_body(q_ref, k_ref, v_ref, qi_ref, w_ref, ki_ref, o_ref, key_ref, bias_ref, j_ref):
    i = pl.program_id(1)
    tq = Q_TILE
    ck = DSA_CHUNK
    rep = HQ_C // HKV_C
    t_all = key_ref.shape[1]
    nch = ((i + 1) * tq + ck - 1) // ck
    qpos = i * tq + lax.broadcasted_iota(I32, (tq, 1), 0)
    qi = qi_ref[0].astype(BF16)
    qi8 = jnp.concatenate([qi[:, h * IDX_DIM:(h + 1) * IDX_DIM] for h in range(N_IDX_HEADS)], axis=0)
    wv = w_ref[0][:, IDX_DIM:IDX_DIM + N_IDX_HEADS] * (IDX_DIM ** -0.5 * N_IDX_HEADS ** -0.5)
    lane_c = lax.broadcasted_iota(I32, (tq, ck), 1)

    def scores(c, _):
        off = pl.multiple_of(c * ck, ck)
        sc = _index_scores(qi8, wv, ki_ref[0, pl.ds(off, ck), 0:IDX_DIM], tq)
        key_ref[:, pl.ds(off, ck)] = jnp.where(off + lane_c <= qpos, _sortable(sc), INT_MIN)
        return 0

    lax.fori_loop(0, nch, scores, 0)

    def fold(x):
        out = x[:, 0:LANES]
        for a in range(1, ck // LANES):
            out = out + x[:, a * LANES:(a + 1) * LANES]
        return out

    def count(pred):
        def body(c, acc):
            off = pl.multiple_of(c * ck, ck)
            return acc + fold(pred(key_ref[:, pl.ds(off, ck)], off + lane_c).astype(I32))
        return jnp.sum(lax.fori_loop(0, nch, body, jnp.zeros((tq, LANES), I32)), axis=1, keepdims=True)

    thr = _kth_largest_key(lambda cand: count(lambda kx, cx: kx >= cand), tq, DSA_TOPK)
    real = thr > INT_MIN
    need = DSA_TOPK - count(lambda kx, cx: kx > thr)
    n_tie = count(lambda kx, cx: (kx == thr) & real)
    j_ref[...] = jnp.full((tq, 1), t_all, I32)

    @pl.when(jnp.max(jnp.where(n_tie > need, 1, 0)) > 0)
    def _():
        def step(b, jc):
            cand = jc + jnp.left_shift(jnp.int32(1), 13 - b)
            below = count(lambda kx, cx: (kx == thr) & real & (cx < cand))
            return jnp.where(below < need, cand, jc)
        j_ref[...] = lax.fori_loop(0, 14, step, jnp.zeros((tq, 1), I32))

    jmax = j_ref[...]

    def write_bias(c, _):
        off = pl.multiple_of(c * ck, ck)
        kx = key_ref[:, pl.ds(off, ck)]
        keep = (kx > thr) | ((kx == thr) & real & (off + lane_c <= jmax))
        bias_ref[pl.ds(off, ck), :] = jnp.where(keep, 0.0, NEG_BIG).T
        return 0

    lax.fori_loop(0, nch, write_bias, 0)

    ca = MOBA_BLOCK
    nr = rep * tq
    qt = q_ref[0].T
    qas = []
    for kvh in range(HKV_C):
        q4t = jnp.concatenate([qt[(kvh * rep + r) * HEAD_DIM:(kvh * rep + r + 1) * HEAD_DIM] for r in range(rep)],
                              axis=1)
        qas.append(jnp.concatenate([q4t * LOG2E, jnp.zeros((HEAD_DIM, nr), F32)], axis=0).astype(BF16))

    def body(c, carry):
        off = pl.multiple_of(c * ca, ca)
        bt = bias_ref[pl.ds(off, ca), :]
        b4 = jnp.concatenate([bt] * rep, axis=1)

        def scores(kvh):
            return jnp.dot(k_ref[0, pl.ds(off, ca), kvh * LANES:(kvh + 1) * LANES], qas[kvh],
                           preferred_element_type=F32) + b4

        def update(kvh, st):
            m, acc = carry[kvh]
            m_new = jnp.maximum(m, jnp.max(st, axis=0, keepdims=True))
            p = jnp.exp2(st - m_new)
            pv = jnp.dot(v_ref[0, kvh * VT_ROWS:(kvh + 1) * VT_ROWS, pl.ds(off, ca)], p.astype(BF16),
                         preferred_element_type=F32)
            return m_new, jnp.exp2(m - m_new) * acc + pv

        out = []
        st = scores(0)
        for kvh in range(HKV_C):
            nxt = scores(kvh + 1) if kvh + 1 < HKV_C else None
            out.append(update(kvh, st))
            st = nxt
        return tuple(out)

    init = tuple((jnp.full((1, nr), NEG_BIG, F32), jnp.zeros((VT_ROWS, nr), F32)) for _ in range(HKV_C))
    res = lax.fori_loop(0, ((i + 1) * tq + ca - 1) // ca, body, init)
    heads = []
    for kvh in range(HKV_C):
        acc = res[kvh][1]
        ot = acc[0:HEAD_DIM] / acc[HEAD_DIM:HEAD_DIM + 1]
        heads += [ot[:, r * tq:(r + 1) * tq] for r in range(rep)]
    o_ref[0] = jnp.concatenate(heads, axis=0).T


def _dsa_prompt(q, kaug16, vt16, qi, kiwi, kiwi16):
    b, t, _ = q.shape
    o = pl.pallas_call(
        _dsa_prompt_body,
        grid=(b, t // Q_TILE),
        in_specs=[pl.BlockSpec((1, Q_TILE, Q_C), lambda bb, i: (bb, i, 0)),
                  pl.BlockSpec((1, t, HKV_C * LANES), lambda bb, i: (bb, 0, 0)),
                  pl.BlockSpec((1, HKV_C * VT_ROWS, t), lambda bb, i: (bb, 0, 0)),
                  pl.BlockSpec((1, Q_TILE, N_IDX_HEADS * IDX_DIM), lambda bb, i: (bb, i, 0)),
                  pl.BlockSpec((1, Q_TILE, LANES), lambda bb, i: (bb, i, 0)),
                  pl.BlockSpec((1, t, LANES), lambda bb, i: (bb, 0, 0))],
        out_specs=pl.BlockSpec((1, Q_TILE, OUT_C), lambda bb, i: (bb, i, 0)),
        out_shape=jax.ShapeDtypeStruct((b, t, OUT_C), F32),
        scratch_shapes=[pltpu.VMEM((Q_TILE, t), I32), pltpu.VMEM((t, Q_TILE), F32), pltpu.VMEM((Q_TILE, 1), I32)],
        compiler_params=_params(("parallel", "arbitrary")),
        name="dsa_prompt",
    )(q, kaug16, vt16, qi, kiwi, kiwi16)
    return o.reshape(b * t, OUT_C)


def _dsa_sample_body(n_pages, pt_ref, q_ref, kn_ref, vn_ref, qi_ref, kiwi_ref, *refs):
    kp = refs[:n_pages]
    vp = refs[n_pages:2 * n_pages]
    ip = refs[2 * n_pages:3 * n_pages]
    o_ref = refs[3 * n_pages]
    rep = HQ_C // HKV_C
    ts = q_ref.shape[1]
    nrows = HQ_C * ts
    n_cache = n_pages * PAGE_SIZE
    kiwi = kiwi_ref[0]
    qi = qi_ref[0].astype(BF16)
    qi8 = jnp.concatenate([qi[:, h * IDX_DIM:(h + 1) * IDX_DIM] for h in range(N_IDX_HEADS)], axis=0)
    wv = kiwi[:, IDX_DIM:IDX_DIM + N_IDX_HEADS] * (IDX_DIM ** -0.5 * N_IDX_HEADS ** -0.5)
    ki_all = jnp.concatenate([ip[p][0].astype(BF16) for p in range(n_pages)]
                             + [_pad_rows(kiwi[:, 0:IDX_DIM], LANES).astype(BF16)], axis=0)
    sc = _index_scores(qi8, wv, ki_all, ts)
    col = lax.broadcasted_iota(I32, sc.shape, 1)
    trow = lax.broadcasted_iota(I32, sc.shape, 0)
    adm = (col < n_cache) | (col - n_cache <= trow)
    keys = jnp.where(adm, _sortable(sc), INT_MIN)

    def count(mask):
        return jnp.sum(mask.astype(I32), axis=1, keepdims=True)

    n_sel = min(DSA_TOPK, (n_cache + ts) // 4)
    thr = _kth_largest_key(lambda cand: count(keys >= cand), ts, n_sel)
    real = thr > INT_MIN
    need = n_sel - count(keys > thr)
    tie = (keys == thr) & real

    def step(b, jc):
        cand = jc + jnp.left_shift(jnp.int32(1), 13 - b)
        return jnp.where(count(tie & (col < cand)) < need, cand, jc)

    jmax = lax.fori_loop(0, 14, step, jnp.zeros((ts, 1), I32))
    keep = (keys > thr) | (tie & (col <= jmax))
    bias = jnp.where(keep, 0.0, NEG_BIG)
    bias = jnp.concatenate([bias] * HQ_C, axis=0)

    qf16 = _block_diag_queries(q_ref[0], HKV_C, rep).astype(BF16)
    kn16 = _pad_rows(kn_ref[0], LANES).astype(BF16)
    vn16 = _pad_rows(vn_ref[0], LANES).astype(BF16)
    s = jnp.concatenate([lax.dot_general(qf16, kp[p][0].astype(BF16), _NT, preferred_element_type=F32)
                         for p in range(n_pages)]
                        + [lax.dot_general(qf16, kn16, _NT, preferred_element_type=F32)], axis=1) + bias
    m = jnp.max(s, axis=1, keepdims=True)
    p_all = jnp.exp(s - m)
    l = jnp.sum(p_all, axis=1, keepdims=True)
    p16 = p_all.astype(BF16)
    acc = jnp.dot(p16[:, n_cache:n_cache + LANES], vn16, preferred_element_type=F32)
    for p in range(n_pages):
        acc = acc + jnp.dot(p16[:, p * PAGE_SIZE:(p + 1) * PAGE_SIZE], vp[p][0].astype(BF16),
                            preferred_element_type=F32)
    o_ref[0] = _unstack_block_diag(acc / l, HKV_C, rep, ts)


def _dsa_sample(q, kn, vn, qi, kiwi, cache_k, cache_v, cache_ki, page_table):
    db, ts, _ = q.shape
    n_pages = page_table.shape[1]
    n_pool = cache_k.shape[0]
    ck = cache_k.reshape(n_pool, PAGE_SIZE, KV_C)
    cv = cache_v.reshape(n_pool, PAGE_SIZE, KV_C)
    seq = lambda w: pl.BlockSpec((1, ts, w), lambda i, pt: (i, 0, 0))
    o = pl.pallas_call(
        functools.partial(_dsa_sample_body, n_pages),
        grid_spec=pltpu.PrefetchScalarGridSpec(
            num_scalar_prefetch=1,
            grid=(db,),
            in_specs=[seq(Q_C), seq(KV_C), seq(KV_C), seq(N_IDX_HEADS * IDX_DIM), seq(LANES)]
            + _page_specs(n_pages, KV_C) * 2 + _page_specs(n_pages, IDX_DIM),
            out_specs=seq(OUT_C),
        ),
        out_shape=jax.ShapeDtypeStruct((db, ts, OUT_C), F32),
        compiler_params=_params(("arbitrary",)),
        name="dsa_sample",
    )(page_table.reshape(-1), q, kn, vn, qi, kiwi,
      *([ck] * n_pages), *([cv] * n_pages), *([cache_ki] * n_pages))
    return o.reshape(db * ts, OUT_C)


def _rope_tables(pos):
    half = HEAD_DIM // 2
    inv_freq = ROPE_THETA ** (-jnp.arange(half, dtype=F32) / half)
    ang = pos.astype(F32)[:, None] * inv_freq[None, :]
    c, s = jnp.cos(ang), jnp.sin(ang)
    return jnp.tile(c, (1, LANES // half)), jnp.concatenate([-s, s] * (LANES // HEAD_DIM), axis=1)


def _gain128(g):
    return jnp.tile(g.astype(F32), LANES // HEAD_DIM).reshape(1, LANES)


def _layer_a(xp, xs, shp, shs, states, params, tabs_p, tabs_s):
    norm_g, w_in, q_gain, k_gain, w_out = params
    b, t = shp
    db, ts = shs
    w16 = w_in.astype(BF16)
    args = (norm_g.reshape(1, -1), w16)
    gains = (_gain128(q_gain), _gain128(k_gain))
    qp, kp, vp, kp16, vp16, sgp = _project("a", xp, *args, *tabs_p, *gains)
    qs, ks, vs, _, _, sgs = _project("a", xs, *args, *tabs_s, *gains)
    hw = HKV_A * HEAD_DIM
    o_p, l_p, o_s, l_s, new_state = [], [], [], [], []
    for g, (win, dil) in enumerate(DIL_GROUPS):
        o, lse = _dil_prompt(qp.reshape(b, t, Q_A), kp16.reshape(b, t, KV_A), vp16.reshape(b, t, KV_A), g, dil)
        o_p.append(o)
        l_p.append(lse)
        o, lse, ns = _dil_sample(qs.reshape(db, ts, Q_A), ks.reshape(db, ts, KV_A), vs.reshape(db, ts, KV_A),
                                 states[g], g, states[g].shape[1], dil)
        o_s.append(o)
        l_s.append(lse)
        wk = min(win, t)
        kg = kp.reshape(b, t, KV_A)[:, t - wk:, g * hw:(g + 1) * hw].reshape(b, wk, 1, HKV_A, HEAD_DIM)
        vg = vp.reshape(b, t, KV_A)[:, t - wk:, g * hw:(g + 1) * hw].reshape(b, wk, 1, HKV_A, HEAD_DIM)
        new_state.append(jnp.concatenate([kg, vg], axis=2))
        new_state.append(ns)
    w_out16 = w_out.astype(BF16)
    return _out_project(o_p, l_p, sgp, xp, w_out16), _out_project(o_s, l_s, sgs, xs, w_out16), new_state


def _layer_b(xp, xs, shp, shs, states, params, tabs_p, tabs_s, page_table):
    norm_g, w_in, q_gain, k_gain, w_out = params
    cache_k, cache_v = states
    b, t = shp
    db, ts = shs
    w16 = w_in.astype(BF16)
    args = (norm_g.reshape(1, -1), w16)
    gains = (_gain128(q_gain), _gain128(k_gain))
    qp, kp, vp, _, _, sgp, kmean, kaug16, vt16 = _project("b", xp, *args, *tabs_p, *gains, aug=True)
    qs, ks, vs, _, _, sgs = _project("b", xs, *args, *tabs_s, *gains)
    o_p = _moba_prompt(qp.reshape(b, t, Q_B), kaug16.reshape(b, t, HKV_B * LANES), vt16,
                       kmean.reshape(b, t // MOBA_BLOCK, KV_B))
    o_s = _moba_sample(qs.reshape(db, ts, Q_B), ks.reshape(db, ts, KV_B), vs.reshape(db, ts, KV_B),
                       cache_k, cache_v, page_table)
    w_out16 = w_out.astype(BF16)
    new_state = [kp.reshape(b, t, HKV_B, HEAD_DIM), ks.reshape(db, ts, HKV_B, HEAD_DIM),
                 vp.reshape(b, t, HKV_B, HEAD_DIM), vs.reshape(db, ts, HKV_B, HEAD_DIM)]
    return _out_project([o_p], None, sgp, xp, w_out16), _out_project([o_s], None, sgs, xs, w_out16), new_state


def _layer_c(xp, xs, shp, shs, states, params, tabs_p, tabs_s, page_table):
    norm_g, w_in, q_gain, k_gain, w_out = params
    cache_k, cache_v, cache_ki = states
    b, t = shp
    db, ts = shs
    w16 = jnp.pad(w_in, ((0, 0), (0, IN_C_PAD - IN_C))).astype(BF16)
    args = (norm_g.reshape(1, -1), w16)
    gains = (_gain128(q_gain), _gain128(k_gain))
    qp, kp, vp, _, _, sgp, qip, kiwip, kiwip16, kaug16, vt16 = _project("c", xp, *args, *tabs_p, *gains, aug=True)
    qs, ks, vs, _, _, sgs, qis, kiwis, _ = _project("c", xs, *args, *tabs_s, *gains)
    o_p = _dsa_prompt(qp.reshape(b, t, Q_C), kaug16.reshape(b, t, HKV_C * LANES), vt16,
                      qip.reshape(b, t, -1), kiwip.reshape(b, t, LANES), kiwip16.reshape(b, t, LANES))
    o_s = _dsa_sample(qs.reshape(db, ts, Q_C), ks.reshape(db, ts, KV_C), vs.reshape(db, ts, KV_C),
                      qis.reshape(db, ts, -1), kiwis.reshape(db, ts, LANES), cache_k, cache_v, cache_ki, page_table)
    w_out16 = w_out.astype(BF16)
    new_state = [kp.reshape(b, t, HKV_C, HEAD_DIM), ks.reshape(db, ts, HKV_C, HEAD_DIM),
                 vp.reshape(b, t, HKV_C, HEAD_DIM), vs.reshape(db, ts, HKV_C, HEAD_DIM),
                 kiwip.reshape(b, t, LANES)[:, :, :IDX_DIM], kiwis.reshape(db, ts, LANES)[:, :, :IDX_DIM]]
    return _out_project([o_p], None, sgp, xp, w_out16), _out_project([o_s], None, sgs, xs, w_out16), new_state


def kernel(x_prompt, x_sample, state_l0_kv_w128, state_l0_kv_w512, state_l0_kv_w2048, cache_l1_k, cache_l1_v, cache_l2_k, cache_l2_v, cache_l2_kidx, state_l3_kv_w128, state_l3_kv_w512, state_l3_kv_w2048, page_table, l0_norm, l0_w_in, l0_q_norm, l0_k_norm, l0_w_out, l1_norm, l1_w_in, l1_q_norm, l1_k_norm, l1_w_out, l2_norm, l2_w_in, l2_q_norm, l2_k_norm, l2_w_out, l3_norm, l3_w_in, l3_q_norm, l3_k_norm, l3_w_out):
    b, t, _ = x_prompt.shape
    db, ts, _ = x_sample.shape
    assert t % (16 * Q_TILE) == 0 and t % DSA_CHUNK == 0 and (db * ts) % PROJ_ROWS == 0 and PROJ_ROWS % ts == 0
    tabs_p = _rope_tables(jnp.arange(t))
    tabs_s = _rope_tables(PAST_LEN + (jnp.arange(PROJ_ROWS) % ts))
    xp = x_prompt.reshape(b * t, D_MODEL)
    xs = x_sample.reshape(db * ts, D_MODEL)
    states = ((state_l0_kv_w128, state_l0_kv_w512, state_l0_kv_w2048), (cache_l1_k, cache_l1_v),
              (cache_l2_k, cache_l2_v, cache_l2_kidx), (state_l3_kv_w128, state_l3_kv_w512, state_l3_kv_w2048))
    params = ((l0_norm, l0_w_in, l0_q_norm, l0_k_norm, l0_w_out), (l1_norm, l1_w_in, l1_q_norm, l1_k_norm, l1_w_out),
              (l2_norm, l2_w_in, l2_q_norm, l2_k_norm, l2_w_out), (l3_norm, l3_w_in, l3_q_norm, l3_k_norm, l3_w_out))
    new_state = []
    for i in range(4):
        common = (xp, xs, (b, t), (db, ts), states[i], params[i], tabs_p, tabs_s)
        if i % 3 == 0:
            xp, xs, st = _layer_a(*common)
        elif i % 3 == 1:
            xp, xs, st = _layer_b(*common, page_table)
        else:
            xp, xs, st = _layer_c(*common, page_table)
        new_state.extend(st)
    return (xp.reshape(b, t, D_MODEL), xs.reshape(db, ts, D_MODEL), *new_state)
```

```python
import functools

import jax
import jax.numpy as jnp
from jax import lax
from jax.experimental import pallas as pl
from jax.experimental.pallas import tpu as pltpu

F32 = jnp.float32
BF16 = jnp.bfloat16
I32 = jnp.int32

D_MODEL = 1024
PAST_LEN = 2048
PAGE_SIZE = 128
HEAD_DIM = 64
ROPE_THETA = 10000.0
NORM_EPS = 1e-6

DIL_GROUPS = ((128, 1), (512, 4), (2048, 16))
HQ_A, HKV_A = 8, 2
Q_A, KV_A, OUT_A = 1536, 384, 512

HQ_B, HKV_B = 16, 4
MOBA_BLOCK, MOBA_TOPK = 256, 3
Q_B, KV_B, OUT_B = 1024, 256, 1024

HQ_C, HKV_C = 16, 4
N_IDX_HEADS, IDX_DIM, DSA_TOPK = 8, 64, 256
Q_C, KV_C, OUT_C = 1024, 256, 1024
IN_C = Q_C + 2 * KV_C + OUT_C + N_IDX_HEADS * IDX_DIM + IDX_DIM + N_IDX_HEADS
IN_C_PAD = 3200

LANES = 128
PROJ_ROWS = 256
Q_TILE = 128
DSA_CHUNK = 512
VMEM_LIMIT = 56 * 1024 * 1024
NEG_BIG = -1e30
LOG2E = 1.4426950408889634
VT_ROWS = 80
INT_MIN = -2147483648

_NT = (((1,), (1,)), ((), ()))


def _params(sem):
    return pltpu.CompilerParams(dimension_semantics=sem, vmem_limit_bytes=VMEM_LIMIT)


def _group_sum_matrix():
    r = lax.broadcasted_iota(I32, (LANES, LANES), 0) // HEAD_DIM
    c = lax.broadcasted_iota(I32, (LANES, LANES), 1) // HEAD_DIM
    return (r == c).astype(BF16)


def _rope(y, cos, sin):
    lane = lax.broadcasted_iota(I32, y.shape, 1)
    first_half = (lane & (HEAD_DIM // 2)) == 0
    partner = jnp.where(first_half, pltpu.roll(y, LANES - HEAD_DIM // 2, 1), pltpu.roll(y, HEAD_DIM // 2, 1))
    return y * cos + partner * sin


def _head_norm(x, gain, gmat):
    ss = x * x
    hi = ss.astype(BF16)
    lo = (ss - hi.astype(F32)).astype(BF16)
    gs = jnp.dot(hi, gmat, preferred_element_type=F32) + jnp.dot(lo, gmat, preferred_element_type=F32)
    return x * lax.rsqrt(gs * (1.0 / HEAD_DIM) + NORM_EPS) * gain


def _proj_body(kind, aug_tiles, x_ref, g_ref, w_ref, cos_ref, sin_ref, qg_ref, kg_ref, *outs):
    x = x_ref[...]
    ms = jnp.mean(x * x, axis=-1, keepdims=True)
    h = (x * lax.rsqrt(ms + NORM_EPS) * g_ref[...]).astype(BF16)
    cos = cos_ref[...]
    sin = sin_ref[...]
    qg = qg_ref[...]
    kg = kg_ref[...]
    gmat = _group_sum_matrix()
    qw, kvw, gw = {"a": (Q_A, KV_A, OUT_A), "b": (Q_B, KV_B, OUT_B), "c": (Q_C, KV_C, OUT_C)}[kind]
    q_ref, k_ref, v_ref, k16_ref, v16_ref, sg_ref = outs[:6]
    rest = outs[6:]
    moba = aug_tiles and kind == "b"
    if aug_tiles:
        kaug_ref, vt_ref = rest[-2:]

    def seg(start, width):
        return jnp.dot(h, w_ref[:, start:start + width], preferred_element_type=F32)

    zq = seg(0, qw)
    for c in range(qw // LANES):
        y = _rope(_head_norm(zq[:, c * LANES:(c + 1) * LANES], qg, gmat), cos, sin)
        q_ref[:, c * LANES:(c + 1) * LANES] = y * (HEAD_DIM ** -0.5)
    zk = seg(qw, kvw)
    ksum = []
    for c in range(kvw // LANES):
        y = _rope(_head_norm(zk[:, c * LANES:(c + 1) * LANES], kg, gmat), cos, sin)
        k_ref[:, c * LANES:(c + 1) * LANES] = y
        k16_ref[:, c * LANES:(c + 1) * LANES] = y.astype(BF16)
        if moba:
            ksum.append(jnp.sum(y, axis=0, keepdims=True) * (1.0 / MOBA_BLOCK))
        if aug_tiles:
            lane = lax.broadcasted_iota(I32, y.shape, 1)
            hot = HEAD_DIM + pl.program_id(0) % aug_tiles if moba else -1
            onehot = (lane == hot).astype(F32)
            kaug_ref[:, 2 * c * LANES:(2 * c + 1) * LANES] = jnp.where(lane < HEAD_DIM, y, onehot).astype(BF16)
            kaug_ref[:, (2 * c + 1) * LANES:(2 * c + 2) * LANES] = jnp.where(
                lane < HEAD_DIM, pltpu.roll(y, HEAD_DIM, 1), onehot).astype(BF16)
    zv = seg(qw + kvw, kvw)
    v_ref[...] = zv
    v16_ref[...] = zv.astype(BF16)
    if aug_tiles:
        zvt = zv.T
        tail = (lax.broadcasted_iota(I32, (VT_ROWS - HEAD_DIM, zvt.shape[1]), 0) == 0).astype(BF16)
        for kvh in range(kvw // HEAD_DIM):
            vt_ref[0, kvh * VT_ROWS:kvh * VT_ROWS + HEAD_DIM, :] = zvt[kvh * HEAD_DIM:(kvh + 1) * HEAD_DIM].astype(BF16)
            vt_ref[0, kvh * VT_ROWS + HEAD_DIM:(kvh + 1) * VT_ROWS, :] = tail
    zg = seg(qw + 2 * kvw, gw)
    sg_ref[...] = zg / (1.0 + jnp.exp(-zg))
    nrest = 0
    if kind == "c":
        qi_ref, kiwi_ref, kiwi16_ref = rest[:3]
        nrest = 3
        base = qw + 2 * kvw + gw
        zi = seg(base, N_IDX_HEADS * IDX_DIM)
        for c in range(N_IDX_HEADS * IDX_DIM // LANES):
            qi_ref[:, c * LANES:(c + 1) * LANES] = _rope(zi[:, c * LANES:(c + 1) * LANES], cos, sin)
        zz = seg(base + N_IDX_HEADS * IDX_DIM, LANES)
        lane = lax.broadcasted_iota(I32, zz.shape, 1)
        kiwi = jnp.where(lane < IDX_DIM, _rope(zz, cos, sin), zz)
        kiwi_ref[...] = kiwi
        kiwi16_ref[...] = kiwi.astype(BF16)
    if moba:
        km_ref = rest[0]
        for c in range(kvw // LANES):
            km_ref[0, :, c * LANES:(c + 1) * LANES] = ksum[c]


def _project(kind, x, gain, w16, cos, sin, qg, kg, aug=False):
    n = x.shape[0]
    tm = PROJ_ROWS
    nt = cos.shape[0] // tm
    qw, kvw, gw = {"a": (Q_A, KV_A, OUT_A), "b": (Q_B, KV_B, OUT_B), "c": (Q_C, KV_C, OUT_C)}[kind]
    wp = w16.shape[1]

    def rows(width):
        return pl.BlockSpec((tm, width), lambda i: (i, 0))

    def const(shape):
        return pl.BlockSpec(shape, lambda i: (0,) * len(shape))

    out_shape = [jax.ShapeDtypeStruct((n, qw), F32), jax.ShapeDtypeStruct((n, kvw), F32),
                 jax.ShapeDtypeStruct((n, kvw), F32), jax.ShapeDtypeStruct((n, kvw), BF16),
                 jax.ShapeDtypeStruct((n, kvw), BF16), jax.ShapeDtypeStruct((n, gw), F32)]
    out_specs = [rows(qw), rows(kvw), rows(kvw), rows(kvw), rows(kvw), rows(gw)]
    if kind == "c":
        out_shape += [jax.ShapeDtypeStruct((n, N_IDX_HEADS * IDX_DIM), F32),
                      jax.ShapeDtypeStruct((n, LANES), F32), jax.ShapeDtypeStruct((n, LANES), BF16)]
        out_specs += [rows(N_IDX_HEADS * IDX_DIM), rows(LANES), rows(LANES)]
    if aug:
        n_kv = kvw // HEAD_DIM
        if kind == "b":
            out_shape.append(jax.ShapeDtypeStruct((n // tm, 1, kvw), F32))
            out_specs.append(pl.BlockSpec((1, 1, kvw), lambda i: (i, 0, 0)))
        out_shape += [jax.ShapeDtypeStruct((n, n_kv * LANES), BF16),
                      jax.ShapeDtypeStruct((n // (nt * tm), n_kv * VT_ROWS, nt * tm), BF16)]
        out_specs += [rows(n_kv * LANES), pl.BlockSpec((1, n_kv * VT_ROWS, tm), lambda i: (i // nt, 0, i % nt))]
    return pl.pallas_call(
        functools.partial(_proj_body, kind, nt if aug else 0),
        grid=(n // tm,),
        in_specs=[rows(D_MODEL), const((1, D_MODEL)), const((D_MODEL, wp)),
                  pl.BlockSpec((tm, LANES), lambda i: (i % nt, 0)),
                  pl.BlockSpec((tm, LANES), lambda i: (i % nt, 0)),
                  const((1, LANES)), const((1, LANES))],
        out_specs=out_specs,
        out_shape=out_shape,
        compiler_params=_params(("parallel",)),
        name=f"proj_{kind}",
    )(x, gain, w16, cos, sin, qg, kg)


LSE_REP = LANES // HQ_A
O_SLABS = OUT_A // LANES


def _outproj_body(n_groups, *refs):
    per = O_SLABS + 1
    sg_ref, x_ref, w_ref, y_ref = refs[-4:]
    if n_groups > 1:
        lses = [refs[g * per + O_SLABS][...] for g in range(n_groups)]
        m = functools.reduce(jnp.maximum, lses)
        es = [jnp.exp(l - m) for l in lses]
        tot = functools.reduce(lambda a, b: a + b, es)
        lane = lax.broadcasted_iota(I32, m.shape, 1)
        slabs = []
        for c in range(O_SLABS):
            acc = None
            for g in range(n_groups):
                wg = es[g] / tot
                lo = wg[:, 2 * c * LSE_REP:2 * c * LSE_REP + 1]
                hi = wg[:, (2 * c + 1) * LSE_REP:(2 * c + 1) * LSE_REP + 1]
                term = jnp.where(lane < HEAD_DIM, lo, hi) * refs[g * per + c][...]
                acc = term if acc is None else acc + term
            slabs.append(acc)
        o = jnp.concatenate(slabs, axis=1)
    else:
        o = refs[0][...]
    g16 = (o * sg_ref[...]).astype(BF16)
    y_ref[...] = x_ref[...] + jnp.dot(g16, w_ref[...], preferred_element_type=F32)


def _out_project(groups, sg, x, w16):
    n = x.shape[0]
    f = sg.shape[1]
    tm = PROJ_ROWS
    ins = [a for grp in groups for a in grp]
    in_specs = [pl.BlockSpec((tm, a.shape[1]), lambda i: (i, 0)) for a in ins]
    in_specs += [pl.BlockSpec((tm, f), lambda i: (i, 0)), pl.BlockSpec((tm, D_MODEL), lambda i: (i, 0)),
                 pl.BlockSpec((f, D_MODEL), lambda i: (0, 0))]
    return pl.pallas_call(
        functools.partial(_outproj_body, len(groups)),
        grid=(n // tm,),
        in_specs=in_specs,
        out_specs=pl.BlockSpec((tm, D_MODEL), lambda i: (i, 0)),
        out_shape=jax.ShapeDtypeStruct((n, D_MODEL), F32),
        compiler_params=_params(("parallel",)),
        name="out_proj_merge" if len(groups) > 1 else "out_proj",
    )(*ins, sg, x, w16)


def _stack_heads(q, kvh, rep):
    return jnp.concatenate([q[:, (kvh * rep + r) * HEAD_DIM:(kvh * rep + r + 1) * HEAD_DIM] for r in range(rep)],
                           axis=0)


FLASH_BLOCK = 256


def _flash_pair(k_ref, v_ref, qas, carry, n, bias_ref):
    carry = list(carry)
    units = [(j, kvh) for j in range(2) for kvh in range(len(qas))]
    offs = [pl.multiple_of((2 * n + j) * FLASH_BLOCK, FLASH_BLOCK) for j in range(2)]
    rep = qas[0].shape[1] // Q_TILE
    scores = []
    for j, kvh in units:
        st = jnp.dot(k_ref[0, pl.ds(offs[j], FLASH_BLOCK), kvh * LANES:(kvh + 1) * LANES], qas[kvh],
                     preferred_element_type=F32)
        if bias_ref is not None:
            st = st + jnp.concatenate([bias_ref[pl.ds(offs[j], FLASH_BLOCK), :]] * rep, axis=1)
        scores.append(st)
    for (j, kvh), st in zip(units, scores):
        m, acc = carry[kvh]
        m_new = jnp.maximum(m, jnp.max(st, axis=0, keepdims=True))
        p = jnp.exp2(st - m_new)
        pv = jnp.dot(v_ref[0, kvh * VT_ROWS:(kvh + 1) * VT_ROWS, pl.ds(offs[j], FLASH_BLOCK)], p.astype(BF16),
                     preferred_element_type=F32)
        carry[kvh] = (m_new, jnp.exp2(m - m_new) * acc + pv)
    return tuple(carry)


def _lse_slab(lses, rows):
    return jnp.concatenate([jnp.broadcast_to(l, (rows, LSE_REP)) for l in lses], axis=1)


def _dil_prompt_body(dil, *refs):
    q_refs = refs[:O_SLABS]
    kp_ref, kc_ref, vp_ref, vc_ref = refs[O_SLABS:O_SLABS + 4]
    o_refs = refs[O_SLABS + 4:2 * O_SLABS + 4]
    lse_ref = refs[2 * O_SLABS + 4]
    j = pl.program_id(1)
    tq = Q_TILE
    rep = HQ_A // HKV_A
    row = lax.broadcasted_iota(I32, (tq, 2 * tq), 0)
    col = lax.broadcasted_iota(I32, (tq, 2 * tq), 1)
    dist = row + tq - col
    valid = (dist >= 0) & (dist <= tq) & ((col >= tq) | (j > 0))
    valid = jnp.concatenate([valid] * rep, axis=0)

    def residue(r, _):
        rows = pl.ds(r, tq, stride=dil) if dil > 1 else pl.ds(0, tq)
        q = jnp.concatenate([qr[0, rows, :] for qr in q_refs], axis=1).astype(BF16)
        kk = jnp.concatenate([kp_ref[0, rows, :], kc_ref[0, rows, :]], axis=0).astype(BF16)
        vv = jnp.concatenate([vp_ref[0, rows, :], vc_ref[0, rows, :]], axis=0).astype(BF16)
        outs, lses = [], []
        for kvh in range(HKV_A):
            kh = kk[:, kvh * HEAD_DIM:(kvh + 1) * HEAD_DIM]
            vh = vv[:, kvh * HEAD_DIM:(kvh + 1) * HEAD_DIM]
            qh = _stack_heads(q, kvh, rep)
            s = lax.dot_general(qh, kh, _NT, preferred_element_type=F32)
            s = jnp.where(valid, s, -jnp.inf)
            m = jnp.max(s, axis=1, keepdims=True)
            p = jnp.exp(s - m)
            l = jnp.sum(p, axis=1, keepdims=True)
            o = jnp.dot(p.astype(BF16), vh, preferred_element_type=F32) / l
            lse = m + jnp.log(l)
            for h in range(rep):
                outs.append(o[h * tq:(h + 1) * tq])
                lses.append(lse[h * tq:(h + 1) * tq])
        for c in range(O_SLABS):
            o_refs[c][0, rows, :] = jnp.concatenate(outs[2 * c:2 * c + 2], axis=1)
        lse_ref[0, rows, :] = _lse_slab(lses, tq)
        return 0

    if dil == 1:
        residue(0, 0)
    else:
        lax.fori_loop(0, dil, residue, 0)


def _dil_prompt(q, k, v, g, dil):
    b, t, _ = q.shape
    blk = Q_TILE * dil
    hw = HKV_A * HEAD_DIM
    cur = lambda bb, j: (bb, j, g)
    prev = lambda bb, j: (bb, jnp.maximum(j - 1, 0), g)
    slab = pl.BlockSpec((1, blk, LANES), lambda bb, j: (bb, j, 0))
    outs = pl.pallas_call(
        functools.partial(_dil_prompt_body, dil),
        grid=(b, t // blk),
        in_specs=[pl.BlockSpec((1, blk, LANES), functools.partial(lambda c, bb, j: (bb, j, g * O_SLABS + c), c))
                  for c in range(O_SLABS)]
        + [pl.BlockSpec((1, blk, hw), prev), pl.BlockSpec((1, blk, hw), cur),
           pl.BlockSpec((1, blk, hw), prev), pl.BlockSpec((1, blk, hw), cur)],
        out_specs=[slab] * (O_SLABS + 1),
        out_shape=[jax.ShapeDtypeStruct((b, t, LANES), F32)] * (O_SLABS + 1),
        compiler_params=_params(("parallel", "arbitrary")),
        name=f"dil_prompt_g{g}",
    )(*([q] * O_SLABS), k, k, v, v)
    return [a.reshape(b * t, LANES) for a in outs]


def _dil_sample_body(win, dil, sb, q_ref, kn_ref, vn_ref, st_ref, *out_refs):
    o_refs = out_refs[:O_SLABS]
    lse_ref, ns_ref = out_refs[O_SLABS:]
    rep = HQ_A // HKV_A
    hw = HKV_A * HEAD_DIM
    ts = q_ref.shape[1]
    row = lax.broadcasted_iota(I32, (rep * ts, win), 0)
    col = lax.broadcasted_iota(I32, (rep * ts, win), 1)
    dist = win + (row & (ts - 1)) - col
    valid_c = (dist <= win) & ((dist & (dil - 1)) == 0)
    rown = lax.broadcasted_iota(I32, (rep * ts, LANES), 0)
    coln = lax.broadcasted_iota(I32, (rep * ts, LANES), 1)
    distn = (rown & (ts - 1)) - coln
    valid_n = (distn >= 0) & ((distn & (dil - 1)) == 0)
    lane_w = lax.broadcasted_iota(I32, (2 * hw, LANES), 1)
    for s_i in range(sb):
        st = st_ref[s_i]
        knp = _pad_rows(kn_ref[s_i], LANES)
        vnp = _pad_rows(vn_ref[s_i], LANES)
        shifted = pltpu.roll(st, win - ts, 1)
        newcols = pltpu.roll(jnp.concatenate([knp.T, vnp.T], axis=0), LANES - ts, 1)
        if win > LANES:
            ns_ref[s_i, :, 0:win - LANES] = shifted[:, 0:win - LANES]
        ns_ref[s_i, :, win - LANES:win] = jnp.where(lane_w >= LANES - ts, newcols, shifted[:, win - LANES:win])
        q = q_ref[s_i].astype(BF16)
        kn16 = knp.astype(BF16)
        vn16 = vnp.astype(BF16)
        outs, lses = [], []
        for kvh in range(HKV_A):
            hs = slice(kvh * HEAD_DIM, (kvh + 1) * HEAD_DIM)
            kt = st[kvh * HEAD_DIM:(kvh + 1) * HEAD_DIM].astype(BF16)
            vt = st[hw + kvh * HEAD_DIM:hw + (kvh + 1) * HEAD_DIM].astype(BF16)
            qh = _stack_heads(q, kvh, rep)
            s_c = jnp.where(valid_c, jnp.dot(qh, kt, preferred_element_type=F32), -jnp.inf)
            s_n = jnp.where(valid_n, lax.dot_general(qh, kn16[:, hs], _NT, preferred_element_type=F32), -jnp.inf)
            m = jnp.maximum(jnp.max(s_c, axis=1, keepdims=True), jnp.max(s_n, axis=1, keepdims=True))
            p_c = jnp.exp(s_c - m)
            p_n = jnp.exp(s_n - m)
            l = jnp.sum(p_c, axis=1, keepdims=True) + jnp.sum(p_n, axis=1, keepdims=True)
            o = (lax.dot_general(p_c.astype(BF16), vt, _NT, preferred_element_type=F32)
                 + jnp.dot(p_n.astype(BF16), vn16[:, hs], preferred_element_type=F32)) / l
            lse = m + jnp.log(l)
            for r in range(rep):
                outs.append(o[r * ts:(r + 1) * ts])
                lses.append(lse[r * ts:(r + 1) * ts])
        for c in range(O_SLABS):
            o_refs[c][s_i] = jnp.concatenate(outs[2 * c:2 * c + 2], axis=1)
        lse_ref[s_i] = _lse_slab(lses, ts)


def _dil_sample(q, kn, vn, state, g, win, dil):
    db, ts, _ = q.shape
    hw = HKV_A * HEAD_DIM
    st = state.transpose(0, 2, 3, 4, 1).reshape(db, 2 * hw, win)
    sb = max(1, min(8, 1024 // win))
    slab = pl.BlockSpec((sb, ts, LANES), lambda i: (i, 0, 0))
    *slabs, ns = pl.pallas_call(
        functools.partial(_dil_sample_body, win, dil, sb),
        grid=(db // sb,),
        in_specs=[pl.BlockSpec((sb, ts, OUT_A), lambda i: (i, 0, g)),
                  pl.BlockSpec((sb, ts, hw), lambda i: (i, 0, g)),
                  pl.BlockSpec((sb, ts, hw), lambda i: (i, 0, g)),
                  pl.BlockSpec((sb, 2 * hw, win), lambda i: (i, 0, 0))],
        out_specs=[slab] * (O_SLABS + 1) + [pl.BlockSpec((sb, 2 * hw, win), lambda i: (i, 0, 0))],
        out_shape=[jax.ShapeDtypeStruct((db, ts, LANES), F32)] * (O_SLABS + 1)
        + [jax.ShapeDtypeStruct((db, 2 * hw, win), F32)],
        compiler_params=_params(("parallel",)),
        name=f"dil_sample_g{g}",
    )(q, kn, vn, st)
    ns = ns.reshape(db, 2, HKV_A, HEAD_DIM, win).transpose(0, 4, 1, 2, 3)
    return [a.reshape(db * ts, LANES) for a in slabs], ns


def _top_blocks(bs, n_valid, topk):
    lane = lax.broadcasted_iota(I32, bs.shape, 1)
    nb = bs.shape[1]
    work = jnp.where(lane < n_valid, bs, -jnp.inf)
    chosen = jnp.zeros(bs.shape, jnp.bool_)
    for _ in range(topk):
        m = jnp.max(work, axis=1, keepdims=True)
        idx = jnp.min(jnp.where(work == m, lane, nb), axis=1, keepdims=True)
        pick = lane == idx
        chosen = chosen | pick
        work = jnp.where(pick, -jnp.inf, work)
    return chosen & (lane < n_valid)


def _top_blocks_t(bs, n_valid, topk):
    row = lax.broadcasted_iota(I32, bs.shape, 0)
    rowf = row.astype(F32)
    work = jnp.where(row < n_valid, bs, -jnp.inf)
    chosen = jnp.zeros(bs.shape, jnp.bool_)
    for _ in range(topk):
        m = jnp.max(work, axis=0, keepdims=True)
        idx = jnp.min(jnp.where(work == m, rowf, float(bs.shape[0])), axis=0, keepdims=True)
        pick = rowf == idx
        chosen = chosen | pick
        work = jnp.where(pick, -jnp.inf, work)
    return chosen & (row < n_valid)


def _moba_prompt_body(q_ref, k_ref, v_ref, km_ref, o_ref):
    i = pl.program_id(1)
    tq = Q_TILE
    rep = HQ_B // HKV_B
    nb = km_ref.shape[1]
    nr = rep * tq
    own = (i * tq) // MOBA_BLOCK
    own0 = pl.multiple_of(own * MOBA_BLOCK, MOBA_BLOCK)
    qt = q_ref[0].T
    km = km_ref[0]
    krow = lax.broadcasted_iota(I32, (MOBA_BLOCK, nr), 0)
    qcol = lax.broadcasted_iota(I32, (MOBA_BLOCK, nr), 1)
    causal = own0 + krow <= i * tq + (qcol & (tq - 1))
    brow = lax.broadcasted_iota(I32, (HEAD_DIM, nr), 0)
    qas_past, init = [], []
    for kvh in range(HKV_B):
        hs = slice(kvh * HEAD_DIM, (kvh + 1) * HEAD_DIM)
        ls = slice(kvh * LANES, (kvh + 1) * LANES)
        vs = slice(kvh * VT_ROWS, (kvh + 1) * VT_ROWS)
        q4t = jnp.concatenate([qt[(kvh * rep + r) * HEAD_DIM:(kvh * rep + r + 1) * HEAD_DIM] for r in range(rep)],
                              axis=1)
        bst = jnp.dot(km[:, hs], q4t, precision=lax.Precision.HIGHEST, preferred_element_type=F32)
        if nb < HEAD_DIM:
            bst = jnp.concatenate([bst, jnp.zeros((HEAD_DIM - nb, nr), F32)], axis=0)
        chosen = _top_blocks_t(bst, own, MOBA_TOPK)
        qs = q4t * LOG2E
        qas_past.append(jnp.concatenate([qs, jnp.where(chosen, 0.0, NEG_BIG)], axis=0).astype(BF16))
        qa = jnp.concatenate([qs, jnp.where(brow == own, 0.0, NEG_BIG)], axis=0).astype(BF16)
        st = jnp.dot(k_ref[0, pl.ds(own0, MOBA_BLOCK), ls], qa, preferred_element_type=F32)
        st = jnp.where(causal, st, -jnp.inf)
        m = jnp.max(st, axis=0, keepdims=True)
        p = jnp.exp2(st - m)
        init.append((m, jnp.dot(v_ref[0, vs, pl.ds(own0, MOBA_BLOCK)], p.astype(BF16), preferred_element_type=F32)))

    res = lax.fori_loop(0, (own + 1) // 2,
                        lambda n, carry: _flash_pair(k_ref, v_ref, qas_past, carry, n, None), tuple(init))
    heads = []
    for kvh in range(HKV_B):
        acc = res[kvh][1]
        ot = acc[0:HEAD_DIM] / acc[HEAD_DIM:HEAD_DIM + 1]
        heads += [ot[:, r * tq:(r + 1) * tq] for r in range(rep)]
    o_ref[0] = jnp.concatenate(heads, axis=0).T


def _moba_prompt(q, kaug16, vt16, kmean):
    b, t, _ = q.shape
    nb = t // MOBA_BLOCK
    o = pl.pallas_call(
        _moba_prompt_body,
        grid=(b, t // Q_TILE),
        in_specs=[pl.BlockSpec((1, Q_TILE, Q_B), lambda bb, i: (bb, i, 0)),
                  pl.BlockSpec((1, t, HKV_B * LANES), lambda bb, i: (bb, 0, 0)),
                  pl.BlockSpec((1, HKV_B * VT_ROWS, t), lambda bb, i: (bb, 0, 0)),
                  pl.BlockSpec((1, nb, KV_B), lambda bb, i: (bb, 0, 0))],
        out_specs=pl.BlockSpec((1, Q_TILE, OUT_B), lambda bb, i: (bb, i, 0)),
        out_shape=jax.ShapeDtypeStruct((b, t, OUT_B), F32),
        compiler_params=_params(("parallel", "arbitrary")),
        name="moba_prompt",
    )(q, kaug16, vt16, kmean)
    return o.reshape(b * t, OUT_B)


def _block_diag_queries(q, n_kv, rep):
    ts = q.shape[0]
    rows = []
    for kvh in range(n_kv):
        for r in range(rep):
            hd = (kvh * rep + r) * HEAD_DIM
            parts = []
            if kvh > 0:
                parts.append(jnp.zeros((ts, kvh * HEAD_DIM), q.dtype))
            parts.append(q[:, hd:hd + HEAD_DIM])
            if kvh < n_kv - 1:
                parts.append(jnp.zeros((ts, (n_kv - 1 - kvh) * HEAD_DIM), q.dtype))
            rows.append(jnp.concatenate(parts, axis=1))
    return jnp.concatenate(rows, axis=0)


def _unstack_block_diag(o_all, n_kv, rep, ts):
    parts = []
    for kvh in range(n_kv):
        for r in range(rep):
            r0 = (kvh * rep + r) * ts
            parts.append(o_all[r0:r0 + ts, kvh * HEAD_DIM:(kvh + 1) * HEAD_DIM])
    return jnp.concatenate(parts, axis=1)


def _pad_rows(x, rows):
    return jnp.concatenate([x, jnp.zeros((rows - x.shape[0], x.shape[1]), x.dtype)], axis=0)


def _moba_sample_body(n_pages, pt_ref, q_ref, kn_ref, vn_ref, *refs):
    kp = refs[:n_pages]
    vp = refs[n_pages:2 * n_pages]
    o_ref = refs[2 * n_pages]
    rep = HQ_B // HKV_B
    ts = q_ref.shape[1]
    nrows = HQ_B * ts
    ppb = MOBA_BLOCK // PAGE_SIZE
    n_past = n_pages // ppb
    qf = _block_diag_queries(q_ref[0], HKV_B, rep)
    qf16 = qf.astype(BF16)
    kms = []
    for n in range(n_past):
        tot = kp[n * ppb][0]
        for j in range(1, ppb):
            tot = tot + kp[n * ppb + j][0]
        kms.append(jnp.sum(tot, axis=0, keepdims=True) * (1.0 / MOBA_BLOCK))
    km = _pad_rows(jnp.concatenate(kms, axis=0), LANES)
    bs = lax.dot_general(qf, km, _NT, precision=lax.Precision.HIGHEST, preferred_element_type=F32)
    chosen = _top_blocks(bs, n_past, MOBA_TOPK)
    selb = jnp.where(chosen, 0.0, NEG_BIG)
    bias = jnp.concatenate([jnp.broadcast_to(selb[:, n:n + 1], (nrows, MOBA_BLOCK)) for n in range(n_past)], axis=1)
    s_c = jnp.concatenate([lax.dot_general(qf16, kp[p][0].astype(BF16), _NT, preferred_element_type=F32)
                           for p in range(n_pages)], axis=1) + bias
    kn16 = _pad_rows(kn_ref[0], LANES).astype(BF16)
    vn16 = _pad_rows(vn_ref[0], LANES).astype(BF16)
    s_n = lax.dot_general(qf16, kn16, _NT, preferred_element_type=F32)
    row = lax.broadcasted_iota(I32, (nrows, LANES), 0)
    col = lax.broadcasted_iota(I32, (nrows, LANES), 1)
    s_n = jnp.where(col <= (row & (ts - 1)), s_n, -jnp.inf)
    m = jnp.maximum(jnp.max(s_c, axis=1, keepdims=True), jnp.max(s_n, axis=1, keepdims=True))
    p_c = jnp.exp(s_c - m)
    p_n = jnp.exp(s_n - m)
    l = jnp.sum(p_c, axis=1, keepdims=True) + jnp.sum(p_n, axis=1, keepdims=True)
    acc = jnp.dot(p_n.astype(BF16), vn16, preferred_element_type=F32)
    p16 = p_c.astype(BF16)
    for p in range(n_pages):
        acc = acc + jnp.dot(p16[:, p * PAGE_SIZE:(p + 1) * PAGE_SIZE], vp[p][0].astype(BF16),
                            preferred_element_type=F32)
    o_ref[0] = _unstack_block_diag(acc / l, HKV_B, rep, ts)


def _page_specs(n_pages, width):
    return [pl.BlockSpec((1, PAGE_SIZE, width), functools.partial(lambda p, i, pt: (pt[i * n_pages + p], 0, 0), p))
            for p in range(n_pages)]


def _moba_sample(q, kn, vn, cache_k, cache_v, page_table):
    db, ts, _ = q.shape
    n_pages = page_table.shape[1]
    n_pool = cache_k.shape[0]
    ck = cache_k.reshape(n_pool, PAGE_SIZE, KV_B)
    cv = cache_v.reshape(n_pool, PAGE_SIZE, KV_B)
    seq = lambda w: pl.BlockSpec((1, ts, w), lambda i, pt: (i, 0, 0))
    o = pl.pallas_call(
        functools.partial(_moba_sample_body, n_pages),
        grid_spec=pltpu.PrefetchScalarGridSpec(
            num_scalar_prefetch=1,
            grid=(db,),
            in_specs=[seq(Q_B), seq(KV_B), seq(KV_B)] + _page_specs(n_pages, KV_B) * 2,
            out_specs=seq(OUT_B),
        ),
        out_shape=jax.ShapeDtypeStruct((db, ts, OUT_B), F32),
        compiler_params=_params(("arbitrary",)),
        name="moba_sample",
    )(page_table.reshape(-1), q, kn, vn, *([ck] * n_pages), *([cv] * n_pages))
    return o.reshape(db * ts, OUT_B)


def _sortable(score):
    bits = pltpu.bitcast(score + 0.0, I32)
    return jnp.where(bits >= 0, bits, bits ^ 0x7FFFFFFF)


def _kth_largest_key(count_ge, rows, k):
    def step(b, t):
        cand = jnp.where(b == 0, jnp.zeros_like(t), t | jnp.left_shift(jnp.int32(1), 31 - b))
        return jnp.where(count_ge(cand) >= k, cand, t)

    return lax.fori_loop(0, 32, step, jnp.full((rows, 1), INT_MIN, I32))


def _index_scores(qi8, wv, ki16, rows):
    dots = lax.dot_general(qi8, ki16, _NT, preferred_element_type=F32)
    sc = jnp.maximum(dots[0:rows], 0.0) * wv[:, 0:1]
    for h in range(1, N_IDX_HEADS):
        sc = sc + jnp.maximum(dots[h * rows:(h + 1) * rows], 0.0) * wv[:, h:h + 1]
    return sc


def _dsa_prompt_body(q_ref, k_ref, v_ref, qi_ref, w_ref, ki_ref, o_ref, key_ref, bias_ref, j_ref):
    i = pl.program_id(1)
    tq = Q_TILE
    ck = DSA_CHUNK
    rep = HQ_C // HKV_C
    t_all = key_ref.shape[1]
    nch = ((i + 1) * tq + ck - 1) // ck
    qpos = i * tq + lax.broadcasted_iota(I32, (tq, 1), 0)
    qi = qi_ref[0].astype(BF16)
    qi8 = jnp.concatenate([qi[:, h * IDX_DIM:(h + 1) * IDX_DIM] for h in range(N_IDX_HEADS)], axis=0)
    wv = w_ref[0][:, IDX_DIM:IDX_DIM + N_IDX_HEADS] * (IDX_DIM ** -0.5 * N_IDX_HEADS ** -0.5)
    lane_c = lax.broadcasted_iota(I32, (tq, ck), 1)

    def scores(c, _):
        off = pl.multiple_of(c * ck, ck)
        sc = _index_scores(qi8, wv, ki_ref[0, pl.ds(off, ck), 0:IDX_DIM], tq)
        key_ref[:, pl.ds(off, ck)] = jnp.where(off + lane_c <= qpos, _sortable(sc), INT_MIN)
        return 0

    lax.fori_loop(0, nch, scores, 0)

    def fold(x):
        out = x[:, 0:LANES]
        for a in range(1, ck // LANES):
            out = out + x[:, a * LANES:(a + 1) * LANES]
        return out

    def count(pred):
        def body(c, acc):
            off = pl.multiple_of(c * ck, ck)
            return acc + fold(pred(key_ref[:, pl.ds(off, ck)], off + lane_c).astype(I32))
        return jnp.sum(lax.fori_loop(0, nch, body, jnp.zeros((tq, LANES), I32)), axis=1, keepdims=True)

    thr = _kth_largest_key(lambda cand: count(lambda kx, cx: kx >= cand), tq, DSA_TOPK)
    real = thr > INT_MIN
    need = DSA_TOPK - count(lambda kx, cx: kx > thr)
    n_tie = count(lambda kx, cx: (kx == thr) & real)
    j_ref[...] = jnp.full((tq, 1), t_all, I32)

    @pl.when(jnp.max(jnp.where(n_tie > need, 1, 0)) > 0)
    def _():
        def step(b, jc):
            cand = jc + jnp.left_shift(jnp.int32(1), 13 - b)
            below = count(lambda kx, cx: (kx == thr) & real & (cx < cand))
            return jnp.where(below < need, cand, jc)
        j_ref[...] = lax.fori_loop(0, 14, step, jnp.zeros((tq, 1), I32))

    jmax = j_ref[...]

    def write_bias(c, _):
        off = pl.multiple_of(c * ck, ck)
        kx = key_ref[:, pl.ds(off, ck)]
        keep = (kx > thr) | ((kx == thr) & real & (off + lane_c <= jmax))
        bias_ref[pl.ds(off, ck), :] = jnp.where(keep, 0.0, NEG_BIG).T
        return 0

    lax.fori_loop(0, nch, write_bias, 0)

    assert ck == 2 * FLASH_BLOCK
    nr = rep * tq
    qt = q_ref[0].T
    qas = []
    for kvh in range(HKV_C):
        q4t = jnp.concatenate([qt[(kvh * rep + r) * HEAD_DIM:(kvh * rep + r + 1) * HEAD_DIM] for r in range(rep)],
                              axis=1)
        qas.append(jnp.concatenate([q4t * LOG2E, jnp.zeros((HEAD_DIM, nr), F32)], axis=0).astype(BF16))

    init = tuple((jnp.full((1, nr), NEG_BIG, F32), jnp.zeros((VT_ROWS, nr), F32)) for _ in range(HKV_C))
    res = lax.fori_loop(0, nch, lambda c, carry: _flash_pair(k_ref, v_ref, qas, carry, c, bias_ref), init)
    heads = []
    for kvh in range(HKV_C):
        acc = res[kvh][1]
        ot = acc[0:HEAD_DIM] / acc[HEAD_DIM:HEAD_DIM + 1]
        heads += [ot[:, r * tq:(r + 1) * tq] for r in range(rep)]
    o_ref[0] = jnp.concatenate(heads, axis=0).T


def _dsa_prompt(q, kaug16, vt16, qi, kiwi, kiwi16):
    b, t, _ = q.shape
    o = pl.pallas_call(
        _dsa_prompt_body,
        grid=(b, t // Q_TILE),
        in_specs=[pl.BlockSpec((1, Q_TILE, Q_C), lambda bb, i: (bb, i, 0)),
                  pl.BlockSpec((1, t, HKV_C * LANES), lambda bb, i: (bb, 0, 0)),
                  pl.BlockSpec((1, HKV_C * VT_ROWS, t), lambda bb, i: (bb, 0, 0)),
                  pl.BlockSpec((1, Q_TILE, N_IDX_HEADS * IDX_DIM), lambda bb, i: (bb, i, 0)),
                  pl.BlockSpec((1, Q_TILE, LANES), lambda bb, i: (bb, i, 0)),
                  pl.BlockSpec((1, t, LANES), lambda bb, i: (bb, 0, 0))],
        out_specs=pl.BlockSpec((1, Q_TILE, OUT_C), lambda bb, i: (bb, i, 0)),
        out_shape=jax.ShapeDtypeStruct((b, t, OUT_C), F32),
        scratch_shapes=[pltpu.VMEM((Q_TILE, t), I32), pltpu.VMEM((t, Q_TILE), F32), pltpu.VMEM((Q_TILE, 1), I32)],
        compiler_params=_params(("parallel", "arbitrary")),
        name="dsa_prompt",
    )(q, kaug16, vt16, qi, kiwi, kiwi16)
    return o.reshape(b * t, OUT_C)


def _dsa_sample_body(n_pages, pt_ref, q_ref, kn_ref, vn_ref, qi_ref, kiwi_ref, *refs):
    kp = refs[:n_pages]
    vp = refs[n_pages:2 * n_pages]
    ip = refs[2 * n_pages:3 * n_pages]
    o_ref = refs[3 * n_pages]
    rep = HQ_C // HKV_C
    ts = q_ref.shape[1]
    nrows = HQ_C * ts
    n_cache = n_pages * PAGE_SIZE
    kiwi = kiwi_ref[0]
    qi = qi_ref[0].astype(BF16)
    qi8 = jnp.concatenate([qi[:, h * IDX_DIM:(h + 1) * IDX_DIM] for h in range(N_IDX_HEADS)], axis=0)
    wv = kiwi[:, IDX_DIM:IDX_DIM + N_IDX_HEADS] * (IDX_DIM ** -0.5 * N_IDX_HEADS ** -0.5)
    ki_all = jnp.concatenate([ip[p][0].astype(BF16) for p in range(n_pages)]
                             + [_pad_rows(kiwi[:, 0:IDX_DIM], LANES).astype(BF16)], axis=0)
    sc = _index_scores(qi8, wv, ki_all, ts)
    col = lax.broadcasted_iota(I32, sc.shape, 1)
    trow = lax.broadcasted_iota(I32, sc.shape, 0)
    adm = (col < n_cache) | (col - n_cache <= trow)
    keys = jnp.where(adm, _sortable(sc), INT_MIN)

    def count(mask):
        return jnp.sum(mask.astype(I32), axis=1, keepdims=True)

    n_sel = min(DSA_TOPK, (n_cache + ts) // 4)
    thr = _kth_largest_key(lambda cand: count(keys >= cand), ts, n_sel)
    real = thr > INT_MIN
    need = n_sel - count(keys > thr)
    tie = (keys == thr) & real

    def step(b, jc):
        cand = jc + jnp.left_shift(jnp.int32(1), 13 - b)
        return jnp.where(count(tie & (col < cand)) < need, cand, jc)

    jmax = lax.fori_loop(0, 14, step, jnp.zeros((ts, 1), I32))
    keep = (keys > thr) | (tie & (col <= jmax))
    bias = jnp.where(keep, 0.0, NEG_BIG)
    bias = jnp.concatenate([bias] * HQ_C, axis=0)

    qf16 = _block_diag_queries(q_ref[0], HKV_C, rep).astype(BF16)
    kn16 = _pad_rows(kn_ref[0], LANES).astype(BF16)
    vn16 = _pad_rows(vn_ref[0], LANES).astype(BF16)
    s = jnp.concatenate([lax.dot_general(qf16, kp[p][0].astype(BF16), _NT, preferred_element_type=F32)
                         for p in range(n_pages)]
                        + [lax.dot_general(qf16, kn16, _NT, preferred_element_type=F32)], axis=1) + bias
    m = jnp.max(s, axis=1, keepdims=True)
    p_all = jnp.exp(s - m)
    l = jnp.sum(p_all, axis=1, keepdims=True)
    p16 = p_all.astype(BF16)
    acc = jnp.dot(p16[:, n_cache:n_cache + LANES], vn16, preferred_element_type=F32)
    for p in range(n_pages):
        acc = acc + jnp.dot(p16[:, p * PAGE_SIZE:(p + 1) * PAGE_SIZE], vp[p][0].astype(BF16),
                            preferred_element_type=F32)
    o_ref[0] = _unstack_block_diag(acc / l, HKV_C, rep, ts)


def _dsa_sample(q, kn, vn, qi, kiwi, cache_k, cache_v, cache_ki, page_table):
    db, ts, _ = q.shape
    n_pages = page_table.shape[1]
    n_pool = cache_k.shape[0]
    ck = cache_k.reshape(n_pool, PAGE_SIZE, KV_C)
    cv = cache_v.reshape(n_pool, PAGE_SIZE, KV_C)
    seq = lambda w: pl.BlockSpec((1, ts, w), lambda i, pt: (i, 0, 0))
    o = pl.pallas_call(
        functools.partial(_dsa_sample_body, n_pages),
        grid_spec=pltpu.PrefetchScalarGridSpec(
            num_scalar_prefetch=1,
            grid=(db,),
            in_specs=[seq(Q_C), seq(KV_C), seq(KV_C), seq(N_IDX_HEADS * IDX_DIM), seq(LANES)]
            + _page_specs(n_pages, KV_C) * 2 + _page_specs(n_pages, IDX_DIM),
            out_specs=seq(OUT_C),
        ),
        out_shape=jax.ShapeDtypeStruct((db, ts, OUT_C), F32),
        compiler_params=_params(("arbitrary",)),
        name="dsa_sample",
    )(page_table.reshape(-1), q, kn, vn, qi, kiwi,
      *([ck] * n_pages), *([cv] * n_pages), *([cache_ki] * n_pages))
    return o.reshape(db * ts, OUT_C)


def _rope_tables(pos):
    half = HEAD_DIM // 2
    inv_freq = ROPE_THETA ** (-jnp.arange(half, dtype=F32) / half)
    ang = pos.astype(F32)[:, None] * inv_freq[None, :]
    c, s = jnp.cos(ang), jnp.sin(ang)
    return jnp.tile(c, (1, LANES // half)), jnp.concatenate([-s, s] * (LANES // HEAD_DIM), axis=1)


def _gain128(g):
    return jnp.tile(g.astype(F32), LANES // HEAD_DIM).reshape(1, LANES)


def _layer_a(xp, xs, shp, shs, states, params, tabs_p, tabs_s):
    norm_g, w_in, q_gain, k_gain, w_out = params
    b, t = shp
    db, ts = shs
    w16 = w_in.astype(BF16)
    args = (norm_g.reshape(1, -1), w16)
    gains = (_gain128(q_gain), _gain128(k_gain))
    qp, kp, vp, kp16, vp16, sgp = _project("a", xp, *args, *tabs_p, *gains)
    qs, ks, vs, _, _, sgs = _project("a", xs, *args, *tabs_s, *gains)
    hw = HKV_A * HEAD_DIM
    grp_p, grp_s, new_state = [], [], []
    for g, (win, dil) in enumerate(DIL_GROUPS):
        grp_p.append(_dil_prompt(qp.reshape(b, t, Q_A), kp.reshape(b, t, KV_A), vp.reshape(b, t, KV_A), g, dil))
        slabs, ns = _dil_sample(qs.reshape(db, ts, Q_A), ks.reshape(db, ts, KV_A), vs.reshape(db, ts, KV_A),
                                states[g], g, states[g].shape[1], dil)
        grp_s.append(slabs)
        wk = min(win, t)
        kg = kp.reshape(b, t, KV_A)[:, t - wk:, g * hw:(g + 1) * hw].reshape(b, wk, 1, HKV_A, HEAD_DIM)
        vg = vp.reshape(b, t, KV_A)[:, t - wk:, g * hw:(g + 1) * hw].reshape(b, wk, 1, HKV_A, HEAD_DIM)
        new_state.append(jnp.concatenate([kg, vg], axis=2))
        new_state.append(ns)
    w_out16 = w_out.astype(BF16)
    return _out_project(grp_p, sgp, xp, w_out16), _out_project(grp_s, sgs, xs, w_out16), new_state


def _layer_b(xp, xs, shp, shs, states, params, tabs_p, tabs_s, page_table):
    norm_g, w_in, q_gain, k_gain, w_out = params
    cache_k, cache_v = states
    b, t = shp
    db, ts = shs
    w16 = w_in.astype(BF16)
    args = (norm_g.reshape(1, -1), w16)
    gains = (_gain128(q_gain), _gain128(k_gain))
    qp, kp, vp, _, _, sgp, kmean, kaug16, vt16 = _project("b", xp, *args, *tabs_p, *gains, aug=True)
    qs, ks, vs, _, _, sgs = _project("b", xs, *args, *tabs_s, *gains)
    o_p = _moba_prompt(qp.reshape(b, t, Q_B), kaug16.reshape(b, t, HKV_B * LANES), vt16,
                       kmean.reshape(b, t // MOBA_BLOCK, KV_B))
    o_s = _moba_sample(qs.reshape(db, ts, Q_B), ks.reshape(db, ts, KV_B), vs.reshape(db, ts, KV_B),
                       cache_k, cache_v, page_table)
    w_out16 = w_out.astype(BF16)
    new_state = [kp.reshape(b, t, HKV_B, HEAD_DIM), ks.reshape(db, ts, HKV_B, HEAD_DIM),
                 vp.reshape(b, t, HKV_B, HEAD_DIM), vs.reshape(db, ts, HKV_B, HEAD_DIM)]
    return _out_project([[o_p]], sgp, xp, w_out16), _out_project([[o_s]], sgs, xs, w_out16), new_state


def _layer_c(xp, xs, shp, shs, states, params, tabs_p, tabs_s, page_table):
    norm_g, w_in, q_gain, k_gain, w_out = params
    cache_k, cache_v, cache_ki = states
    b, t = shp
    db, ts = shs
    w16 = jnp.pad(w_in, ((0, 0), (0, IN_C_PAD - IN_C))).astype(BF16)
    args = (norm_g.reshape(1, -1), w16)
    gains = (_gain128(q_gain), _gain128(k_gain))
    qp, kp, vp, _, _, sgp, qip, kiwip, kiwip16, kaug16, vt16 = _project("c", xp, *args, *tabs_p, *gains, aug=True)
    qs, ks, vs, _, _, sgs, qis, kiwis, _ = _project("c", xs, *args, *tabs_s, *gains)
    o_p = _dsa_prompt(qp.reshape(b, t, Q_C), kaug16.reshape(b, t, HKV_C * LANES), vt16,
                      qip.reshape(b, t, -1), kiwip.reshape(b, t, LANES), kiwip16.reshape(b, t, LANES))
    o_s = _dsa_sample(qs.reshape(db, ts, Q_C), ks.reshape(db, ts, KV_C), vs.reshape(db, ts, KV_C),
                      qis.reshape(db, ts, -1), kiwis.reshape(db, ts, LANES), cache_k, cache_v, cache_ki, page_table)
    w_out16 = w_out.astype(BF16)
    new_state = [kp.reshape(b, t, HKV_C, HEAD_DIM), ks.reshape(db, ts, HKV_C, HEAD_DIM),
                 vp.reshape(b, t, HKV_C, HEAD_DIM), vs.reshape(db, ts, HKV_C, HEAD_DIM),
                 kiwip.reshape(b, t, LANES)[:, :, :IDX_DIM], kiwis.reshape(db, ts, LANES)[:, :, :IDX_DIM]]
    return _out_project([[o_p]], sgp, xp, w_out16), _out_project([[o_s]], sgs, xs, w_out16), new_state


def kernel(x_prompt, x_sample, state_l0_kv_w128, state_l0_kv_w512, state_l0_kv_w2048, cache_l1_k, cache_l1_v, cache_l2_k, cache_l2_v, cache_l2_kidx, state_l3_kv_w128, state_l3_kv_w512, state_l3_kv_w2048, page_table, l0_norm, l0_w_in, l0_q_norm, l0_k_norm, l0_w_out, l1_norm, l1_w_in, l1_q_norm, l1_k_norm, l1_w_out, l2_norm, l2_w_in, l2_q_norm, l2_k_norm, l2_w_out, l3_norm, l3_w_in, l3_q_norm, l3_k_norm, l3_w_out):
    b, t, _ = x_prompt.shape
    db, ts, _ = x_sample.shape
    assert t % (16 * Q_TILE) == 0 and t % DSA_CHUNK == 0 and (db * ts) % PROJ_ROWS == 0 and PROJ_ROWS % ts == 0
    tabs_p = _rope_tables(jnp.arange(t))
    tabs_s = _rope_tables(PAST_LEN + (jnp.arange(PROJ_ROWS) % ts))
    xp = x_prompt.reshape(b * t, D_MODEL)
    xs = x_sample.reshape(db * ts, D_MODEL)
    states = ((state_l0_kv_w128, state_l0_kv_w512, state_l0_kv_w2048), (cache_l1_k, cache_l1_v),
              (cache_l2_k, cache_l2_v, cache_l2_kidx), (state_l3_kv_w128, state_l3_kv_w512, state_l3_kv_w2048))
    params = ((l0_norm, l0_w_in, l0_q_norm, l0_k_norm, l0_w_out), (l1_norm, l1_w_in, l1_q_norm, l1_k_norm, l1_w_out),
              (l2_norm, l2_w_in, l2_q_norm, l2_k_norm, l2_w_out), (l3_norm, l3_w_in, l3_q_norm, l3_k_norm, l3_w_out))
    new_state = []
    for i in range(4):
        common = (xp, xs, (b, t), (db, ts), states[i], params[i], tabs_p, tabs_s)
        if i % 3 == 0:
            xp, xs, st = _layer_a(*common)
        elif i % 3 == 1:
            xp, xs, st = _layer_b(*common, page_table)
        else:
            xp, xs, st = _layer_c(*common, page_table)
        new_state.extend(st)
    return (xp.reshape(b, t, D_MODEL), xs.reshape(db, ts, D_MODEL), *new_state)
```

```python
import functools

import jax
import jax.numpy as jnp
from jax import lax
from jax.experimental import pallas as pl
from jax.experimental.pallas import tpu as pltpu

F32 = jnp.float32
BF16 = jnp.bfloat16
I32 = jnp.int32

D_MODEL = 1024
PAST_LEN = 2048
PAGE_SIZE = 128
HEAD_DIM = 64
ROPE_THETA = 10000.0
NORM_EPS = 1e-6

DIL_GROUPS = ((128, 1), (512, 4), (2048, 16))
HQ_A, HKV_A = 8, 2
Q_A, KV_A, OUT_A = 1536, 384, 512

HQ_B, HKV_B = 16, 4
MOBA_BLOCK, MOBA_TOPK = 256, 3
Q_B, KV_B, OUT_B = 1024, 256, 1024

HQ_C, HKV_C = 16, 4
N_IDX_HEADS, IDX_DIM, DSA_TOPK = 8, 64, 256
Q_C, KV_C, OUT_C = 1024, 256, 1024
IN_C = Q_C + 2 * KV_C + OUT_C + N_IDX_HEADS * IDX_DIM + IDX_DIM + N_IDX_HEADS
IN_C_PAD = 3200

LANES = 128
PROJ_ROWS = 256
Q_TILE = 128
DSA_CHUNK = 512
VMEM_LIMIT = 56 * 1024 * 1024
NEG_BIG = -1e30
LOG2E = 1.4426950408889634
VT_ROWS = 80
INT_MIN = -2147483648

_NT = (((1,), (1,)), ((), ()))


def _params(sem):
    return pltpu.CompilerParams(dimension_semantics=sem, vmem_limit_bytes=VMEM_LIMIT)


def _group_sum_matrix():
    r = lax.broadcasted_iota(I32, (LANES, LANES), 0) // HEAD_DIM
    c = lax.broadcasted_iota(I32, (LANES, LANES), 1) // HEAD_DIM
    return (r == c).astype(BF16)


def _rope(y, cos, sin):
    lane = lax.broadcasted_iota(I32, y.shape, 1)
    first_half = (lane & (HEAD_DIM // 2)) == 0
    partner = jnp.where(first_half, pltpu.roll(y, LANES - HEAD_DIM // 2, 1), pltpu.roll(y, HEAD_DIM // 2, 1))
    return y * cos + partner * sin


def _head_norm(x, gain, gmat):
    ss = x * x
    hi = ss.astype(BF16)
    lo = (ss - hi.astype(F32)).astype(BF16)
    gs = jnp.dot(hi, gmat, preferred_element_type=F32) + jnp.dot(lo, gmat, preferred_element_type=F32)
    return x * lax.rsqrt(gs * (1.0 / HEAD_DIM) + NORM_EPS) * gain


def _proj_body(kind, aug_tiles, x_ref, g_ref, w_ref, cos_ref, sin_ref, qg_ref, kg_ref, *outs):
    x = x_ref[...]
    ms = jnp.mean(x * x, axis=-1, keepdims=True)
    h = (x * lax.rsqrt(ms + NORM_EPS) * g_ref[...]).astype(BF16)
    cos = cos_ref[...]
    sin = sin_ref[...]
    qg = qg_ref[...]
    kg = kg_ref[...]
    gmat = _group_sum_matrix()
    qw, kvw, gw = {"a": (Q_A, KV_A, OUT_A), "b": (Q_B, KV_B, OUT_B), "c": (Q_C, KV_C, OUT_C)}[kind]
    q_ref, k_ref, v_ref, k16_ref, v16_ref, sg_ref = outs[:6]
    rest = outs[6:]
    moba = aug_tiles and kind == "b"
    if aug_tiles:
        kaug_ref, vt_ref = rest[-2:]

    def seg(start, width):
        return jnp.dot(h, w_ref[:, start:start + width], preferred_element_type=F32)

    zq = seg(0, qw)
    for c in range(qw // LANES):
        y = _rope(_head_norm(zq[:, c * LANES:(c + 1) * LANES], qg, gmat), cos, sin)
        q_ref[:, c * LANES:(c + 1) * LANES] = y * (HEAD_DIM ** -0.5)
    zk = seg(qw, kvw)
    ksum = []
    for c in range(kvw // LANES):
        y = _rope(_head_norm(zk[:, c * LANES:(c + 1) * LANES], kg, gmat), cos, sin)
        k_ref[:, c * LANES:(c + 1) * LANES] = y
        k16_ref[:, c * LANES:(c + 1) * LANES] = y.astype(BF16)
        if moba:
            ksum.append(jnp.sum(y, axis=0, keepdims=True) * (1.0 / MOBA_BLOCK))
        if aug_tiles:
            lane = lax.broadcasted_iota(I32, y.shape, 1)
            hot = HEAD_DIM + pl.program_id(0) % aug_tiles if moba else -1
            onehot = (lane == hot).astype(F32)
            kaug_ref[:, 2 * c * LANES:(2 * c + 1) * LANES] = jnp.where(lane < HEAD_DIM, y, onehot).astype(BF16)
            kaug_ref[:, (2 * c + 1) * LANES:(2 * c + 2) * LANES] = jnp.where(
                lane < HEAD_DIM, pltpu.roll(y, HEAD_DIM, 1), onehot).astype(BF16)
    zv = seg(qw + kvw, kvw)
    v_ref[...] = zv
    v16_ref[...] = zv.astype(BF16)
    if aug_tiles:
        zvt = zv.T
        tail = (lax.broadcasted_iota(I32, (VT_ROWS - HEAD_DIM, zvt.shape[1]), 0) == 0).astype(BF16)
        for kvh in range(kvw // HEAD_DIM):
            vt_ref[0, kvh * VT_ROWS:kvh * VT_ROWS + HEAD_DIM, :] = zvt[kvh * HEAD_DIM:(kvh + 1) * HEAD_DIM].astype(BF16)
            vt_ref[0, kvh * VT_ROWS + HEAD_DIM:(kvh + 1) * VT_ROWS, :] = tail
    zg = seg(qw + 2 * kvw, gw)
    sg_ref[...] = zg / (1.0 + jnp.exp(-zg))
    nrest = 0
    if kind == "c":
        qi_ref, kiwi_ref, kiwi16_ref = rest[:3]
        nrest = 3
        base = qw + 2 * kvw + gw
        zi = seg(base, N_IDX_HEADS * IDX_DIM)
        for c in range(N_IDX_HEADS * IDX_DIM // LANES):
            qi_ref[:, c * LANES:(c + 1) * LANES] = _rope(zi[:, c * LANES:(c + 1) * LANES], cos, sin)
        zz = seg(base + N_IDX_HEADS * IDX_DIM, LANES)
        lane = lax.broadcasted_iota(I32, zz.shape, 1)
        kiwi = jnp.where(lane < IDX_DIM, _rope(zz, cos, sin), zz)
        kiwi_ref[...] = kiwi
        kiwi16_ref[...] = kiwi.astype(BF16)
    if moba:
        km_ref = rest[0]
        for c in range(kvw // LANES):
            km_ref[0, :, c * LANES:(c + 1) * LANES] = ksum[c]


def _project(kind, x, gain, w16, cos, sin, qg, kg, aug=False):
    n = x.shape[0]
    tm = PROJ_ROWS
    nt = cos.shape[0] // tm
    qw, kvw, gw = {"a": (Q_A, KV_A, OUT_A), "b": (Q_B, KV_B, OUT_B), "c": (Q_C, KV_C, OUT_C)}[kind]
    wp = w16.shape[1]

    def rows(width):
        return pl.BlockSpec((tm, width), lambda i: (i, 0))

    def const(shape):
        return pl.BlockSpec(shape, lambda i: (0,) * len(shape))

    out_shape = [jax.ShapeDtypeStruct((n, qw), F32), jax.ShapeDtypeStruct((n, kvw), F32),
                 jax.ShapeDtypeStruct((n, kvw), F32), jax.ShapeDtypeStruct((n, kvw), BF16),
                 jax.ShapeDtypeStruct((n, kvw), BF16), jax.ShapeDtypeStruct((n, gw), F32)]
    out_specs = [rows(qw), rows(kvw), rows(kvw), rows(kvw), rows(kvw), rows(gw)]
    if kind == "c":
        out_shape += [jax.ShapeDtypeStruct((n, N_IDX_HEADS * IDX_DIM), F32),
                      jax.ShapeDtypeStruct((n, LANES), F32), jax.ShapeDtypeStruct((n, LANES), BF16)]
        out_specs += [rows(N_IDX_HEADS * IDX_DIM), rows(LANES), rows(LANES)]
    if aug:
        n_kv = kvw // HEAD_DIM
        if kind == "b":
            out_shape.append(jax.ShapeDtypeStruct((n // tm, 1, kvw), F32))
            out_specs.append(pl.BlockSpec((1, 1, kvw), lambda i: (i, 0, 0)))
        out_shape += [jax.ShapeDtypeStruct((n, n_kv * LANES), BF16),
                      jax.ShapeDtypeStruct((n // (nt * tm), n_kv * VT_ROWS, nt * tm), BF16)]
        out_specs += [rows(n_kv * LANES), pl.BlockSpec((1, n_kv * VT_ROWS, tm), lambda i: (i // nt, 0, i % nt))]
    return pl.pallas_call(
        functools.partial(_proj_body, kind, nt if aug else 0),
        grid=(n // tm,),
        in_specs=[rows(D_MODEL), const((1, D_MODEL)), const((D_MODEL, wp)),
                  pl.BlockSpec((tm, LANES), lambda i: (i % nt, 0)),
                  pl.BlockSpec((tm, LANES), lambda i: (i % nt, 0)),
                  const((1, LANES)), const((1, LANES))],
        out_specs=out_specs,
        out_shape=out_shape,
        compiler_params=_params(("parallel",)),
        name=f"proj_{kind}",
    )(x, gain, w16, cos, sin, qg, kg)


LSE_REP = LANES // HQ_A
O_SLABS = OUT_A // LANES


def _outproj_body(n_groups, *refs):
    per = O_SLABS + 1
    sg_ref, x_ref, w_ref, y_ref = refs[-4:]
    if n_groups > 1:
        lses = [refs[g * per + O_SLABS][...] for g in range(n_groups)]
        m = functools.reduce(jnp.maximum, lses)
        es = [jnp.exp(l - m) for l in lses]
        tot = functools.reduce(lambda a, b: a + b, es)
        lane = lax.broadcasted_iota(I32, m.shape, 1)
        slabs = []
        for c in range(O_SLABS):
            acc = None
            for g in range(n_groups):
                wg = es[g] / tot
                lo = wg[:, 2 * c * LSE_REP:2 * c * LSE_REP + 1]
                hi = wg[:, (2 * c + 1) * LSE_REP:(2 * c + 1) * LSE_REP + 1]
                term = jnp.where(lane < HEAD_DIM, lo, hi) * refs[g * per + c][...]
                acc = term if acc is None else acc + term
            slabs.append(acc)
        o = jnp.concatenate(slabs, axis=1)
    else:
        o = refs[0][...]
    g16 = (o * sg_ref[...]).astype(BF16)
    y_ref[...] = x_ref[...] + jnp.dot(g16, w_ref[...], preferred_element_type=F32)


def _out_project(groups, sg, x, w16):
    n = x.shape[0]
    f = sg.shape[1]
    tm = PROJ_ROWS
    ins = [a for grp in groups for a in grp]
    in_specs = [pl.BlockSpec((tm, a.shape[1]), lambda i: (i, 0)) for a in ins]
    in_specs += [pl.BlockSpec((tm, f), lambda i: (i, 0)), pl.BlockSpec((tm, D_MODEL), lambda i: (i, 0)),
                 pl.BlockSpec((f, D_MODEL), lambda i: (0, 0))]
    return pl.pallas_call(
        functools.partial(_outproj_body, len(groups)),
        grid=(n // tm,),
        in_specs=in_specs,
        out_specs=pl.BlockSpec((tm, D_MODEL), lambda i: (i, 0)),
        out_shape=jax.ShapeDtypeStruct((n, D_MODEL), F32),
        compiler_params=_params(("parallel",)),
        name="out_proj_merge" if len(groups) > 1 else "out_proj",
    )(*ins, sg, x, w16)


def _stack_heads(q, kvh, rep):
    return jnp.concatenate([q[:, (kvh * rep + r) * HEAD_DIM:(kvh * rep + r + 1) * HEAD_DIM] for r in range(rep)],
                           axis=0)


FLASH_BLOCK = 256


def _flash_pair(k_ref, v_ref, qas, carry, n, bias_ref):
    carry = list(carry)
    units = [(j, kvh) for j in range(2) for kvh in range(len(qas))]
    offs = [pl.multiple_of((2 * n + j) * FLASH_BLOCK, FLASH_BLOCK) for j in range(2)]
    rep = qas[0].shape[1] // Q_TILE
    scores = []
    for j, kvh in units:
        st = jnp.dot(k_ref[0, pl.ds(offs[j], FLASH_BLOCK), kvh * LANES:(kvh + 1) * LANES], qas[kvh],
                     preferred_element_type=F32)
        if bias_ref is not None:
            st = st + jnp.concatenate([bias_ref[pl.ds(offs[j], FLASH_BLOCK), :]] * rep, axis=1)
        scores.append(st)
    for (j, kvh), st in zip(units, scores):
        m, acc = carry[kvh]
        m_new = jnp.maximum(m, jnp.max(st, axis=0, keepdims=True))
        p = jnp.exp2(st - m_new)
        pv = jnp.dot(v_ref[0, kvh * VT_ROWS:(kvh + 1) * VT_ROWS, pl.ds(offs[j], FLASH_BLOCK)], p.astype(BF16),
                     preferred_element_type=F32)
        carry[kvh] = (m_new, jnp.exp2(m - m_new) * acc + pv)
    return tuple(carry)


def _lse_slab(lses, rows):
    return jnp.concatenate([jnp.broadcast_to(l, (rows, LSE_REP)) for l in lses], axis=1)


def _dil_prompt_body(dil, *refs):
    q_refs = refs[:O_SLABS]
    kp_ref, kc_ref, vp_ref, vc_ref = refs[O_SLABS:O_SLABS + 4]
    o_refs = refs[O_SLABS + 4:2 * O_SLABS + 4]
    lse_ref = refs[2 * O_SLABS + 4]
    j = pl.program_id(1)
    tq = Q_TILE
    rep = HQ_A // HKV_A
    row = lax.broadcasted_iota(I32, (tq, 2 * tq), 0)
    col = lax.broadcasted_iota(I32, (tq, 2 * tq), 1)
    dist = row + tq - col
    valid = (dist >= 0) & (dist <= tq) & ((col >= tq) | (j > 0))
    valid = jnp.concatenate([valid] * rep, axis=0)

    def residue(r, _):
        rows = pl.ds(r, tq, stride=dil) if dil > 1 else pl.ds(0, tq)
        q = jnp.concatenate([qr[0, rows, :] for qr in q_refs], axis=1).astype(BF16)
        kk = jnp.concatenate([kp_ref[0, rows, :], kc_ref[0, rows, :]], axis=0).astype(BF16)
        vv = jnp.concatenate([vp_ref[0, rows, :], vc_ref[0, rows, :]], axis=0).astype(BF16)
        outs, lses = [], []
        for kvh in range(HKV_A):
            kh = kk[:, kvh * HEAD_DIM:(kvh + 1) * HEAD_DIM]
            vh = vv[:, kvh * HEAD_DIM:(kvh + 1) * HEAD_DIM]
            qh = _stack_heads(q, kvh, rep)
            s = lax.dot_general(qh, kh, _NT, preferred_element_type=F32)
            s = jnp.where(valid, s, -jnp.inf)
            m = jnp.max(s, axis=1, keepdims=True)
            p = jnp.exp(s - m)
            l = jnp.sum(p, axis=1, keepdims=True)
            o = jnp.dot(p.astype(BF16), vh, preferred_element_type=F32) / l
            lse = m + jnp.log(l)
            for h in range(rep):
                outs.append(o[h * tq:(h + 1) * tq])
                lses.append(lse[h * tq:(h + 1) * tq])
        for c in range(O_SLABS):
            o_refs[c][0, rows, :] = jnp.concatenate(outs[2 * c:2 * c + 2], axis=1)
        lse_ref[0, rows, :] = _lse_slab(lses, tq)
        return 0

    if dil == 1:
        residue(0, 0)
    else:
        lax.fori_loop(0, dil, residue, 0)


def _dil_prompt(q, k, v, g, dil):
    b, t, _ = q.shape
    blk = Q_TILE * dil
    hw = HKV_A * HEAD_DIM
    cur = lambda bb, j: (bb, j, g)
    prev = lambda bb, j: (bb, jnp.maximum(j - 1, 0), g)
    slab = pl.BlockSpec((1, blk, LANES), lambda bb, j: (bb, j, 0))
    outs = pl.pallas_call(
        functools.partial(_dil_prompt_body, dil),
        grid=(b, t // blk),
        in_specs=[pl.BlockSpec((1, blk, LANES), functools.partial(lambda c, bb, j: (bb, j, g * O_SLABS + c), c))
                  for c in range(O_SLABS)]
        + [pl.BlockSpec((1, blk, hw), prev), pl.BlockSpec((1, blk, hw), cur),
           pl.BlockSpec((1, blk, hw), prev), pl.BlockSpec((1, blk, hw), cur)],
        out_specs=[slab] * (O_SLABS + 1),
        out_shape=[jax.ShapeDtypeStruct((b, t, LANES), F32)] * (O_SLABS + 1),
        compiler_params=_params(("parallel", "arbitrary")),
        name=f"dil_prompt_g{g}",
    )(*([q] * O_SLABS), k, k, v, v)
    return [a.reshape(b * t, LANES) for a in outs]


def _dil_sample_body(win, dil, sb, q_ref, kn_ref, vn_ref, st_ref, *out_refs):
    o_refs = out_refs[:O_SLABS]
    lse_ref, ns_ref = out_refs[O_SLABS:]
    rep = HQ_A // HKV_A
    hw = HKV_A * HEAD_DIM
    ts = q_ref.shape[1]
    row = lax.broadcasted_iota(I32, (rep * ts, win), 0)
    col = lax.broadcasted_iota(I32, (rep * ts, win), 1)
    dist = win + (row & (ts - 1)) - col
    valid_c = (dist <= win) & ((dist & (dil - 1)) == 0)
    rown = lax.broadcasted_iota(I32, (rep * ts, LANES), 0)
    coln = lax.broadcasted_iota(I32, (rep * ts, LANES), 1)
    distn = (rown & (ts - 1)) - coln
    valid_n = (distn >= 0) & ((distn & (dil - 1)) == 0)
    lane_w = lax.broadcasted_iota(I32, (2 * hw, LANES), 1)
    for s_i in range(sb):
        st = st_ref[s_i]
        knp = _pad_rows(kn_ref[s_i], LANES)
        vnp = _pad_rows(vn_ref[s_i], LANES)
        shifted = pltpu.roll(st, win - ts, 1)
        newcols = pltpu.roll(jnp.concatenate([knp.T, vnp.T], axis=0), LANES - ts, 1)
        if win > LANES:
            ns_ref[s_i, :, 0:win - LANES] = shifted[:, 0:win - LANES]
        ns_ref[s_i, :, win - LANES:win] = jnp.where(lane_w >= LANES - ts, newcols, shifted[:, win - LANES:win])
        q = q_ref[s_i].astype(BF16)
        kn16 = knp.astype(BF16)
        vn16 = vnp.astype(BF16)
        outs, lses = [], []
        for kvh in range(HKV_A):
            hs = slice(kvh * HEAD_DIM, (kvh + 1) * HEAD_DIM)
            kt = st[kvh * HEAD_DIM:(kvh + 1) * HEAD_DIM].astype(BF16)
            vt = st[hw + kvh * HEAD_DIM:hw + (kvh + 1) * HEAD_DIM].astype(BF16)
            qh = _stack_heads(q, kvh, rep)
            s_c = jnp.where(valid_c, jnp.dot(qh, kt, preferred_element_type=F32), -jnp.inf)
            s_n = jnp.where(valid_n, lax.dot_general(qh, kn16[:, hs], _NT, preferred_element_type=F32), -jnp.inf)
            m = jnp.maximum(jnp.max(s_c, axis=1, keepdims=True), jnp.max(s_n, axis=1, keepdims=True))
            p_c = jnp.exp(s_c - m)
            p_n = jnp.exp(s_n - m)
            l = jnp.sum(p_c, axis=1, keepdims=True) + jnp.sum(p_n, axis=1, keepdims=True)
            o = (lax.dot_general(p_c.astype(BF16), vt, _NT, preferred_element_type=F32)
                 + jnp.dot(p_n.astype(BF16), vn16[:, hs], preferred_element_type=F32)) / l
            lse = m + jnp.log(l)
            for r in range(rep):
                outs.append(o[r * ts:(r + 1) * ts])
                lses.append(lse[r * ts:(r + 1) * ts])
        for c in range(O_SLABS):
            o_refs[c][s_i] = jnp.concatenate(outs[2 * c:2 * c + 2], axis=1)
        lse_ref[s_i] = _lse_slab(lses, ts)


def _dil_sample(q, kn, vn, state, g, win, dil):
    db, ts, _ = q.shape
    hw = HKV_A * HEAD_DIM
    st = state.transpose(0, 2, 3, 4, 1).reshape(db, 2 * hw, win)
    sb = max(1, min(8, 1024 // win))
    slab = pl.BlockSpec((sb, ts, LANES), lambda i: (i, 0, 0))
    *slabs, ns = pl.pallas_call(
        functools.partial(_dil_sample_body, win, dil, sb),
        grid=(db // sb,),
        in_specs=[pl.BlockSpec((sb, ts, OUT_A), lambda i: (i, 0, g)),
                  pl.BlockSpec((sb, ts, hw), lambda i: (i, 0, g)),
                  pl.BlockSpec((sb, ts, hw), lambda i: (i, 0, g)),
                  pl.BlockSpec((sb, 2 * hw, win), lambda i: (i, 0, 0))],
        out_specs=[slab] * (O_SLABS + 1) + [pl.BlockSpec((sb, 2 * hw, win), lambda i: (i, 0, 0))],
        out_shape=[jax.ShapeDtypeStruct((db, ts, LANES), F32)] * (O_SLABS + 1)
        + [jax.ShapeDtypeStruct((db, 2 * hw, win), F32)],
        compiler_params=_params(("parallel",)),
        name=f"dil_sample_g{g}",
    )(q, kn, vn, st)
    ns = ns.reshape(db, 2, HKV_A, HEAD_DIM, win).transpose(0, 4, 1, 2, 3)
    return [a.reshape(db * ts, LANES) for a in slabs], ns


def _top_blocks(bs, n_valid, topk):
    lane = lax.broadcasted_iota(I32, bs.shape, 1)
    nb = bs.shape[1]
    work = jnp.where(lane < n_valid, bs, -jnp.inf)
    chosen = jnp.zeros(bs.shape, jnp.bool_)
    for _ in range(topk):
        m = jnp.max(work, axis=1, keepdims=True)
        idx = jnp.min(jnp.where(work == m, lane, nb), axis=1, keepdims=True)
        pick = lane == idx
        chosen = chosen | pick
        work = jnp.where(pick, -jnp.inf, work)
    return chosen & (lane < n_valid)


def _top_blocks_t(bs, n_valid, topk):
    row = lax.broadcasted_iota(I32, bs.shape, 0)
    rowf = row.astype(F32)
    work = jnp.where(row < n_valid, bs, -jnp.inf)
    chosen = jnp.zeros(bs.shape, jnp.bool_)
    for _ in range(topk):
        m = jnp.max(work, axis=0, keepdims=True)
        idx = jnp.min(jnp.where(work == m, rowf, float(bs.shape[0])), axis=0, keepdims=True)
        pick = rowf == idx
        chosen = chosen | pick
        work = jnp.where(pick, -jnp.inf, work)
    return chosen & (row < n_valid)


def _moba_prompt_body(q_ref, k_ref, v_ref, km_ref, o_ref):
    i = pl.program_id(1)
    tq = Q_TILE
    rep = HQ_B // HKV_B
    nb = km_ref.shape[1]
    nr = rep * tq
    own = (i * tq) // MOBA_BLOCK
    own0 = pl.multiple_of(own * MOBA_BLOCK, MOBA_BLOCK)
    qt = q_ref[0].T
    km = km_ref[0]
    krow = lax.broadcasted_iota(I32, (MOBA_BLOCK, nr), 0)
    qcol = lax.broadcasted_iota(I32, (MOBA_BLOCK, nr), 1)
    causal = own0 + krow <= i * tq + (qcol & (tq - 1))
    brow = lax.broadcasted_iota(I32, (HEAD_DIM, nr), 0)
    qas_past, init = [], []
    for kvh in range(HKV_B):
        hs = slice(kvh * HEAD_DIM, (kvh + 1) * HEAD_DIM)
        ls = slice(kvh * LANES, (kvh + 1) * LANES)
        vs = slice(kvh * VT_ROWS, (kvh + 1) * VT_ROWS)
        q4t = jnp.concatenate([qt[(kvh * rep + r) * HEAD_DIM:(kvh * rep + r + 1) * HEAD_DIM] for r in range(rep)],
                              axis=1)
        bst = jnp.dot(km[:, hs], q4t, precision=lax.Precision.HIGHEST, preferred_element_type=F32)
        if nb < HEAD_DIM:
            bst = jnp.concatenate([bst, jnp.zeros((HEAD_DIM - nb, nr), F32)], axis=0)
        chosen = _top_blocks_t(bst, own, MOBA_TOPK)
        qs = q4t * LOG2E
        qas_past.append(jnp.concatenate([qs, jnp.where(chosen, 0.0, NEG_BIG)], axis=0).astype(BF16))
        qa = jnp.concatenate([qs, jnp.where(brow == own, 0.0, NEG_BIG)], axis=0).astype(BF16)
        st = jnp.dot(k_ref[0, pl.ds(own0, MOBA_BLOCK), ls], qa, preferred_element_type=F32)
        st = jnp.where(causal, st, -jnp.inf)
        m = jnp.max(st, axis=0, keepdims=True)
        p = jnp.exp2(st - m)
        init.append((m, jnp.dot(v_ref[0, vs, pl.ds(own0, MOBA_BLOCK)], p.astype(BF16), preferred_element_type=F32)))

    res = lax.fori_loop(0, (own + 1) // 2,
                        lambda n, carry: _flash_pair(k_ref, v_ref, qas_past, carry, n, None), tuple(init))
    heads = []
    for kvh in range(HKV_B):
        acc = res[kvh][1]
        ot = acc[0:HEAD_DIM] / acc[HEAD_DIM:HEAD_DIM + 1]
        heads += [ot[:, r * tq:(r + 1) * tq] for r in range(rep)]
    o_ref[0] = jnp.concatenate(heads, axis=0).T


def _moba_prompt(q, kaug16, vt16, kmean):
    b, t, _ = q.shape
    nb = t // MOBA_BLOCK
    o = pl.pallas_call(
        _moba_prompt_body,
        grid=(b, t // Q_TILE),
        in_specs=[pl.BlockSpec((1, Q_TILE, Q_B), lambda bb, i: (bb, i, 0)),
                  pl.BlockSpec((1, t, HKV_B * LANES), lambda bb, i: (bb, 0, 0)),
                  pl.BlockSpec((1, HKV_B * VT_ROWS, t), lambda bb, i: (bb, 0, 0)),
                  pl.BlockSpec((1, nb, KV_B), lambda bb, i: (bb, 0, 0))],
        out_specs=pl.BlockSpec((1, Q_TILE, OUT_B), lambda bb, i: (bb, i, 0)),
        out_shape=jax.ShapeDtypeStruct((b, t, OUT_B), F32),
        compiler_params=_params(("parallel", "arbitrary")),
        name="moba_prompt",
    )(q, kaug16, vt16, kmean)
    return o.reshape(b * t, OUT_B)


def _block_diag_queries(q, n_kv, rep):
    ts = q.shape[0]
    rows = []
    for kvh in range(n_kv):
        for r in range(rep):
            hd = (kvh * rep + r) * HEAD_DIM
            parts = []
            if kvh > 0:
                parts.append(jnp.zeros((ts, kvh * HEAD_DIM), q.dtype))
            parts.append(q[:, hd:hd + HEAD_DIM])
            if kvh < n_kv - 1:
                parts.append(jnp.zeros((ts, (n_kv - 1 - kvh) * HEAD_DIM), q.dtype))
            rows.append(jnp.concatenate(parts, axis=1))
    return jnp.concatenate(rows, axis=0)


def _unstack_block_diag(o_all, n_kv, rep, ts):
    parts = []
    for kvh in range(n_kv):
        for r in range(rep):
            r0 = (kvh * rep + r) * ts
            parts.append(o_all[r0:r0 + ts, kvh * HEAD_DIM:(kvh + 1) * HEAD_DIM])
    return jnp.concatenate(parts, axis=1)


def _pad_rows(x, rows):
    return jnp.concatenate([x, jnp.zeros((rows - x.shape[0], x.shape[1]), x.dtype)], axis=0)


def _moba_sample_body(n_pages, pt_ref, q_ref, kn_ref, vn_ref, *refs):
    kp = refs[:n_pages]
    vp = refs[n_pages:2 * n_pages]
    o_ref = refs[2 * n_pages]
    rep = HQ_B // HKV_B
    ts = q_ref.shape[1]
    nrows = HQ_B * ts
    ppb = MOBA_BLOCK // PAGE_SIZE
    n_past = n_pages // ppb
    qf = _block_diag_queries(q_ref[0], HKV_B, rep)
    qf16 = qf.astype(BF16)
    lane_k = lax.broadcasted_iota(I32, (KV_B, LANES), 1)
    kmt = jnp.zeros((KV_B, LANES), F32)
    for n in range(n_past):
        tot = kp[n * ppb][0]
        for j in range(1, ppb):
            tot = tot + kp[n * ppb + j][0]
        kmt = jnp.where(lane_k == n, jnp.sum(tot, axis=1, keepdims=True) * (1.0 / MOBA_BLOCK), kmt)
    bs = jnp.dot(qf, kmt, precision=lax.Precision.HIGHEST, preferred_element_type=F32)
    chosen = _top_blocks(bs, n_past, MOBA_TOPK)
    selb = jnp.where(chosen, 0.0, NEG_BIG)
    bias = jnp.concatenate([jnp.broadcast_to(selb[:, n:n + 1], (nrows, MOBA_BLOCK)) for n in range(n_past)], axis=1)
    s_c = jnp.concatenate([jnp.dot(qf16, kp[p][0].astype(BF16), preferred_element_type=F32)
                           for p in range(n_pages)], axis=1) + bias
    kn16 = _pad_rows(kn_ref[0], LANES).astype(BF16)
    vn16 = _pad_rows(vn_ref[0], LANES).astype(BF16)
    s_n = lax.dot_general(qf16, kn16, _NT, preferred_element_type=F32)
    row = lax.broadcasted_iota(I32, (nrows, LANES), 0)
    col = lax.broadcasted_iota(I32, (nrows, LANES), 1)
    s_n = jnp.where(col <= (row & (ts - 1)), s_n, -jnp.inf)
    m = jnp.maximum(jnp.max(s_c, axis=1, keepdims=True), jnp.max(s_n, axis=1, keepdims=True))
    p_c = jnp.exp(s_c - m)
    p_n = jnp.exp(s_n - m)
    l = jnp.sum(p_c, axis=1, keepdims=True) + jnp.sum(p_n, axis=1, keepdims=True)
    acc = jnp.dot(p_n.astype(BF16), vn16, preferred_element_type=F32)
    p16 = p_c.astype(BF16)
    for p in range(n_pages):
        acc = acc + lax.dot_general(p16[:, p * PAGE_SIZE:(p + 1) * PAGE_SIZE], vp[p][0].astype(BF16), _NT,
                                    preferred_element_type=F32)
    o_ref[0] = _unstack_block_diag(acc / l, HKV_B, rep, ts)


def _page_specs(n_pages, width):
    return [pl.BlockSpec((1, width, PAGE_SIZE), functools.partial(lambda p, i, pt: (pt[i * n_pages + p], 0, 0), p))
            for p in range(n_pages)]


def _pages_minor(cache):
    n_pool = cache.shape[0]
    return jnp.moveaxis(cache, 1, -1).reshape(n_pool, -1, PAGE_SIZE)


def _moba_sample(q, kn, vn, cache_k, cache_v, page_table):
    db, ts, _ = q.shape
    n_pages = page_table.shape[1]
    ck = _pages_minor(cache_k)
    cv = _pages_minor(cache_v)
    seq = lambda w: pl.BlockSpec((1, ts, w), lambda i, pt: (i, 0, 0))
    o = pl.pallas_call(
        functools.partial(_moba_sample_body, n_pages),
        grid_spec=pltpu.PrefetchScalarGridSpec(
            num_scalar_prefetch=1,
            grid=(db,),
            in_specs=[seq(Q_B), seq(KV_B), seq(KV_B)] + _page_specs(n_pages, KV_B) * 2,
            out_specs=seq(OUT_B),
        ),
        out_shape=jax.ShapeDtypeStruct((db, ts, OUT_B), F32),
        compiler_params=_params(("arbitrary",)),
        name="moba_sample",
    )(page_table.reshape(-1), q, kn, vn, *([ck] * n_pages), *([cv] * n_pages))
    return o.reshape(db * ts, OUT_B)


def _sortable(score):
    bits = pltpu.bitcast(score + 0.0, I32)
    return jnp.where(bits >= 0, bits, bits ^ 0x7FFFFFFF)


def _kth_largest_key(count_ge, rows, k):
    def step(b, t):
        cand = jnp.where(b == 0, jnp.zeros_like(t), t | jnp.left_shift(jnp.int32(1), 31 - b))
        return jnp.where(count_ge(cand) >= k, cand, t)

    return lax.fori_loop(0, 32, step, jnp.full((rows, 1), INT_MIN, I32))


def _index_scores(dots, wv, rows):
    sc = jnp.maximum(dots[0:rows], 0.0) * wv[:, 0:1]
    for h in range(1, N_IDX_HEADS):
        sc = sc + jnp.maximum(dots[h * rows:(h + 1) * rows], 0.0) * wv[:, h:h + 1]
    return sc


def _dsa_prompt_body(q_ref, k_ref, v_ref, qi_ref, w_ref, ki_ref, o_ref, key_ref, bias_ref, j_ref):
    i = pl.program_id(1)
    tq = Q_TILE
    ck = DSA_CHUNK
    rep = HQ_C // HKV_C
    t_all = key_ref.shape[1]
    nch = ((i + 1) * tq + ck - 1) // ck
    qpos = i * tq + lax.broadcasted_iota(I32, (tq, 1), 0)
    qi = qi_ref[0].astype(BF16)
    qi8 = jnp.concatenate([qi[:, h * IDX_DIM:(h + 1) * IDX_DIM] for h in range(N_IDX_HEADS)], axis=0)
    wv = w_ref[0][:, IDX_DIM:IDX_DIM + N_IDX_HEADS] * (IDX_DIM ** -0.5 * N_IDX_HEADS ** -0.5)
    lane_c = lax.broadcasted_iota(I32, (tq, ck), 1)

    def scores(c, _):
        off = pl.multiple_of(c * ck, ck)
        dots = lax.dot_general(qi8, ki_ref[0, pl.ds(off, ck), 0:IDX_DIM], _NT, preferred_element_type=F32)
        sc = _index_scores(dots, wv, tq)
        key_ref[:, pl.ds(off, ck)] = jnp.where(off + lane_c <= qpos, _sortable(sc), INT_MIN)
        return 0

    lax.fori_loop(0, nch, scores, 0)

    def fold(x):
        out = x[:, 0:LANES]
        for a in range(1, ck // LANES):
            out = out + x[:, a * LANES:(a + 1) * LANES]
        return out

    def count(pred):
        def body(c, acc):
            off = pl.multiple_of(c * ck, ck)
            return acc + fold(pred(key_ref[:, pl.ds(off, ck)], off + lane_c).astype(I32))
        return jnp.sum(lax.fori_loop(0, nch, body, jnp.zeros((tq, LANES), I32)), axis=1, keepdims=True)

    thr = _kth_largest_key(lambda cand: count(lambda kx, cx: kx >= cand), tq, DSA_TOPK)
    real = thr > INT_MIN
    need = DSA_TOPK - count(lambda kx, cx: kx > thr)
    n_tie = count(lambda kx, cx: (kx == thr) & real)
    j_ref[...] = jnp.full((tq, 1), t_all, I32)

    @pl.when(jnp.max(jnp.where(n_tie > need, 1, 0)) > 0)
    def _():
        def step(b, jc):
            cand = jc + jnp.left_shift(jnp.int32(1), 13 - b)
            below = count(lambda kx, cx: (kx == thr) & real & (cx < cand))
            return jnp.where(below < need, cand, jc)
        j_ref[...] = lax.fori_loop(0, 14, step, jnp.zeros((tq, 1), I32))

    jmax = j_ref[...]

    def write_bias(c, _):
        off = pl.multiple_of(c * ck, ck)
        kx = key_ref[:, pl.ds(off, ck)]
        keep = (kx > thr) | ((kx == thr) & real & (off + lane_c <= jmax))
        bias_ref[pl.ds(off, ck), :] = jnp.where(keep, 0.0, NEG_BIG).T
        return 0

    lax.fori_loop(0, nch, write_bias, 0)

    assert ck == 2 * FLASH_BLOCK
    nr = rep * tq
    qt = q_ref[0].T
    qas = []
    for kvh in range(HKV_C):
        q4t = jnp.concatenate([qt[(kvh * rep + r) * HEAD_DIM:(kvh * rep + r + 1) * HEAD_DIM] for r in range(rep)],
                              axis=1)
        qas.append(jnp.concatenate([q4t * LOG2E, jnp.zeros((HEAD_DIM, nr), F32)], axis=0).astype(BF16))

    init = tuple((jnp.full((1, nr), NEG_BIG, F32), jnp.zeros((VT_ROWS, nr), F32)) for _ in range(HKV_C))
    res = lax.fori_loop(0, nch, lambda c, carry: _flash_pair(k_ref, v_ref, qas, carry, c, bias_ref), init)
    heads = []
    for kvh in range(HKV_C):
        acc = res[kvh][1]
        ot = acc[0:HEAD_DIM] / acc[HEAD_DIM:HEAD_DIM + 1]
        heads += [ot[:, r * tq:(r + 1) * tq] for r in range(rep)]
    o_ref[0] = jnp.concatenate(heads, axis=0).T


def _dsa_prompt(q, kaug16, vt16, qi, kiwi, kiwi16):
    b, t, _ = q.shape
    o = pl.pallas_call(
        _dsa_prompt_body,
        grid=(b, t // Q_TILE),
        in_specs=[pl.BlockSpec((1, Q_TILE, Q_C), lambda bb, i: (bb, i, 0)),
                  pl.BlockSpec((1, t, HKV_C * LANES), lambda bb, i: (bb, 0, 0)),
                  pl.BlockSpec((1, HKV_C * VT_ROWS, t), lambda bb, i: (bb, 0, 0)),
                  pl.BlockSpec((1, Q_TILE, N_IDX_HEADS * IDX_DIM), lambda bb, i: (bb, i, 0)),
                  pl.BlockSpec((1, Q_TILE, LANES), lambda bb, i: (bb, i, 0)),
                  pl.BlockSpec((1, t, LANES), lambda bb, i: (bb, 0, 0))],
        out_specs=pl.BlockSpec((1, Q_TILE, OUT_C), lambda bb, i: (bb, i, 0)),
        out_shape=jax.ShapeDtypeStruct((b, t, OUT_C), F32),
        scratch_shapes=[pltpu.VMEM((Q_TILE, t), I32), pltpu.VMEM((t, Q_TILE), F32), pltpu.VMEM((Q_TILE, 1), I32)],
        compiler_params=_params(("parallel", "arbitrary")),
        name="dsa_prompt",
    )(q, kaug16, vt16, qi, kiwi, kiwi16)
    return o.reshape(b * t, OUT_C)


def _dsa_sample_body(n_pages, pt_ref, q_ref, kn_ref, vn_ref, qi_ref, kiwi_ref, *refs):
    kp = refs[:n_pages]
    vp = refs[n_pages:2 * n_pages]
    ip = refs[2 * n_pages:3 * n_pages]
    o_ref = refs[3 * n_pages]
    rep = HQ_C // HKV_C
    ts = q_ref.shape[1]
    nrows = HQ_C * ts
    n_cache = n_pages * PAGE_SIZE
    kiwi = kiwi_ref[0]
    qi = qi_ref[0].astype(BF16)
    qi8 = jnp.concatenate([qi[:, h * IDX_DIM:(h + 1) * IDX_DIM] for h in range(N_IDX_HEADS)], axis=0)
    wv = kiwi[:, IDX_DIM:IDX_DIM + N_IDX_HEADS] * (IDX_DIM ** -0.5 * N_IDX_HEADS ** -0.5)
    dots = jnp.concatenate(
        [jnp.dot(qi8, ip[p][0].astype(BF16), preferred_element_type=F32) for p in range(n_pages)]
        + [lax.dot_general(qi8, _pad_rows(kiwi[:, 0:IDX_DIM], LANES).astype(BF16), _NT, preferred_element_type=F32)],
        axis=1)
    sc = _index_scores(dots, wv, ts)
    col = lax.broadcasted_iota(I32, sc.shape, 1)
    trow = lax.broadcasted_iota(I32, sc.shape, 0)
    adm = (col < n_cache) | (col - n_cache <= trow)
    keys = jnp.where(adm, _sortable(sc), INT_MIN)

    def count(mask):
        return jnp.sum(mask.astype(I32), axis=1, keepdims=True)

    n_sel = min(DSA_TOPK, (n_cache + ts) // 4)
    thr = _kth_largest_key(lambda cand: count(keys >= cand), ts, n_sel)
    real = thr > INT_MIN
    need = n_sel - count(keys > thr)
    tie = (keys == thr) & real

    def step(b, jc):
        cand = jc + jnp.left_shift(jnp.int32(1), 13 - b)
        return jnp.where(count(tie & (col < cand)) < need, cand, jc)

    jmax = lax.fori_loop(0, 14, step, jnp.zeros((ts, 1), I32))
    keep = (keys > thr) | (tie & (col <= jmax))
    bias = jnp.where(keep, 0.0, NEG_BIG)
    bias = jnp.concatenate([bias] * HQ_C, axis=0)

    qf16 = _block_diag_queries(q_ref[0], HKV_C, rep).astype(BF16)
    kn16 = _pad_rows(kn_ref[0], LANES).astype(BF16)
    vn16 = _pad_rows(vn_ref[0], LANES).astype(BF16)
    s = jnp.concatenate([jnp.dot(qf16, kp[p][0].astype(BF16), preferred_element_type=F32) for p in range(n_pages)]
                        + [lax.dot_general(qf16, kn16, _NT, preferred_element_type=F32)], axis=1) + bias
    m = jnp.max(s, axis=1, keepdims=True)
    p_all = jnp.exp(s - m)
    l = jnp.sum(p_all, axis=1, keepdims=True)
    p16 = p_all.astype(BF16)
    acc = jnp.dot(p16[:, n_cache:n_cache + LANES], vn16, preferred_element_type=F32)
    for p in range(n_pages):
        acc = acc + lax.dot_general(p16[:, p * PAGE_SIZE:(p + 1) * PAGE_SIZE], vp[p][0].astype(BF16), _NT,
                                    preferred_element_type=F32)
    o_ref[0] = _unstack_block_diag(acc / l, HKV_C, rep, ts)


def _dsa_sample(q, kn, vn, qi, kiwi, cache_k, cache_v, cache_ki, page_table):
    db, ts, _ = q.shape
    n_pages = page_table.shape[1]
    ck = _pages_minor(cache_k)
    cv = _pages_minor(cache_v)
    cache_ki = _pages_minor(cache_ki)
    seq = lambda w: pl.BlockSpec((1, ts, w), lambda i, pt: (i, 0, 0))
    o = pl.pallas_call(
        functools.partial(_dsa_sample_body, n_pages),
        grid_spec=pltpu.PrefetchScalarGridSpec(
            num_scalar_prefetch=1,
            grid=(db,),
            in_specs=[seq(Q_C), seq(KV_C), seq(KV_C), seq(N_IDX_HEADS * IDX_DIM), seq(LANES)]
            + _page_specs(n_pages, KV_C) * 2 + _page_specs(n_pages, IDX_DIM),
            out_specs=seq(OUT_C),
        ),
        out_shape=jax.ShapeDtypeStruct((db, ts, OUT_C), F32),
        compiler_params=_params(("arbitrary",)),
        name="dsa_sample",
    )(page_table.reshape(-1), q, kn, vn, qi, kiwi,
      *([ck] * n_pages), *([cv] * n_pages), *([cache_ki] * n_pages))
    return o.reshape(db * ts, OUT_C)


def _rope_tables(pos):
    half = HEAD_DIM // 2
    inv_freq = ROPE_THETA ** (-jnp.arange(half, dtype=F32) / half)
    ang = pos.astype(F32)[:, None] * inv_freq[None, :]
    c, s = jnp.cos(ang), jnp.sin(ang)
    return jnp.tile(c, (1, LANES // half)), jnp.concatenate([-s, s] * (LANES // HEAD_DIM), axis=1)


def _gain128(g):
    return jnp.tile(g.astype(F32), LANES // HEAD_DIM).reshape(1, LANES)


def _layer_a(xp, xs, shp, shs, states, params, tabs_p, tabs_s):
    norm_g, w_in, q_gain, k_gain, w_out = params
    b, t = shp
    db, ts = shs
    w16 = w_in.astype(BF16)
    args = (norm_g.reshape(1, -1), w16)
    gains = (_gain128(q_gain), _gain128(k_gain))
    qp, kp, vp, kp16, vp16, sgp = _project("a", xp, *args, *tabs_p, *gains)
    qs, ks, vs, _, _, sgs = _project("a", xs, *args, *tabs_s, *gains)
    hw = HKV_A * HEAD_DIM
    grp_p, grp_s, new_state = [], [], []
    for g, (win, dil) in enumerate(DIL_GROUPS):
        grp_p.append(_dil_prompt(qp.reshape(b, t, Q_A), kp.reshape(b, t, KV_A), vp.reshape(b, t, KV_A), g, dil))
        slabs, ns = _dil_sample(qs.reshape(db, ts, Q_A), ks.reshape(db, ts, KV_A), vs.reshape(db, ts, KV_A),
                                states[g], g, states[g].shape[1], dil)
        grp_s.append(slabs)
        wk = min(win, t)
        kg = kp.reshape(b, t, KV_A)[:, t - wk:, g * hw:(g + 1) * hw].reshape(b, wk, 1, HKV_A, HEAD_DIM)
        vg = vp.reshape(b, t, KV_A)[:, t - wk:, g * hw:(g + 1) * hw].reshape(b, wk, 1, HKV_A, HEAD_DIM)
        new_state.append(jnp.concatenate([kg, vg], axis=2))
        new_state.append(ns)
    w_out16 = w_out.astype(BF16)
    return _out_project(grp_p, sgp, xp, w_out16), _out_project(grp_s, sgs, xs, w_out16), new_state


def _layer_b(xp, xs, shp, shs, states, params, tabs_p, tabs_s, page_table):
    norm_g, w_in, q_gain, k_gain, w_out = params
    cache_k, cache_v = states
    b, t = shp
    db, ts = shs
    w16 = w_in.astype(BF16)
    args = (norm_g.reshape(1, -1), w16)
    gains = (_gain128(q_gain), _gain128(k_gain))
    qp, kp, vp, _, _, sgp, kmean, kaug16, vt16 = _project("b", xp, *args, *tabs_p, *gains, aug=True)
    qs, ks, vs, _, _, sgs = _project("b", xs, *args, *tabs_s, *gains)
    o_p = _moba_prompt(qp.reshape(b, t, Q_B), kaug16.reshape(b, t, HKV_B * LANES), vt16,
                       kmean.reshape(b, t // MOBA_BLOCK, KV_B))
    o_s = _moba_sample(qs.reshape(db, ts, Q_B), ks.reshape(db, ts, KV_B), vs.reshape(db, ts, KV_B),
                       cache_k, cache_v, page_table)
    w_out16 = w_out.astype(BF16)
    new_state = [kp.reshape(b, t, HKV_B, HEAD_DIM), ks.reshape(db, ts, HKV_B, HEAD_DIM),
                 vp.reshape(b, t, HKV_B, HEAD_DIM), vs.reshape(db, ts, HKV_B, HEAD_DIM)]
    return _out_project([[o_p]], sgp, xp, w_out16), _out_project([[o_s]], sgs, xs, w_out16), new_state


def _layer_c(xp, xs, shp, shs, states, params, tabs_p, tabs_s, page_table):
    norm_g, w_in, q_gain, k_gain, w_out = params
    cache_k, cache_v, cache_ki = states
    b, t = shp
    db, ts = shs
    w16 = jnp.pad(w_in, ((0, 0), (0, IN_C_PAD - IN_C))).astype(BF16)
    args = (norm_g.reshape(1, -1), w16)
    gains = (_gain128(q_gain), _gain128(k_gain))
    qp, kp, vp, _, _, sgp, qip, kiwip, kiwip16, kaug16, vt16 = _project("c", xp, *args, *tabs_p, *gains, aug=True)
    qs, ks, vs, _, _, sgs, qis, kiwis, _ = _project("c", xs, *args, *tabs_s, *gains)
    o_p = _dsa_prompt(qp.reshape(b, t, Q_C), kaug16.reshape(b, t, HKV_C * LANES), vt16,
                      qip.reshape(b, t, -1), kiwip.reshape(b, t, LANES), kiwip16.reshape(b, t, LANES))
    o_s = _dsa_sample(qs.reshape(db, ts, Q_C), ks.reshape(db, ts, KV_C), vs.reshape(db, ts, KV_C),
                      qis.reshape(db, ts, -1), kiwis.reshape(db, ts, LANES), cache_k, cache_v, cache_ki, page_table)
    w_out16 = w_out.astype(BF16)
    new_state = [kp.reshape(b, t, HKV_C, HEAD_DIM), ks.reshape(db, ts, HKV_C, HEAD_DIM),
                 vp.reshape(b, t, HKV_C, HEAD_DIM), vs.reshape(db, ts, HKV_C, HEAD_DIM),
                 kiwip.reshape(b, t, LANES)[:, :, :IDX_DIM], kiwis.reshape(db, ts, LANES)[:, :, :IDX_DIM]]
    return _out_project([[o_p]], sgp, xp, w_out16), _out_project([[o_s]], sgs, xs, w_out16), new_state


def kernel(x_prompt, x_sample, state_l0_kv_w128, state_l0_kv_w512, state_l0_kv_w2048, cache_l1_k, cache_l1_v, cache_l2_k, cache_l2_v, cache_l2_kidx, state_l3_kv_w128, state_l3_kv_w512, state_l3_kv_w2048, page_table, l0_norm, l0_w_in, l0_q_norm, l0_k_norm, l0_w_out, l1_norm, l1_w_in, l1_q_norm, l1_k_norm, l1_w_out, l2_norm, l2_w_in, l2_q_norm, l2_k_norm, l2_w_out, l3_norm, l3_w_in, l3_q_norm, l3_k_norm, l3_w_out):
    b, t, _ = x_prompt.shape
    db, ts, _ = x_sample.shape
    assert t % (16 * Q_TILE) == 0 and t % DSA_CHUNK == 0 and (db * ts) % PROJ_ROWS == 0 and PROJ_ROWS % ts == 0
    tabs_p = _rope_tables(jnp.arange(t))
    tabs_s = _rope_tables(PAST_LEN + (jnp.arange(PROJ_ROWS) % ts))
    xp = x_prompt.reshape(b * t, D_MODEL)
    xs = x_sample.reshape(db * ts, D_MODEL)
    states = ((state_l0_kv_w128, state_l0_kv_w512, state_l0_kv_w2048), (cache_l1_k, cache_l1_v),
              (cache_l2_k, cache_l2_v, cache_l2_kidx), (state_l3_kv_w128, state_l3_kv_w512, state_l3_kv_w2048))
    params = ((l0_norm, l0_w_in, l0_q_norm, l0_k_norm, l0_w_out), (l1_norm, l1_w_in, l1_q_norm, l1_k_norm, l1_w_out),
              (l2_norm, l2_w_in, l2_q_norm, l2_k_norm, l2_w_out), (l3_norm, l3_w_in, l3_q_norm, l3_k_norm, l3_w_out))
    new_state = []
    for i in range(4):
        common = (xp, xs, (b, t), (db, ts), states[i], params[i], tabs_p, tabs_s)
        if i % 3 == 0:
            xp, xs, st = _layer_a(*common)
        elif i % 3 == 1:
            xp, xs, st = _layer_b(*common, page_table)
        else:
            xp, xs, st = _layer_c(*common, page_table)
        new_state.extend(st)
    return (xp.reshape(b, t, D_MODEL), xs.reshape(db, ts, D_MODEL), *new_state)
```

```python
import functools

import jax
import jax.numpy as jnp
from jax import lax
from jax.experimental import pallas as pl
from jax.experimental.pallas import tpu as pltpu

F32 = jnp.float32
BF16 = jnp.bfloat16
I32 = jnp.int32

D_MODEL = 1024
PAST_LEN = 2048
PAGE_SIZE = 128
HEAD_DIM = 64
ROPE_THETA = 10000.0
NORM_EPS = 1e-6

DIL_GROUPS = ((128, 1), (512, 4), (2048, 16))
HQ_A, HKV_A = 8, 2
Q_A, KV_A, OUT_A = 1536, 384, 512

HQ_B, HKV_B = 16, 4
MOBA_BLOCK, MOBA_TOPK = 256, 3
Q_B, KV_B, OUT_B = 1024, 256, 1024

HQ_C, HKV_C = 16, 4
N_IDX_HEADS, IDX_DIM, DSA_TOPK = 8, 64, 256
Q_C, KV_C, OUT_C = 1024, 256, 1024
IN_C = Q_C + 2 * KV_C + OUT_C + N_IDX_HEADS * IDX_DIM + IDX_DIM + N_IDX_HEADS
IN_C_PAD = 3200

LANES = 128
PROJ_ROWS = 256
Q_TILE = 128
DSA_CHUNK = 512
VMEM_LIMIT = 56 * 1024 * 1024
NEG_BIG = -1e30
LOG2E = 1.4426950408889634
VT_ROWS = 80
INT_MIN = -2147483648

_NT = (((1,), (1,)), ((), ()))


def _params(sem):
    return pltpu.CompilerParams(dimension_semantics=sem, vmem_limit_bytes=VMEM_LIMIT)


def _group_sum_matrix():
    r = lax.broadcasted_iota(I32, (LANES, LANES), 0) // HEAD_DIM
    c = lax.broadcasted_iota(I32, (LANES, LANES), 1) // HEAD_DIM
    return (r == c).astype(BF16)


def _rope(y, cos, sin):
    lane = lax.broadcasted_iota(I32, y.shape, 1)
    first_half = (lane & (HEAD_DIM // 2)) == 0
    partner = jnp.where(first_half, pltpu.roll(y, LANES - HEAD_DIM // 2, 1), pltpu.roll(y, HEAD_DIM // 2, 1))
    return y * cos + partner * sin


def _head_norm(x, gain, gmat):
    ss = x * x
    hi = ss.astype(BF16)
    lo = (ss - hi.astype(F32)).astype(BF16)
    gs = jnp.dot(hi, gmat, preferred_element_type=F32) + jnp.dot(lo, gmat, preferred_element_type=F32)
    return x * lax.rsqrt(gs * (1.0 / HEAD_DIM) + NORM_EPS) * gain


def _proj_body(kind, aug_tiles, x_ref, g_ref, w_ref, cos_ref, sin_ref, qg_ref, kg_ref, *outs):
    x = x_ref[...]
    ms = jnp.mean(x * x, axis=-1, keepdims=True)
    h = (x * lax.rsqrt(ms + NORM_EPS) * g_ref[...]).astype(BF16)
    cos = cos_ref[...]
    sin = sin_ref[...]
    qg = qg_ref[...]
    kg = kg_ref[...]
    gmat = _group_sum_matrix()
    qw, kvw, gw = {"a": (Q_A, KV_A, OUT_A), "b": (Q_B, KV_B, OUT_B), "c": (Q_C, KV_C, OUT_C)}[kind]
    q_ref, k_ref, v_ref, k16_ref, v16_ref, sg_ref = outs[:6]
    rest = outs[6:]
    moba = aug_tiles and kind == "b"
    if aug_tiles:
        kaug_ref, vt_ref = rest[-2:]

    def seg(start, width):
        return jnp.dot(h, w_ref[:, start:start + width], preferred_element_type=F32)

    zq = seg(0, qw)
    for c in range(qw // LANES):
        y = _rope(_head_norm(zq[:, c * LANES:(c + 1) * LANES], qg, gmat), cos, sin)
        q_ref[:, c * LANES:(c + 1) * LANES] = y * (HEAD_DIM ** -0.5)
    zk = seg(qw, kvw)
    ksum = []
    for c in range(kvw // LANES):
        y = _rope(_head_norm(zk[:, c * LANES:(c + 1) * LANES], kg, gmat), cos, sin)
        k_ref[:, c * LANES:(c + 1) * LANES] = y
        k16_ref[:, c * LANES:(c + 1) * LANES] = y.astype(BF16)
        if moba:
            ksum.append(jnp.sum(y, axis=0, keepdims=True) * (1.0 / MOBA_BLOCK))
        if aug_tiles:
            lane = lax.broadcasted_iota(I32, y.shape, 1)
            hot = HEAD_DIM + pl.program_id(0) % aug_tiles if moba else -1
            onehot = (lane == hot).astype(F32)
            kaug_ref[:, 2 * c * LANES:(2 * c + 1) * LANES] = jnp.where(lane < HEAD_DIM, y, onehot).astype(BF16)
            kaug_ref[:, (2 * c + 1) * LANES:(2 * c + 2) * LANES] = jnp.where(
                lane < HEAD_DIM, pltpu.roll(y, HEAD_DIM, 1), onehot).astype(BF16)
    zv = seg(qw + kvw, kvw)
    v_ref[...] = zv
    v16_ref[...] = zv.astype(BF16)
    if aug_tiles:
        zvt = zv.T
        tail = (lax.broadcasted_iota(I32, (VT_ROWS - HEAD_DIM, zvt.shape[1]), 0) == 0).astype(BF16)
        for kvh in range(kvw // HEAD_DIM):
            vt_ref[0, kvh * VT_ROWS:kvh * VT_ROWS + HEAD_DIM, :] = zvt[kvh * HEAD_DIM:(kvh + 1) * HEAD_DIM].astype(BF16)
            vt_ref[0, kvh * VT_ROWS + HEAD_DIM:(kvh + 1) * VT_ROWS, :] = tail
    zg = seg(qw + 2 * kvw, gw)
    sg_ref[...] = zg / (1.0 + jnp.exp(-zg))
    nrest = 0
    if kind == "c":
        qi_ref, kiwi_ref, kiwi16_ref = rest[:3]
        nrest = 3
        base = qw + 2 * kvw + gw
        zi = seg(base, N_IDX_HEADS * IDX_DIM)
        for c in range(N_IDX_HEADS * IDX_DIM // LANES):
            qi_ref[:, c * LANES:(c + 1) * LANES] = _rope(zi[:, c * LANES:(c + 1) * LANES], cos, sin)
        zz = seg(base + N_IDX_HEADS * IDX_DIM, LANES)
        lane = lax.broadcasted_iota(I32, zz.shape, 1)
        kiwi = jnp.where(lane < IDX_DIM, _rope(zz, cos, sin), zz)
        kiwi_ref[...] = kiwi
        kiwi16_ref[...] = kiwi.astype(BF16)
    if moba:
        km_ref = rest[0]
        for c in range(kvw // LANES):
            km_ref[0, :, c * LANES:(c + 1) * LANES] = ksum[c]


def _project(kind, x, gain, w16, cos, sin, qg, kg, aug=False):
    n = x.shape[0]
    tm = PROJ_ROWS
    nt = cos.shape[0] // tm
    qw, kvw, gw = {"a": (Q_A, KV_A, OUT_A), "b": (Q_B, KV_B, OUT_B), "c": (Q_C, KV_C, OUT_C)}[kind]
    wp = w16.shape[1]

    def rows(width):
        return pl.BlockSpec((tm, width), lambda i: (i, 0))

    def const(shape):
        return pl.BlockSpec(shape, lambda i: (0,) * len(shape))

    out_shape = [jax.ShapeDtypeStruct((n, qw), F32), jax.ShapeDtypeStruct((n, kvw), F32),
                 jax.ShapeDtypeStruct((n, kvw), F32), jax.ShapeDtypeStruct((n, kvw), BF16),
                 jax.ShapeDtypeStruct((n, kvw), BF16), jax.ShapeDtypeStruct((n, gw), F32)]
    out_specs = [rows(qw), rows(kvw), rows(kvw), rows(kvw), rows(kvw), rows(gw)]
    if kind == "c":
        out_shape += [jax.ShapeDtypeStruct((n, N_IDX_HEADS * IDX_DIM), F32),
                      jax.ShapeDtypeStruct((n, LANES), F32), jax.ShapeDtypeStruct((n, LANES), BF16)]
        out_specs += [rows(N_IDX_HEADS * IDX_DIM), rows(LANES), rows(LANES)]
    if aug:
        n_kv = kvw // HEAD_DIM
        if kind == "b":
            out_shape.append(jax.ShapeDtypeStruct((n // tm, 1, kvw), F32))
            out_specs.append(pl.BlockSpec((1, 1, kvw), lambda i: (i, 0, 0)))
        out_shape += [jax.ShapeDtypeStruct((n, n_kv * LANES), BF16),
                      jax.ShapeDtypeStruct((n // (nt * tm), n_kv * VT_ROWS, nt * tm), BF16)]
        out_specs += [rows(n_kv * LANES), pl.BlockSpec((1, n_kv * VT_ROWS, tm), lambda i: (i // nt, 0, i % nt))]
    return pl.pallas_call(
        functools.partial(_proj_body, kind, nt if aug else 0),
        grid=(n // tm,),
        in_specs=[rows(D_MODEL), const((1, D_MODEL)), const((D_MODEL, wp)),
                  pl.BlockSpec((tm, LANES), lambda i: (i % nt, 0)),
                  pl.BlockSpec((tm, LANES), lambda i: (i % nt, 0)),
                  const((1, LANES)), const((1, LANES))],
        out_specs=out_specs,
        out_shape=out_shape,
        compiler_params=_params(("parallel",)),
        name=f"proj_{kind}",
    )(x, gain, w16, cos, sin, qg, kg)


LSE_REP = LANES // HQ_A
O_SLABS = OUT_A // LANES


def _outproj_body(n_groups, *refs):
    per = O_SLABS + 1
    sg_ref, x_ref, w_ref, y_ref = refs[-4:]
    if n_groups > 1:
        lses = [refs[g * per + O_SLABS][...] for g in range(n_groups)]
        m = functools.reduce(jnp.maximum, lses)
        es = [jnp.exp(l - m) for l in lses]
        tot = functools.reduce(lambda a, b: a + b, es)
        lane = lax.broadcasted_iota(I32, m.shape, 1)
        slabs = []
        for c in range(O_SLABS):
            acc = None
            for g in range(n_groups):
                wg = es[g] / tot
                lo = wg[:, 2 * c * LSE_REP:2 * c * LSE_REP + 1]
                hi = wg[:, (2 * c + 1) * LSE_REP:(2 * c + 1) * LSE_REP + 1]
                term = jnp.where(lane < HEAD_DIM, lo, hi) * refs[g * per + c][...]
                acc = term if acc is None else acc + term
            slabs.append(acc)
        o = jnp.concatenate(slabs, axis=1)
    else:
        o = refs[0][...]
    g16 = (o * sg_ref[...]).astype(BF16)
    y_ref[...] = x_ref[...] + jnp.dot(g16, w_ref[...], preferred_element_type=F32)


def _out_project(groups, sg, x, w16):
    n = x.shape[0]
    f = sg.shape[1]
    tm = PROJ_ROWS
    ins = [a for grp in groups for a in grp]
    in_specs = [pl.BlockSpec((tm, a.shape[1]), lambda i: (i, 0)) for a in ins]
    in_specs += [pl.BlockSpec((tm, f), lambda i: (i, 0)), pl.BlockSpec((tm, D_MODEL), lambda i: (i, 0)),
                 pl.BlockSpec((f, D_MODEL), lambda i: (0, 0))]
    return pl.pallas_call(
        functools.partial(_outproj_body, len(groups)),
        grid=(n // tm,),
        in_specs=in_specs,
        out_specs=pl.BlockSpec((tm, D_MODEL), lambda i: (i, 0)),
        out_shape=jax.ShapeDtypeStruct((n, D_MODEL), F32),
        compiler_params=_params(("parallel",)),
        name="out_proj_merge" if len(groups) > 1 else "out_proj",
    )(*ins, sg, x, w16)


def _stack_heads(q, kvh, rep):
    return jnp.concatenate([q[:, (kvh * rep + r) * HEAD_DIM:(kvh * rep + r + 1) * HEAD_DIM] for r in range(rep)],
                           axis=0)


FLASH_BLOCK = 256


def _flash_pair(k_ref, v_ref, qas, carry, n, bias_ref):
    carry = list(carry)
    units = [(j, kvh) for j in range(2) for kvh in range(len(qas))]
    offs = [pl.multiple_of((2 * n + j) * FLASH_BLOCK, FLASH_BLOCK) for j in range(2)]
    rep = qas[0].shape[1] // Q_TILE
    scores = []
    for j, kvh in units:
        st = jnp.dot(k_ref[0, pl.ds(offs[j], FLASH_BLOCK), kvh * LANES:(kvh + 1) * LANES], qas[kvh],
                     preferred_element_type=F32)
        if bias_ref is not None:
            st = st + jnp.concatenate([bias_ref[pl.ds(offs[j], FLASH_BLOCK), :]] * rep, axis=1)
        scores.append(st)
    for (j, kvh), st in zip(units, scores):
        m, acc = carry[kvh]
        m_new = jnp.maximum(m, jnp.max(st, axis=0, keepdims=True))
        p = jnp.exp2(st - m_new)
        pv = jnp.dot(v_ref[0, kvh * VT_ROWS:(kvh + 1) * VT_ROWS, pl.ds(offs[j], FLASH_BLOCK)], p.astype(BF16),
                     preferred_element_type=F32)
        carry[kvh] = (m_new, jnp.exp2(m - m_new) * acc + pv)
    return tuple(carry)


def _lse_slab(lses, rows):
    return jnp.concatenate([jnp.broadcast_to(l, (rows, LSE_REP)) for l in lses], axis=1)


def _dil_prompt_body(dil, *refs):
    q_refs = refs[:O_SLABS]
    kp_ref, kc_ref, vp_ref, vc_ref = refs[O_SLABS:O_SLABS + 4]
    o_refs = refs[O_SLABS + 4:2 * O_SLABS + 4]
    lse_ref = refs[2 * O_SLABS + 4]
    j = pl.program_id(1)
    tq = Q_TILE
    rep = HQ_A // HKV_A
    row = lax.broadcasted_iota(I32, (tq, 2 * tq), 0)
    col = lax.broadcasted_iota(I32, (tq, 2 * tq), 1)
    dist = row + tq - col
    valid = (dist >= 0) & (dist <= tq) & ((col >= tq) | (j > 0))
    valid = jnp.concatenate([valid] * rep, axis=0)

    def residue(r, _):
        rows = pl.ds(r, tq, stride=dil) if dil > 1 else pl.ds(0, tq)
        q = jnp.concatenate([qr[0, rows, :] for qr in q_refs], axis=1).astype(BF16)
        kk = jnp.concatenate([kp_ref[0, rows, :], kc_ref[0, rows, :]], axis=0).astype(BF16)
        vv = jnp.concatenate([vp_ref[0, rows, :], vc_ref[0, rows, :]], axis=0).astype(BF16)
        outs, lses = [], []
        for kvh in range(HKV_A):
            kh = kk[:, kvh * HEAD_DIM:(kvh + 1) * HEAD_DIM]
            vh = vv[:, kvh * HEAD_DIM:(kvh + 1) * HEAD_DIM]
            qh = _stack_heads(q, kvh, rep)
            s = lax.dot_general(qh, kh, _NT, preferred_element_type=F32)
            s = jnp.where(valid, s, -jnp.inf)
            m = jnp.max(s, axis=1, keepdims=True)
            p = jnp.exp(s - m)
            l = jnp.sum(p, axis=1, keepdims=True)
            o = jnp.dot(p.astype(BF16), vh, preferred_element_type=F32) / l
            lse = m + jnp.log(l)
            for h in range(rep):
                outs.append(o[h * tq:(h + 1) * tq])
                lses.append(lse[h * tq:(h + 1) * tq])
        for c in range(O_SLABS):
            o_refs[c][0, rows, :] = jnp.concatenate(outs[2 * c:2 * c + 2], axis=1)
        lse_ref[0, rows, :] = _lse_slab(lses, tq)
        return 0

    if dil == 1:
        residue(0, 0)
    else:
        lax.fori_loop(0, dil, residue, 0)


def _dil_prompt(q, k, v, g, dil):
    b, t, _ = q.shape
    blk = Q_TILE * dil
    hw = HKV_A * HEAD_DIM
    cur = lambda bb, j: (bb, j, g)
    prev = lambda bb, j: (bb, jnp.maximum(j - 1, 0), g)
    slab = pl.BlockSpec((1, blk, LANES), lambda bb, j: (bb, j, 0))
    outs = pl.pallas_call(
        functools.partial(_dil_prompt_body, dil),
        grid=(b, t // blk),
        in_specs=[pl.BlockSpec((1, blk, LANES), functools.partial(lambda c, bb, j: (bb, j, g * O_SLABS + c), c))
                  for c in range(O_SLABS)]
        + [pl.BlockSpec((1, blk, hw), prev), pl.BlockSpec((1, blk, hw), cur),
           pl.BlockSpec((1, blk, hw), prev), pl.BlockSpec((1, blk, hw), cur)],
        out_specs=[slab] * (O_SLABS + 1),
        out_shape=[jax.ShapeDtypeStruct((b, t, LANES), F32)] * (O_SLABS + 1),
        compiler_params=_params(("parallel", "arbitrary")),
        name=f"dil_prompt_g{g}",
    )(*([q] * O_SLABS), k, k, v, v)
    return [a.reshape(b * t, LANES) for a in outs]


def _dil_sample_body(win, dil, sb, q_ref, kn_ref, vn_ref, st_ref, *out_refs):
    o_refs = out_refs[:O_SLABS]
    lse_ref, ns_ref = out_refs[O_SLABS:]
    rep = HQ_A // HKV_A
    hw = HKV_A * HEAD_DIM
    ts = q_ref.shape[1]
    row = lax.broadcasted_iota(I32, (rep * ts, win), 0)
    col = lax.broadcasted_iota(I32, (rep * ts, win), 1)
    dist = win + (row & (ts - 1)) - col
    valid_c = (dist <= win) & ((dist & (dil - 1)) == 0)
    rown = lax.broadcasted_iota(I32, (rep * ts, LANES), 0)
    coln = lax.broadcasted_iota(I32, (rep * ts, LANES), 1)
    distn = (rown & (ts - 1)) - coln
    valid_n = (distn >= 0) & ((distn & (dil - 1)) == 0)
    lane_w = lax.broadcasted_iota(I32, (2 * hw, LANES), 1)
    for s_i in range(sb):
        st = st_ref[s_i]
        knp = _pad_rows(kn_ref[s_i], LANES)
        vnp = _pad_rows(vn_ref[s_i], LANES)
        shifted = pltpu.roll(st, win - ts, 1)
        newcols = pltpu.roll(jnp.concatenate([knp.T, vnp.T], axis=0), LANES - ts, 1)
        if win > LANES:
            ns_ref[s_i, :, 0:win - LANES] = shifted[:, 0:win - LANES]
        ns_ref[s_i, :, win - LANES:win] = jnp.where(lane_w >= LANES - ts, newcols, shifted[:, win - LANES:win])
        q = q_ref[s_i].astype(BF16)
        kn16 = knp.astype(BF16)
        vn16 = vnp.astype(BF16)
        outs, lses = [], []
        for kvh in range(HKV_A):
            hs = slice(kvh * HEAD_DIM, (kvh + 1) * HEAD_DIM)
            kt = st[kvh * HEAD_DIM:(kvh + 1) * HEAD_DIM].astype(BF16)
            vt = st[hw + kvh * HEAD_DIM:hw + (kvh + 1) * HEAD_DIM].astype(BF16)
            qh = _stack_heads(q, kvh, rep)
            s_c = jnp.where(valid_c, jnp.dot(qh, kt, preferred_element_type=F32), -jnp.inf)
            s_n = jnp.where(valid_n, lax.dot_general(qh, kn16[:, hs], _NT, preferred_element_type=F32), -jnp.inf)
            m = jnp.maximum(jnp.max(s_c, axis=1, keepdims=True), jnp.max(s_n, axis=1, keepdims=True))
            p_c = jnp.exp(s_c - m)
            p_n = jnp.exp(s_n - m)
            l = jnp.sum(p_c, axis=1, keepdims=True) + jnp.sum(p_n, axis=1, keepdims=True)
            o = (lax.dot_general(p_c.astype(BF16), vt, _NT, preferred_element_type=F32)
                 + jnp.dot(p_n.astype(BF16), vn16[:, hs], preferred_element_type=F32)) / l
            lse = m + jnp.log(l)
            for r in range(rep):
                outs.append(o[r * ts:(r + 1) * ts])
                lses.append(lse[r * ts:(r + 1) * ts])
        for c in range(O_SLABS):
            o_refs[c][s_i] = jnp.concatenate(outs[2 * c:2 * c + 2], axis=1)
        lse_ref[s_i] = _lse_slab(lses, ts)


def _dil_sample(q, kn, vn, state, g, win, dil):
    db, ts, _ = q.shape
    hw = HKV_A * HEAD_DIM
    st = state.transpose(0, 2, 3, 4, 1).reshape(db, 2 * hw, win)
    sb = max(1, min(8, 1024 // win))
    slab = pl.BlockSpec((sb, ts, LANES), lambda i: (i, 0, 0))
    *slabs, ns = pl.pallas_call(
        functools.partial(_dil_sample_body, win, dil, sb),
        grid=(db // sb,),
        in_specs=[pl.BlockSpec((sb, ts, OUT_A), lambda i: (i, 0, g)),
                  pl.BlockSpec((sb, ts, hw), lambda i: (i, 0, g)),
                  pl.BlockSpec((sb, ts, hw), lambda i: (i, 0, g)),
                  pl.BlockSpec((sb, 2 * hw, win), lambda i: (i, 0, 0))],
        out_specs=[slab] * (O_SLABS + 1) + [pl.BlockSpec((sb, 2 * hw, win), lambda i: (i, 0, 0))],
        out_shape=[jax.ShapeDtypeStruct((db, ts, LANES), F32)] * (O_SLABS + 1)
        + [jax.ShapeDtypeStruct((db, 2 * hw, win), F32)],
        compiler_params=_params(("parallel",)),
        name=f"dil_sample_g{g}",
    )(q, kn, vn, st)
    ns = ns.reshape(db, 2, HKV_A, HEAD_DIM, win).transpose(0, 4, 1, 2, 3)
    return [a.reshape(db * ts, LANES) for a in slabs], ns


def _top_blocks(bs, n_valid, topk):
    lane = lax.broadcasted_iota(I32, bs.shape, 1)
    nb = bs.shape[1]
    work = jnp.where(lane < n_valid, bs, -jnp.inf)
    chosen = jnp.zeros(bs.shape, jnp.bool_)
    for _ in range(topk):
        m = jnp.max(work, axis=1, keepdims=True)
        idx = jnp.min(jnp.where(work == m, lane, nb), axis=1, keepdims=True)
        pick = lane == idx
        chosen = chosen | pick
        work = jnp.where(pick, -jnp.inf, work)
    return chosen & (lane < n_valid)


def _top_blocks_t(bs, n_valid, topk):
    row = lax.broadcasted_iota(I32, bs.shape, 0)
    rowf = row.astype(F32)
    work = jnp.where(row < n_valid, bs, -jnp.inf)
    chosen = jnp.zeros(bs.shape, jnp.bool_)
    for _ in range(topk):
        m = jnp.max(work, axis=0, keepdims=True)
        idx = jnp.min(jnp.where(work == m, rowf, float(bs.shape[0])), axis=0, keepdims=True)
        pick = rowf == idx
        chosen = chosen | pick
        work = jnp.where(pick, -jnp.inf, work)
    return chosen & (row < n_valid)


def _moba_prompt_body(q_ref, k_ref, v_ref, km_ref, o_ref):
    i = pl.program_id(1)
    tq = Q_TILE
    rep = HQ_B // HKV_B
    nb = km_ref.shape[1]
    nr = rep * tq
    own = (i * tq) // MOBA_BLOCK
    own0 = pl.multiple_of(own * MOBA_BLOCK, MOBA_BLOCK)
    qt = q_ref[0].T
    km = km_ref[0]
    krow = lax.broadcasted_iota(I32, (MOBA_BLOCK, nr), 0)
    qcol = lax.broadcasted_iota(I32, (MOBA_BLOCK, nr), 1)
    causal = own0 + krow <= i * tq + (qcol & (tq - 1))
    brow = lax.broadcasted_iota(I32, (HEAD_DIM, nr), 0)
    qas_past, init = [], []
    for kvh in range(HKV_B):
        hs = slice(kvh * HEAD_DIM, (kvh + 1) * HEAD_DIM)
        ls = slice(kvh * LANES, (kvh + 1) * LANES)
        vs = slice(kvh * VT_ROWS, (kvh + 1) * VT_ROWS)
        q4t = jnp.concatenate([qt[(kvh * rep + r) * HEAD_DIM:(kvh * rep + r + 1) * HEAD_DIM] for r in range(rep)],
                              axis=1)
        bst = jnp.dot(km[:, hs], q4t, precision=lax.Precision.HIGHEST, preferred_element_type=F32)
        if nb < HEAD_DIM:
            bst = jnp.concatenate([bst, jnp.zeros((HEAD_DIM - nb, nr), F32)], axis=0)
        chosen = _top_blocks_t(bst, own, MOBA_TOPK)
        qs = q4t * LOG2E
        qas_past.append(jnp.concatenate([qs, jnp.where(chosen, 0.0, NEG_BIG)], axis=0).astype(BF16))
        qa = jnp.concatenate([qs, jnp.where(brow == own, 0.0, NEG_BIG)], axis=0).astype(BF16)
        st = jnp.dot(k_ref[0, pl.ds(own0, MOBA_BLOCK), ls], qa, preferred_element_type=F32)
        st = jnp.where(causal, st, -jnp.inf)
        m = jnp.max(st, axis=0, keepdims=True)
        p = jnp.exp2(st - m)
        init.append((m, jnp.dot(v_ref[0, vs, pl.ds(own0, MOBA_BLOCK)], p.astype(BF16), preferred_element_type=F32)))

    res = lax.fori_loop(0, (own + 1) // 2,
                        lambda n, carry: _flash_pair(k_ref, v_ref, qas_past, carry, n, None), tuple(init))
    heads = []
    for kvh in range(HKV_B):
        acc = res[kvh][1]
        ot = acc[0:HEAD_DIM] / acc[HEAD_DIM:HEAD_DIM + 1]
        heads += [ot[:, r * tq:(r + 1) * tq] for r in range(rep)]
    o_ref[0] = jnp.concatenate(heads, axis=0).T


def _moba_prompt(q, kaug16, vt16, kmean):
    b, t, _ = q.shape
    nb = t // MOBA_BLOCK
    o = pl.pallas_call(
        _moba_prompt_body,
        grid=(b, t // Q_TILE),
        in_specs=[pl.BlockSpec((1, Q_TILE, Q_B), lambda bb, i: (bb, i, 0)),
                  pl.BlockSpec((1, t, HKV_B * LANES), lambda bb, i: (bb, 0, 0)),
                  pl.BlockSpec((1, HKV_B * VT_ROWS, t), lambda bb, i: (bb, 0, 0)),
                  pl.BlockSpec((1, nb, KV_B), lambda bb, i: (bb, 0, 0))],
        out_specs=pl.BlockSpec((1, Q_TILE, OUT_B), lambda bb, i: (bb, i, 0)),
        out_shape=jax.ShapeDtypeStruct((b, t, OUT_B), F32),
        compiler_params=_params(("parallel", "arbitrary")),
        name="moba_prompt",
    )(q, kaug16, vt16, kmean)
    return o.reshape(b * t, OUT_B)


def _block_diag_queries(q, n_kv, rep):
    ts = q.shape[0]
    rows = []
    for kvh in range(n_kv):
        for r in range(rep):
            hd = (kvh * rep + r) * HEAD_DIM
            parts = []
            if kvh > 0:
                parts.append(jnp.zeros((ts, kvh * HEAD_DIM), q.dtype))
            parts.append(q[:, hd:hd + HEAD_DIM])
            if kvh < n_kv - 1:
                parts.append(jnp.zeros((ts, (n_kv - 1 - kvh) * HEAD_DIM), q.dtype))
            rows.append(jnp.concatenate(parts, axis=1))
    return jnp.concatenate(rows, axis=0)


def _unstack_block_diag(o_all, n_kv, rep, ts):
    parts = []
    for kvh in range(n_kv):
        for r in range(rep):
            r0 = (kvh * rep + r) * ts
            parts.append(o_all[r0:r0 + ts, kvh * HEAD_DIM:(kvh + 1) * HEAD_DIM])
    return jnp.concatenate(parts, axis=1)


def _pad_rows(x, rows):
    return jnp.concatenate([x, jnp.zeros((rows - x.shape[0], x.shape[1]), x.dtype)], axis=0)


def _moba_sample_body(n_pages, pt_ref, q_ref, kn_ref, vn_ref, *refs):
    kp = refs[:n_pages]
    vp = refs[n_pages:2 * n_pages]
    o_ref = refs[2 * n_pages]
    rep = HQ_B // HKV_B
    ts = q_ref.shape[1]
    nrows = HQ_B * ts
    ppb = MOBA_BLOCK // PAGE_SIZE
    n_past = n_pages // ppb
    qf = _block_diag_queries(q_ref[0], HKV_B, rep)
    qf16 = qf.astype(BF16)
    lane_k = lax.broadcasted_iota(I32, (KV_B, LANES), 1)
    kmt = jnp.zeros((KV_B, LANES), F32)
    for n in range(n_past):
        tot = kp[n * ppb][0]
        for j in range(1, ppb):
            tot = tot + kp[n * ppb + j][0]
        kmt = jnp.where(lane_k == n, jnp.sum(tot, axis=1, keepdims=True) * (1.0 / MOBA_BLOCK), kmt)
    bs = jnp.dot(qf, kmt, precision=lax.Precision.HIGHEST, preferred_element_type=F32)
    chosen = _top_blocks(bs, n_past, MOBA_TOPK)
    selb = jnp.where(chosen, 0.0, NEG_BIG)
    bias = jnp.concatenate([jnp.broadcast_to(selb[:, n:n + 1], (nrows, MOBA_BLOCK)) for n in range(n_past)], axis=1)
    s_c = jnp.concatenate([jnp.dot(qf16, kp[p][0].astype(BF16), preferred_element_type=F32)
                           for p in range(n_pages)], axis=1) + bias
    kn16 = _pad_rows(kn_ref[0], LANES).astype(BF16)
    vn16 = _pad_rows(vn_ref[0], LANES).astype(BF16)
    s_n = lax.dot_general(qf16, kn16, _NT, preferred_element_type=F32)
    row = lax.broadcasted_iota(I32, (nrows, LANES), 0)
    col = lax.broadcasted_iota(I32, (nrows, LANES), 1)
    s_n = jnp.where(col <= (row & (ts - 1)), s_n, -jnp.inf)
    m = jnp.maximum(jnp.max(s_c, axis=1, keepdims=True), jnp.max(s_n, axis=1, keepdims=True))
    p_c = jnp.exp(s_c - m)
    p_n = jnp.exp(s_n - m)
    l = jnp.sum(p_c, axis=1, keepdims=True) + jnp.sum(p_n, axis=1, keepdims=True)
    acc = jnp.dot(p_n.astype(BF16), vn16, preferred_element_type=F32)
    p16 = p_c.astype(BF16)
    for p in range(n_pages):
        acc = acc + lax.dot_general(p16[:, p * PAGE_SIZE:(p + 1) * PAGE_SIZE], vp[p][0].astype(BF16), _NT,
                                    preferred_element_type=F32)
    o_ref[0] = _unstack_block_diag(acc / l, HKV_B, rep, ts)


def _page_specs(n_pages, width):
    return [pl.BlockSpec((1, width, PAGE_SIZE), functools.partial(lambda p, i, pt: (pt[i * n_pages + p], 0, 0), p))
            for p in range(n_pages)]


def _pages_minor(cache):
    n_pool = cache.shape[0]
    return jnp.moveaxis(cache, 1, -1).reshape(n_pool, -1, PAGE_SIZE)


def _moba_sample(q, kn, vn, cache_k, cache_v, page_table):
    db, ts, _ = q.shape
    n_pages = page_table.shape[1]
    ck = _pages_minor(cache_k)
    cv = _pages_minor(cache_v)
    seq = lambda w: pl.BlockSpec((1, ts, w), lambda i, pt: (i, 0, 0))
    o = pl.pallas_call(
        functools.partial(_moba_sample_body, n_pages),
        grid_spec=pltpu.PrefetchScalarGridSpec(
            num_scalar_prefetch=1,
            grid=(db,),
            in_specs=[seq(Q_B), seq(KV_B), seq(KV_B)] + _page_specs(n_pages, KV_B) * 2,
            out_specs=seq(OUT_B),
        ),
        out_shape=jax.ShapeDtypeStruct((db, ts, OUT_B), F32),
        compiler_params=_params(("arbitrary",)),
        name="moba_sample",
    )(page_table.reshape(-1), q, kn, vn, *([ck] * n_pages), *([cv] * n_pages))
    return o.reshape(db * ts, OUT_B)


def _sortable(score):
    bits = pltpu.bitcast(score + 0.0, I32)
    return jnp.where(bits >= 0, bits, bits ^ 0x7FFFFFFF)


def _kth_largest_key1(count_ge, shape, k):
    def step(b, t):
        cand = jnp.where(b == 0, jnp.zeros_like(t), t | jnp.left_shift(jnp.int32(1), 31 - b))
        return jnp.where(count_ge([cand])[0] >= k, cand, t)

    return lax.fori_loop(0, 32, step, jnp.full(shape, INT_MIN, I32))


def _kth_largest_key2(count_ge, shape, k):
    def pick(t, cands):
        for cd, cnt in zip(cands, count_ge(cands)):
            t = jnp.where(cnt >= k, cd, t)
        return t

    t = pick(jnp.full(shape, INT_MIN, I32),
             tuple(jnp.full(shape, v, I32) for v in (INT_MIN + (1 << 30), 0, 1 << 30)))

    def step(b, t):
        sh = 28 - 2 * b
        return pick(t, tuple(t | jnp.left_shift(jnp.int32(j), sh) for j in (1, 2, 3)))

    return lax.fori_loop(0, 15, step, t)


def _index_scores(dots, wv, rows):
    sc = jnp.maximum(dots[0:rows], 0.0) * wv[:, 0:1]
    for h in range(1, N_IDX_HEADS):
        sc = sc + jnp.maximum(dots[h * rows:(h + 1) * rows], 0.0) * wv[:, h:h + 1]
    return sc


def _dsa_prompt_body(q_ref, k_ref, v_ref, qi_ref, w_ref, ki_ref, o_ref, key_ref, bias_ref, j_ref):
    i = pl.program_id(1)
    tq = Q_TILE
    ck = DSA_CHUNK
    rep = HQ_C // HKV_C
    t_all = key_ref.shape[0]
    nch = ((i + 1) * tq + ck - 1) // ck
    qpos = i * tq + lax.broadcasted_iota(I32, (1, tq), 1)
    qit = qi_ref[0].T
    qi8t = jnp.concatenate([qit[h * IDX_DIM:(h + 1) * IDX_DIM] for h in range(N_IDX_HEADS)], axis=1)
    qi8t = jnp.concatenate([qi8t, jnp.zeros_like(qi8t)], axis=0).astype(BF16)
    wt = w_ref[0].T[IDX_DIM:IDX_DIM + N_IDX_HEADS] * (IDX_DIM ** -0.5 * N_IDX_HEADS ** -0.5)
    row_c = lax.broadcasted_iota(I32, (ck, tq), 0)

    def scores(c, _):
        off = pl.multiple_of(c * ck, ck)
        kic = ki_ref[0, pl.ds(off, ck), :]
        sc = None
        for hp in range(N_IDX_HEADS // 2):
            d2 = jnp.dot(kic, qi8t[:, 2 * hp * tq:(2 * hp + 2) * tq], preferred_element_type=F32)
            for j in range(2):
                h = 2 * hp + j
                term = jnp.maximum(d2[:, j * tq:(j + 1) * tq], 0.0) * wt[h:h + 1]
                sc = term if sc is None else sc + term
        key_ref[pl.ds(off, ck), :] = jnp.where(off + row_c <= qpos, _sortable(sc), INT_MIN)
        return 0

    lax.fori_loop(0, nch, scores, 0)

    def count(preds):
        def body(c, accs):
            off = pl.multiple_of(c * ck, ck)
            kx = key_ref[pl.ds(off, ck), :]
            idx = off + row_c
            return tuple(a + jnp.sum(p(kx, idx).astype(I32).reshape(ck // 8, 8, tq), axis=0)
                         for a, p in zip(accs, preds))
        accs = lax.fori_loop(0, nch, body, tuple(jnp.zeros((8, tq), I32) for _ in preds))
        return [jnp.sum(a, axis=0, keepdims=True) for a in accs]

    thr = _kth_largest_key1(lambda cands: count([functools.partial(lambda cd, kx, idx: kx >= cd, cd) for cd in cands]),
                            (1, tq), DSA_TOPK)
    real = thr > INT_MIN
    n_gt, n_tie = count([lambda kx, idx: kx > thr, lambda kx, idx: (kx == thr) & real])
    need = DSA_TOPK - n_gt
    j_ref[...] = jnp.full((1, tq), t_all, I32)

    @pl.when(jnp.max(jnp.where(n_tie > need, 1, 0)) > 0)
    def _():
        def step(b, jc):
            cand = jc + jnp.left_shift(jnp.int32(1), 13 - b)
            below, = count([lambda kx, idx: (kx == thr) & real & (idx < cand)])
            return jnp.where(below < need, cand, jc)
        j_ref[...] = lax.fori_loop(0, 14, step, jnp.zeros((1, tq), I32))

    jmax = j_ref[...]

    def write_bias(c, _):
        off = pl.multiple_of(c * ck, ck)
        kx = key_ref[pl.ds(off, ck), :]
        keep = (kx > thr) | ((kx == thr) & real & (off + row_c <= jmax))
        bias_ref[pl.ds(off, ck), :] = jnp.where(keep, 0.0, NEG_BIG)
        return 0

    lax.fori_loop(0, nch, write_bias, 0)

    assert ck == 2 * FLASH_BLOCK
    nr = rep * tq
    qt = q_ref[0].T
    qas = []
    for kvh in range(HKV_C):
        q4t = jnp.concatenate([qt[(kvh * rep + r) * HEAD_DIM:(kvh * rep + r + 1) * HEAD_DIM] for r in range(rep)],
                              axis=1)
        qas.append(jnp.concatenate([q4t * LOG2E, jnp.zeros((HEAD_DIM, nr), F32)], axis=0).astype(BF16))

    init = tuple((jnp.full((1, nr), NEG_BIG, F32), jnp.zeros((VT_ROWS, nr), F32)) for _ in range(HKV_C))
    res = lax.fori_loop(0, nch, lambda c, carry: _flash_pair(k_ref, v_ref, qas, carry, c, bias_ref), init)
    heads = []
    for kvh in range(HKV_C):
        acc = res[kvh][1]
        ot = acc[0:HEAD_DIM] / acc[HEAD_DIM:HEAD_DIM + 1]
        heads += [ot[:, r * tq:(r + 1) * tq] for r in range(rep)]
    o_ref[0] = jnp.concatenate(heads, axis=0).T


def _dsa_prompt(q, kaug16, vt16, qi, kiwi, kiwi16):
    b, t, _ = q.shape
    o = pl.pallas_call(
        _dsa_prompt_body,
        grid=(b, t // Q_TILE),
        in_specs=[pl.BlockSpec((1, Q_TILE, Q_C), lambda bb, i: (bb, i, 0)),
                  pl.BlockSpec((1, t, HKV_C * LANES), lambda bb, i: (bb, 0, 0)),
                  pl.BlockSpec((1, HKV_C * VT_ROWS, t), lambda bb, i: (bb, 0, 0)),
                  pl.BlockSpec((1, Q_TILE, N_IDX_HEADS * IDX_DIM), lambda bb, i: (bb, i, 0)),
                  pl.BlockSpec((1, Q_TILE, LANES), lambda bb, i: (bb, i, 0)),
                  pl.BlockSpec((1, t, LANES), lambda bb, i: (bb, 0, 0))],
        out_specs=pl.BlockSpec((1, Q_TILE, OUT_C), lambda bb, i: (bb, i, 0)),
        out_shape=jax.ShapeDtypeStruct((b, t, OUT_C), F32),
        scratch_shapes=[pltpu.VMEM((t, Q_TILE), I32), pltpu.VMEM((t, Q_TILE), F32), pltpu.VMEM((1, Q_TILE), I32)],
        compiler_params=_params(("parallel", "arbitrary")),
        name="dsa_prompt",
    )(q, kaug16, vt16, qi, kiwi, kiwi16)
    return o.reshape(b * t, OUT_C)


def _dsa_sample_body(n_pages, pt_ref, q_ref, kn_ref, vn_ref, qi_ref, kiwi_ref, *refs):
    kp = refs[:n_pages]
    vp = refs[n_pages:2 * n_pages]
    ip = refs[2 * n_pages:3 * n_pages]
    o_ref = refs[3 * n_pages]
    rep = HQ_C // HKV_C
    ts = q_ref.shape[1]
    nrows = HQ_C * ts
    n_cache = n_pages * PAGE_SIZE
    kiwi = kiwi_ref[0]
    qi = qi_ref[0].astype(BF16)
    qi8 = jnp.concatenate([qi[:, h * IDX_DIM:(h + 1) * IDX_DIM] for h in range(N_IDX_HEADS)], axis=0)
    wv = kiwi[:, IDX_DIM:IDX_DIM + N_IDX_HEADS] * (IDX_DIM ** -0.5 * N_IDX_HEADS ** -0.5)
    dots = jnp.concatenate(
        [jnp.dot(qi8, ip[p][0].astype(BF16), preferred_element_type=F32) for p in range(n_pages)]
        + [lax.dot_general(qi8, _pad_rows(kiwi[:, 0:IDX_DIM], LANES).astype(BF16), _NT, preferred_element_type=F32)],
        axis=1)
    sc = _index_scores(dots, wv, ts)
    col = lax.broadcasted_iota(I32, sc.shape, 1)
    trow = lax.broadcasted_iota(I32, sc.shape, 0)
    adm = (col < n_cache) | (col - n_cache <= trow)
    keys = jnp.where(adm, _sortable(sc), INT_MIN)

    def count(mask):
        return jnp.sum(mask.astype(I32), axis=1, keepdims=True)

    n_sel = min(DSA_TOPK, (n_cache + ts) // 4)
    thr = _kth_largest_key2(lambda cands: [count(keys >= cd) for cd in cands], (ts, 1), n_sel)
    real = thr > INT_MIN
    need = n_sel - count(keys > thr)
    tie = (keys == thr) & real

    j_ref = refs[3 * n_pages + 1]
    j_ref[...] = jnp.full((ts, 1), n_cache + LANES, I32)

    @pl.when(jnp.max(jnp.where(count(tie) > need, 1, 0)) > 0)
    def _():
        def step(b, jc):
            cand = jc + jnp.left_shift(jnp.int32(1), 13 - b)
            return jnp.where(count(tie & (col < cand)) < need, cand, jc)
        j_ref[...] = lax.fori_loop(0, 14, step, jnp.zeros((ts, 1), I32))

    jmax = j_ref[...]
    keep = (keys > thr) | (tie & (col <= jmax))
    bias = jnp.where(keep, 0.0, NEG_BIG)
    bias = jnp.concatenate([bias] * HQ_C, axis=0)

    qf16 = _block_diag_queries(q_ref[0], HKV_C, rep).astype(BF16)
    kn16 = _pad_rows(kn_ref[0], LANES).astype(BF16)
    vn16 = _pad_rows(vn_ref[0], LANES).astype(BF16)
    s = jnp.concatenate([jnp.dot(qf16, kp[p][0].astype(BF16), preferred_element_type=F32) for p in range(n_pages)]
                        + [lax.dot_general(qf16, kn16, _NT, preferred_element_type=F32)], axis=1) + bias
    m = jnp.max(s, axis=1, keepdims=True)
    p_all = jnp.exp(s - m)
    l = jnp.sum(p_all, axis=1, keepdims=True)
    p16 = p_all.astype(BF16)
    acc = jnp.dot(p16[:, n_cache:n_cache + LANES], vn16, preferred_element_type=F32)
    for p in range(n_pages):
        acc = acc + lax.dot_general(p16[:, p * PAGE_SIZE:(p + 1) * PAGE_SIZE], vp[p][0].astype(BF16), _NT,
                                    preferred_element_type=F32)
    o_ref[0] = _unstack_block_diag(acc / l, HKV_C, rep, ts)


def _dsa_sample(q, kn, vn, qi, kiwi, cache_k, cache_v, cache_ki, page_table):
    db, ts, _ = q.shape
    n_pages = page_table.shape[1]
    ck = _pages_minor(cache_k)
    cv = _pages_minor(cache_v)
    cache_ki = _pages_minor(cache_ki)
    seq = lambda w: pl.BlockSpec((1, ts, w), lambda i, pt: (i, 0, 0))
    o = pl.pallas_call(
        functools.partial(_dsa_sample_body, n_pages),
        grid_spec=pltpu.PrefetchScalarGridSpec(
            num_scalar_prefetch=1,
            grid=(db,),
            in_specs=[seq(Q_C), seq(KV_C), seq(KV_C), seq(N_IDX_HEADS * IDX_DIM), seq(LANES)]
            + _page_specs(n_pages, KV_C) * 2 + _page_specs(n_pages, IDX_DIM),
            out_specs=seq(OUT_C),
            scratch_shapes=[pltpu.VMEM((ts, 1), I32)],
        ),
        out_shape=jax.ShapeDtypeStruct((db, ts, OUT_C), F32),
        compiler_params=_params(("arbitrary",)),
        name="dsa_sample",
    )(page_table.reshape(-1), q, kn, vn, qi, kiwi,
      *([ck] * n_pages), *([cv] * n_pages), *([cache_ki] * n_pages))
    return o.reshape(db * ts, OUT_C)


def _rope_tables(pos):
    half = HEAD_DIM // 2
    inv_freq = ROPE_THETA ** (-jnp.arange(half, dtype=F32) / half)
    ang = pos.astype(F32)[:, None] * inv_freq[None, :]
    c, s = jnp.cos(ang), jnp.sin(ang)
    return jnp.tile(c, (1, LANES // half)), jnp.concatenate([-s, s] * (LANES // HEAD_DIM), axis=1)


def _gain128(g):
    return jnp.tile(g.astype(F32), LANES // HEAD_DIM).reshape(1, LANES)


def _layer_a(xp, xs, shp, shs, states, params, tabs_p, tabs_s):
    norm_g, w_in, q_gain, k_gain, w_out = params
    b, t = shp
    db, ts = shs
    w16 = w_in.astype(BF16)
    args = (norm_g.reshape(1, -1), w16)
    gains = (_gain128(q_gain), _gain128(k_gain))
    qp, kp, vp, kp16, vp16, sgp = _project("a", xp, *args, *tabs_p, *gains)
    qs, ks, vs, _, _, sgs = _project("a", xs, *args, *tabs_s, *gains)
    hw = HKV_A * HEAD_DIM
    grp_p, grp_s, new_state = [], [], []
    for g, (win, dil) in enumerate(DIL_GROUPS):
        grp_p.append(_dil_prompt(qp.reshape(b, t, Q_A), kp.reshape(b, t, KV_A), vp.reshape(b, t, KV_A), g, dil))
        slabs, ns = _dil_sample(qs.reshape(db, ts, Q_A), ks.reshape(db, ts, KV_A), vs.reshape(db, ts, KV_A),
                                states[g], g, states[g].shape[1], dil)
        grp_s.append(slabs)
        wk = min(win, t)
        kg = kp.reshape(b, t, KV_A)[:, t - wk:, g * hw:(g + 1) * hw].reshape(b, wk, 1, HKV_A, HEAD_DIM)
        vg = vp.reshape(b, t, KV_A)[:, t - wk:, g * hw:(g + 1) * hw].reshape(b, wk, 1, HKV_A, HEAD_DIM)
        new_state.append(jnp.concatenate([kg, vg], axis=2))
        new_state.append(ns)
    w_out16 = w_out.astype(BF16)
    return _out_project(grp_p, sgp, xp, w_out16), _out_project(grp_s, sgs, xs, w_out16), new_state


def _layer_b(xp, xs, shp, shs, states, params, tabs_p, tabs_s, page_table):
    norm_g, w_in, q_gain, k_gain, w_out = params
    cache_k, cache_v = states
    b, t = shp
    db, ts = shs
    w16 = w_in.astype(BF16)
    args = (norm_g.reshape(1, -1), w16)
    gains = (_gain128(q_gain), _gain128(k_gain))
    qp, kp, vp, _, _, sgp, kmean, kaug16, vt16 = _project("b", xp, *args, *tabs_p, *gains, aug=True)
    qs, ks, vs, _, _, sgs = _project("b", xs, *args, *tabs_s, *gains)
    o_p = _moba_prompt(qp.reshape(b, t, Q_B), kaug16.reshape(b, t, HKV_B * LANES), vt16,
                       kmean.reshape(b, t // MOBA_BLOCK, KV_B))
    o_s = _moba_sample(qs.reshape(db, ts, Q_B), ks.reshape(db, ts, KV_B), vs.reshape(db, ts, KV_B),
                       cache_k, cache_v, page_table)
    w_out16 = w_out.astype(BF16)
    new_state = [kp.reshape(b, t, HKV_B, HEAD_DIM), ks.reshape(db, ts, HKV_B, HEAD_DIM),
                 vp.reshape(b, t, HKV_B, HEAD_DIM), vs.reshape(db, ts, HKV_B, HEAD_DIM)]
    return _out_project([[o_p]], sgp, xp, w_out16), _out_project([[o_s]], sgs, xs, w_out16), new_state


def _layer_c(xp, xs, shp, shs, states, params, tabs_p, tabs_s, page_table):
    norm_g, w_in, q_gain, k_gain, w_out = params
    cache_k, cache_v, cache_ki = states
    b, t = shp
    db, ts = shs
    w16 = jnp.pad(w_in, ((0, 0), (0, IN_C_PAD - IN_C))).astype(BF16)
    args = (norm_g.reshape(1, -1), w16)
    gains = (_gain128(q_gain), _gain128(k_gain))
    qp, kp, vp, _, _, sgp, qip, kiwip, kiwip16, kaug16, vt16 = _project("c", xp, *args, *tabs_p, *gains, aug=True)
    qs, ks, vs, _, _, sgs, qis, kiwis, _ = _project("c", xs, *args, *tabs_s, *gains)
    o_p = _dsa_prompt(qp.reshape(b, t, Q_C), kaug16.reshape(b, t, HKV_C * LANES), vt16,
                      qip.reshape(b, t, -1), kiwip.reshape(b, t, LANES), kiwip16.reshape(b, t, LANES))
    o_s = _dsa_sample(qs.reshape(db, ts, Q_C), ks.reshape(db, ts, KV_C), vs.reshape(db, ts, KV_C),
                      qis.reshape(db, ts, -1), kiwis.reshape(db, ts, LANES), cache_k, cache_v, cache_ki, page_table)
    w_out16 = w_out.astype(BF16)
    new_state = [kp.reshape(b, t, HKV_C, HEAD_DIM), ks.reshape(db, ts, HKV_C, HEAD_DIM),
                 vp.reshape(b, t, HKV_C, HEAD_DIM), vs.reshape(db, ts, HKV_C, HEAD_DIM),
                 kiwip.reshape(b, t, LANES)[:, :, :IDX_DIM], kiwis.reshape(db, ts, LANES)[:, :, :IDX_DIM]]
    return _out_project([[o_p]], sgp, xp, w_out16), _out_project([[o_s]], sgs, xs, w_out16), new_state


def kernel(x_prompt, x_sample, state_l0_kv_w128, state_l0_kv_w512, state_l0_kv_w2048, cache_l1_k, cache_l1_v, cache_l2_k, cache_l2_v, cache_l2_kidx, state_l3_kv_w128, state_l3_kv_w512, state_l3_kv_w2048, page_table, l0_norm, l0_w_in, l0_q_norm, l0_k_norm, l0_w_out, l1_norm, l1_w_in, l1_q_norm, l1_k_norm, l1_w_out, l2_norm, l2_w_in, l2_q_norm, l2_k_norm, l2_w_out, l3_norm, l3_w_in, l3_q_norm, l3_k_norm, l3_w_out):
    b, t, _ = x_prompt.shape
    db, ts, _ = x_sample.shape
    assert t % (16 * Q_TILE) == 0 and t % DSA_CHUNK == 0 and (db * ts) % PROJ_ROWS == 0 and PROJ_ROWS % ts == 0
    tabs_p = _rope_tables(jnp.arange(t))
    tabs_s = _rope_tables(PAST_LEN + (jnp.arange(PROJ_ROWS) % ts))
    xp = x_prompt.reshape(b * t, D_MODEL)
    xs = x_sample.reshape(db * ts, D_MODEL)
    states = ((state_l0_kv_w128, state_l0_kv_w512, state_l0_kv_w2048), (cache_l1_k, cache_l1_v),
              (cache_l2_k, cache_l2_v, cache_l2_kidx), (state_l3_kv_w128, state_l3_kv_w512, state_l3_kv_w2048))
    params = ((l0_norm, l0_w_in, l0_q_norm, l0_k_norm, l0_w_out), (l1_norm, l1_w_in, l1_q_norm, l1_k_norm, l1_w_out),
              (l2_norm, l2_w_in, l2_q_norm, l2_k_norm, l2_w_out), (l3_norm, l3_w_in, l3_q_norm, l3_k_norm, l3_w_out))
    new_state = []
    for i in range(4):
        common = (xp, xs, (b, t), (db, ts), states[i], params[i], tabs_p, tabs_s)
        if i % 3 == 0:
            xp, xs, st = _layer_a(*common)
        elif i % 3 == 1:
            xp, xs, st = _layer_b(*common, page_table)
        else:
            xp, xs, st = _layer_c(*common, page_table)
        new_state.extend(st)
    return (xp.reshape(b, t, D_MODEL), xs.reshape(db, ts, D_MODEL), *new_state)
```

```python
import functools

import jax
import jax.numpy as jnp
from jax import lax
from jax.experimental import pallas as pl
from jax.experimental.pallas import tpu as pltpu

F32 = jnp.float32
BF16 = jnp.bfloat16
I32 = jnp.int32

D_MODEL = 1024
PAST_LEN = 2048
PAGE_SIZE = 128
HEAD_DIM = 64
ROPE_THETA = 10000.0
NORM_EPS = 1e-6

DIL_GROUPS = ((128, 1), (512, 4), (2048, 16))
HQ_A, HKV_A = 8, 2
Q_A, KV_A, OUT_A = 1536, 384, 512

HQ_B, HKV_B = 16, 4
MOBA_BLOCK, MOBA_TOPK = 256, 3
Q_B, KV_B, OUT_B = 1024, 256, 1024

HQ_C, HKV_C = 16, 4
N_IDX_HEADS, IDX_DIM, DSA_TOPK = 8, 64, 256
Q_C, KV_C, OUT_C = 1024, 256, 1024
IN_C = Q_C + 2 * KV_C + OUT_C + N_IDX_HEADS * IDX_DIM + IDX_DIM + N_IDX_HEADS
IN_C_PAD = 3200

LANES = 128
PROJ_ROWS = 256
Q_TILE = 128
DSA_CHUNK = 512
VMEM_LIMIT = 56 * 1024 * 1024
NEG_BIG = -1e30
LOG2E = 1.4426950408889634
VT_ROWS = 80
INT_MIN = -2147483648

_NT = (((1,), (1,)), ((), ()))


def _params(sem):
    return pltpu.CompilerParams(dimension_semantics=sem, vmem_limit_bytes=VMEM_LIMIT)


def _group_sum_matrix():
    r = lax.broadcasted_iota(I32, (LANES, LANES), 0) // HEAD_DIM
    c = lax.broadcasted_iota(I32, (LANES, LANES), 1) // HEAD_DIM
    return (r == c).astype(BF16)


def _rope(y, cos, sin):
    lane = lax.broadcasted_iota(I32, y.shape, 1)
    first_half = (lane & (HEAD_DIM // 2)) == 0
    partner = jnp.where(first_half, pltpu.roll(y, LANES - HEAD_DIM // 2, 1), pltpu.roll(y, HEAD_DIM // 2, 1))
    return y * cos + partner * sin


def _head_norm(x, gain, gmat):
    ss = x * x
    hi = ss.astype(BF16)
    lo = (ss - hi.astype(F32)).astype(BF16)
    gs = jnp.dot(hi, gmat, preferred_element_type=F32) + jnp.dot(lo, gmat, preferred_element_type=F32)
    return x * lax.rsqrt(gs * (1.0 / HEAD_DIM) + NORM_EPS) * gain


def _proj_body(kind, aug_tiles, x_ref, g_ref, w_ref, cos_ref, sin_ref, qg_ref, kg_ref, *outs):
    x = x_ref[...]
    ms = jnp.mean(x * x, axis=-1, keepdims=True)
    h = (x * lax.rsqrt(ms + NORM_EPS) * g_ref[...]).astype(BF16)
    cos = cos_ref[...]
    sin = sin_ref[...]
    qg = qg_ref[...]
    kg = kg_ref[...]
    gmat = _group_sum_matrix()
    qw, kvw, gw = {"a": (Q_A, KV_A, OUT_A), "b": (Q_B, KV_B, OUT_B), "c": (Q_C, KV_C, OUT_C)}[kind]
    q_ref, k_ref, v_ref, k16_ref, v16_ref, sg_ref = outs[:6]
    rest = outs[6:]
    moba = aug_tiles and kind == "b"
    if aug_tiles:
        kaug_ref, vt_ref = rest[-2:]

    def seg(start, width):
        return jnp.dot(h, w_ref[:, start:start + width], preferred_element_type=F32)

    zq = seg(0, qw)
    for c in range(qw // LANES):
        y = _rope(_head_norm(zq[:, c * LANES:(c + 1) * LANES], qg, gmat), cos, sin)
        q_ref[:, c * LANES:(c + 1) * LANES] = y * (HEAD_DIM ** -0.5)
    zk = seg(qw, kvw)
    ksum = []
    for c in range(kvw // LANES):
        y = _rope(_head_norm(zk[:, c * LANES:(c + 1) * LANES], kg, gmat), cos, sin)
        k_ref[:, c * LANES:(c + 1) * LANES] = y
        k16_ref[:, c * LANES:(c + 1) * LANES] = y.astype(BF16)
        if moba:
            ksum.append(jnp.sum(y, axis=0, keepdims=True) * (1.0 / MOBA_BLOCK))
        if aug_tiles:
            lane = lax.broadcasted_iota(I32, y.shape, 1)
            hot = HEAD_DIM + pl.program_id(0) % aug_tiles if moba else -1
            onehot = (lane == hot).astype(F32)
            kaug_ref[:, 2 * c * LANES:(2 * c + 1) * LANES] = jnp.where(lane < HEAD_DIM, y, onehot).astype(BF16)
            kaug_ref[:, (2 * c + 1) * LANES:(2 * c + 2) * LANES] = jnp.where(
                lane < HEAD_DIM, pltpu.roll(y, HEAD_DIM, 1), onehot).astype(BF16)
    zv = seg(qw + kvw, kvw)
    v_ref[...] = zv
    v16_ref[...] = zv.astype(BF16)
    if aug_tiles:
        zvt = zv.T
        tail = (lax.broadcasted_iota(I32, (VT_ROWS - HEAD_DIM, zvt.shape[1]), 0) == 0).astype(BF16)
        for kvh in range(kvw // HEAD_DIM):
            vt_ref[0, kvh * VT_ROWS:kvh * VT_ROWS + HEAD_DIM, :] = zvt[kvh * HEAD_DIM:(kvh + 1) * HEAD_DIM].astype(BF16)
            vt_ref[0, kvh * VT_ROWS + HEAD_DIM:(kvh + 1) * VT_ROWS, :] = tail
    zg = seg(qw + 2 * kvw, gw)
    sg_ref[...] = zg / (1.0 + jnp.exp(-zg))
    nrest = 0
    if kind == "c":
        qi_ref, kiwi_ref, kiwi16_ref = rest[:3]
        nrest = 3
        base = qw + 2 * kvw + gw
        zi = seg(base, N_IDX_HEADS * IDX_DIM)
        for c in range(N_IDX_HEADS * IDX_DIM // LANES):
            qi_ref[:, c * LANES:(c + 1) * LANES] = _rope(zi[:, c * LANES:(c + 1) * LANES], cos, sin)
        zz = seg(base + N_IDX_HEADS * IDX_DIM, LANES)
        lane = lax.broadcasted_iota(I32, zz.shape, 1)
        kiwi = jnp.where(lane < IDX_DIM, _rope(zz, cos, sin), zz)
        kiwi_ref[...] = kiwi
        kiwi16_ref[...] = kiwi.astype(BF16)
    if moba:
        km_ref = rest[0]
        for c in range(kvw // LANES):
            km_ref[0, :, c * LANES:(c + 1) * LANES] = ksum[c]


def _project(kind, x, gain, w16, cos, sin, qg, kg, aug=False):
    n = x.shape[0]
    tm = PROJ_ROWS
    nt = cos.shape[0] // tm
    qw, kvw, gw = {"a": (Q_A, KV_A, OUT_A), "b": (Q_B, KV_B, OUT_B), "c": (Q_C, KV_C, OUT_C)}[kind]
    wp = w16.shape[1]

    def rows(width):
        return pl.BlockSpec((tm, width), lambda i: (i, 0))

    def const(shape):
        return pl.BlockSpec(shape, lambda i: (0,) * len(shape))

    out_shape = [jax.ShapeDtypeStruct((n, qw), F32), jax.ShapeDtypeStruct((n, kvw), F32),
                 jax.ShapeDtypeStruct((n, kvw), F32), jax.ShapeDtypeStruct((n, kvw), BF16),
                 jax.ShapeDtypeStruct((n, kvw), BF16), jax.ShapeDtypeStruct((n, gw), F32)]
    out_specs = [rows(qw), rows(kvw), rows(kvw), rows(kvw), rows(kvw), rows(gw)]
    if kind == "c":
        out_shape += [jax.ShapeDtypeStruct((n, N_IDX_HEADS * IDX_DIM), F32),
                      jax.ShapeDtypeStruct((n, LANES), F32), jax.ShapeDtypeStruct((n, LANES), BF16)]
        out_specs += [rows(N_IDX_HEADS * IDX_DIM), rows(LANES), rows(LANES)]
    if aug:
        n_kv = kvw // HEAD_DIM
        if kind == "b":
            out_shape.append(jax.ShapeDtypeStruct((n // tm, 1, kvw), F32))
            out_specs.append(pl.BlockSpec((1, 1, kvw), lambda i: (i, 0, 0)))
        out_shape += [jax.ShapeDtypeStruct((n, n_kv * LANES), BF16),
                      jax.ShapeDtypeStruct((n // (nt * tm), n_kv * VT_ROWS, nt * tm), BF16)]
        out_specs += [rows(n_kv * LANES), pl.BlockSpec((1, n_kv * VT_ROWS, tm), lambda i: (i // nt, 0, i % nt))]
    return pl.pallas_call(
        functools.partial(_proj_body, kind, nt if aug else 0),
        grid=(n // tm,),
        in_specs=[rows(D_MODEL), const((1, D_MODEL)), const((D_MODEL, wp)),
                  pl.BlockSpec((tm, LANES), lambda i: (i % nt, 0)),
                  pl.BlockSpec((tm, LANES), lambda i: (i % nt, 0)),
                  const((1, LANES)), const((1, LANES))],
        out_specs=out_specs,
        out_shape=out_shape,
        compiler_params=_params(("parallel",)),
        name=f"proj_{kind}",
    )(x, gain, w16, cos, sin, qg, kg)


LSE_REP = LANES // HQ_A
O_SLABS = OUT_A // LANES


def _outproj_body(n_groups, *refs):
    per = O_SLABS + 1
    sg_ref, x_ref, w_ref, y_ref = refs[-4:]
    if n_groups > 1:
        lses = [refs[g * per + O_SLABS][...] for g in range(n_groups)]
        m = functools.reduce(jnp.maximum, lses)
        es = [jnp.exp(l - m) for l in lses]
        tot = functools.reduce(lambda a, b: a + b, es)
        lane = lax.broadcasted_iota(I32, m.shape, 1)
        slabs = []
        for c in range(O_SLABS):
            acc = None
            for g in range(n_groups):
                wg = es[g] / tot
                lo = wg[:, 2 * c * LSE_REP:2 * c * LSE_REP + 1]
                hi = wg[:, (2 * c + 1) * LSE_REP:(2 * c + 1) * LSE_REP + 1]
                term = jnp.where(lane < HEAD_DIM, lo, hi) * refs[g * per + c][...]
                acc = term if acc is None else acc + term
            slabs.append(acc)
        o = jnp.concatenate(slabs, axis=1)
    else:
        o = refs[0][...]
    g16 = (o * sg_ref[...]).astype(BF16)
    y_ref[...] = x_ref[...] + jnp.dot(g16, w_ref[...], preferred_element_type=F32)


def _out_project(groups, sg, x, w16):
    n = x.shape[0]
    f = sg.shape[1]
    tm = PROJ_ROWS
    ins = [a for grp in groups for a in grp]
    in_specs = [pl.BlockSpec((tm, a.shape[1]), lambda i: (i, 0)) for a in ins]
    in_specs += [pl.BlockSpec((tm, f), lambda i: (i, 0)), pl.BlockSpec((tm, D_MODEL), lambda i: (i, 0)),
                 pl.BlockSpec((f, D_MODEL), lambda i: (0, 0))]
    return pl.pallas_call(
        functools.partial(_outproj_body, len(groups)),
        grid=(n // tm,),
        in_specs=in_specs,
        out_specs=pl.BlockSpec((tm, D_MODEL), lambda i: (i, 0)),
        out_shape=jax.ShapeDtypeStruct((n, D_MODEL), F32),
        compiler_params=_params(("parallel",)),
        name="out_proj_merge" if len(groups) > 1 else "out_proj",
    )(*ins, sg, x, w16)


def _stack_heads(q, kvh, rep):
    return jnp.concatenate([q[:, (kvh * rep + r) * HEAD_DIM:(kvh * rep + r + 1) * HEAD_DIM] for r in range(rep)],
                           axis=0)


FLASH_BLOCK = 256


def _flash_pair(k_ref, v_ref, qas, carry, n, bias_ref):
    carry = list(carry)
    units = [(j, kvh) for j in range(2) for kvh in range(len(qas))]
    offs = [pl.multiple_of((2 * n + j) * FLASH_BLOCK, FLASH_BLOCK) for j in range(2)]
    rep = qas[0].shape[1] // Q_TILE
    scores = []
    for j, kvh in units:
        st = jnp.dot(k_ref[0, pl.ds(offs[j], FLASH_BLOCK), kvh * LANES:(kvh + 1) * LANES], qas[kvh],
                     preferred_element_type=F32)
        if bias_ref is not None:
            st = st + jnp.concatenate([bias_ref[pl.ds(offs[j], FLASH_BLOCK), :]] * rep, axis=1)
        scores.append(st)
    for (j, kvh), st in zip(units, scores):
        m, acc = carry[kvh]
        m_new = jnp.maximum(m, jnp.max(st, axis=0, keepdims=True))
        p = jnp.exp2(st - m_new)
        pv = jnp.dot(v_ref[0, kvh * VT_ROWS:(kvh + 1) * VT_ROWS, pl.ds(offs[j], FLASH_BLOCK)], p.astype(BF16),
                     preferred_element_type=F32)
        carry[kvh] = (m_new, jnp.exp2(m - m_new) * acc + pv)
    return tuple(carry)


def _lse_slab(lses, rows):
    return jnp.concatenate([jnp.broadcast_to(l, (rows, LSE_REP)) for l in lses], axis=1)


def _dil_prompt_body(dil, *refs):
    q_refs = refs[:O_SLABS]
    kp_ref, kc_ref, vp_ref, vc_ref = refs[O_SLABS:O_SLABS + 4]
    o_refs = refs[O_SLABS + 4:2 * O_SLABS + 4]
    lse_ref = refs[2 * O_SLABS + 4]
    j = pl.program_id(1)
    tq = Q_TILE
    rep = HQ_A // HKV_A
    row = lax.broadcasted_iota(I32, (tq, 2 * tq), 0)
    col = lax.broadcasted_iota(I32, (tq, 2 * tq), 1)
    dist = row + tq - col
    valid = (dist >= 0) & (dist <= tq) & ((col >= tq) | (j > 0))
    valid = jnp.concatenate([valid] * rep, axis=0)

    def residue(r, _):
        rows = pl.ds(r, tq, stride=dil) if dil > 1 else pl.ds(0, tq)
        q = jnp.concatenate([qr[0, rows, :] for qr in q_refs], axis=1).astype(BF16)
        kk = jnp.concatenate([kp_ref[0, rows, :], kc_ref[0, rows, :]], axis=0).astype(BF16)
        vv = jnp.concatenate([vp_ref[0, rows, :], vc_ref[0, rows, :]], axis=0).astype(BF16)
        outs, lses = [], []
        for kvh in range(HKV_A):
            kh = kk[:, kvh * HEAD_DIM:(kvh + 1) * HEAD_DIM]
            vh = vv[:, kvh * HEAD_DIM:(kvh + 1) * HEAD_DIM]
            qh = _stack_heads(q, kvh, rep)
            s = lax.dot_general(qh, kh, _NT, preferred_element_type=F32)
            s = jnp.where(valid, s, -jnp.inf)
            m = jnp.max(s, axis=1, keepdims=True)
            p = jnp.exp(s - m)
            l = jnp.sum(p, axis=1, keepdims=True)
            o = jnp.dot(p.astype(BF16), vh, preferred_element_type=F32) / l
            lse = m + jnp.log(l)
            for h in range(rep):
                outs.append(o[h * tq:(h + 1) * tq])
                lses.append(lse[h * tq:(h + 1) * tq])
        for c in range(O_SLABS):
            o_refs[c][0, rows, :] = jnp.concatenate(outs[2 * c:2 * c + 2], axis=1)
        lse_ref[0, rows, :] = _lse_slab(lses, tq)
        return 0

    if dil == 1:
        residue(0, 0)
    else:
        lax.fori_loop(0, dil, residue, 0)


def _dil_prompt(q, k, v, g, dil):
    b, t, _ = q.shape
    blk = Q_TILE * dil
    hw = HKV_A * HEAD_DIM
    cur = lambda bb, j: (bb, j, g)
    prev = lambda bb, j: (bb, jnp.maximum(j - 1, 0), g)
    slab = pl.BlockSpec((1, blk, LANES), lambda bb, j: (bb, j, 0))
    outs = pl.pallas_call(
        functools.partial(_dil_prompt_body, dil),
        grid=(b, t // blk),
        in_specs=[pl.BlockSpec((1, blk, LANES), functools.partial(lambda c, bb, j: (bb, j, g * O_SLABS + c), c))
                  for c in range(O_SLABS)]
        + [pl.BlockSpec((1, blk, hw), prev), pl.BlockSpec((1, blk, hw), cur),
           pl.BlockSpec((1, blk, hw), prev), pl.BlockSpec((1, blk, hw), cur)],
        out_specs=[slab] * (O_SLABS + 1),
        out_shape=[jax.ShapeDtypeStruct((b, t, LANES), F32)] * (O_SLABS + 1),
        compiler_params=_params(("parallel", "arbitrary")),
        name=f"dil_prompt_g{g}",
    )(*([q] * O_SLABS), k, k, v, v)
    return [a.reshape(b * t, LANES) for a in outs]


def _dil_sample_body(win, dil, sb, q_ref, kn_ref, vn_ref, st_ref, *out_refs):
    o_refs = out_refs[:O_SLABS]
    lse_ref, ns_ref = out_refs[O_SLABS:]
    rep = HQ_A // HKV_A
    hw = HKV_A * HEAD_DIM
    ts = q_ref.shape[1]
    row = lax.broadcasted_iota(I32, (rep * ts, win), 0)
    col = lax.broadcasted_iota(I32, (rep * ts, win), 1)
    dist = win + (row & (ts - 1)) - col
    valid_c = (dist <= win) & ((dist & (dil - 1)) == 0)
    rown = lax.broadcasted_iota(I32, (rep * ts, LANES), 0)
    coln = lax.broadcasted_iota(I32, (rep * ts, LANES), 1)
    distn = (rown & (ts - 1)) - coln
    valid_n = (distn >= 0) & ((distn & (dil - 1)) == 0)
    lane_w = lax.broadcasted_iota(I32, (2 * hw, LANES), 1)
    for s_i in range(sb):
        st = st_ref[s_i]
        knp = _pad_rows(kn_ref[s_i], LANES)
        vnp = _pad_rows(vn_ref[s_i], LANES)
        shifted = pltpu.roll(st, win - ts, 1)
        newcols = pltpu.roll(jnp.concatenate([knp.T, vnp.T], axis=0), LANES - ts, 1)
        if win > LANES:
            ns_ref[s_i, :, 0:win - LANES] = shifted[:, 0:win - LANES]
        ns_ref[s_i, :, win - LANES:win] = jnp.where(lane_w >= LANES - ts, newcols, shifted[:, win - LANES:win])
        q = q_ref[s_i].astype(BF16)
        kn16 = knp.astype(BF16)
        vn16 = vnp.astype(BF16)
        outs, lses = [], []
        for kvh in range(HKV_A):
            hs = slice(kvh * HEAD_DIM, (kvh + 1) * HEAD_DIM)
            kt = st[kvh * HEAD_DIM:(kvh + 1) * HEAD_DIM].astype(BF16)
            vt = st[hw + kvh * HEAD_DIM:hw + (kvh + 1) * HEAD_DIM].astype(BF16)
            qh = _stack_heads(q, kvh, rep)
            s_c = jnp.where(valid_c, jnp.dot(qh, kt, preferred_element_type=F32), -jnp.inf)
            s_n = jnp.where(valid_n, lax.dot_general(qh, kn16[:, hs], _NT, preferred_element_type=F32), -jnp.inf)
            m = jnp.maximum(jnp.max(s_c, axis=1, keepdims=True), jnp.max(s_n, axis=1, keepdims=True))
            p_c = jnp.exp(s_c - m)
            p_n = jnp.exp(s_n - m)
            l = jnp.sum(p_c, axis=1, keepdims=True) + jnp.sum(p_n, axis=1, keepdims=True)
            o = (lax.dot_general(p_c.astype(BF16), vt, _NT, preferred_element_type=F32)
                 + jnp.dot(p_n.astype(BF16), vn16[:, hs], preferred_element_type=F32)) / l
            lse = m + jnp.log(l)
            for r in range(rep):
                outs.append(o[r * ts:(r + 1) * ts])
                lses.append(lse[r * ts:(r + 1) * ts])
        for c in range(O_SLABS):
            o_refs[c][s_i] = jnp.concatenate(outs[2 * c:2 * c + 2], axis=1)
        lse_ref[s_i] = _lse_slab(lses, ts)


def _dil_sample(q, kn, vn, state, g, win, dil):
    db, ts, _ = q.shape
    hw = HKV_A * HEAD_DIM
    st = state.transpose(0, 2, 3, 4, 1).reshape(db, 2 * hw, win)
    sb = max(1, min(8, 1024 // win))
    slab = pl.BlockSpec((sb, ts, LANES), lambda i: (i, 0, 0))
    *slabs, ns = pl.pallas_call(
        functools.partial(_dil_sample_body, win, dil, sb),
        grid=(db // sb,),
        in_specs=[pl.BlockSpec((sb, ts, OUT_A), lambda i: (i, 0, g)),
                  pl.BlockSpec((sb, ts, hw), lambda i: (i, 0, g)),
                  pl.BlockSpec((sb, ts, hw), lambda i: (i, 0, g)),
                  pl.BlockSpec((sb, 2 * hw, win), lambda i: (i, 0, 0))],
        out_specs=[slab] * (O_SLABS + 1) + [pl.BlockSpec((sb, 2 * hw, win), lambda i: (i, 0, 0))],
        out_shape=[jax.ShapeDtypeStruct((db, ts, LANES), F32)] * (O_SLABS + 1)
        + [jax.ShapeDtypeStruct((db, 2 * hw, win), F32)],
        compiler_params=_params(("parallel",)),
        name=f"dil_sample_g{g}",
    )(q, kn, vn, st)
    ns = ns.reshape(db, 2, HKV_A, HEAD_DIM, win).transpose(0, 4, 1, 2, 3)
    return [a.reshape(db * ts, LANES) for a in slabs], ns


def _top_blocks(bs, n_valid, topk):
    lane = lax.broadcasted_iota(I32, bs.shape, 1)
    nb = bs.shape[1]
    work = jnp.where(lane < n_valid, bs, -jnp.inf)
    chosen = jnp.zeros(bs.shape, jnp.bool_)
    for _ in range(topk):
        m = jnp.max(work, axis=1, keepdims=True)
        idx = jnp.min(jnp.where(work == m, lane, nb), axis=1, keepdims=True)
        pick = lane == idx
        chosen = chosen | pick
        work = jnp.where(pick, -jnp.inf, work)
    return chosen & (lane < n_valid)


def _top_blocks_t(bs, n_valid, topk):
    row = lax.broadcasted_iota(I32, bs.shape, 0)
    rowf = row.astype(F32)
    work = jnp.where(row < n_valid, bs, -jnp.inf)
    chosen = jnp.zeros(bs.shape, jnp.bool_)
    for _ in range(topk):
        m = jnp.max(work, axis=0, keepdims=True)
        idx = jnp.min(jnp.where(work == m, rowf, float(bs.shape[0])), axis=0, keepdims=True)
        pick = rowf == idx
        chosen = chosen | pick
        work = jnp.where(pick, -jnp.inf, work)
    return chosen & (row < n_valid)


def _moba_prompt_body(q_ref, k_ref, v_ref, km_ref, o_ref):
    i = pl.program_id(1)
    tq = Q_TILE
    rep = HQ_B // HKV_B
    nb = km_ref.shape[1]
    nr = rep * tq
    own = (i * tq) // MOBA_BLOCK
    own0 = pl.multiple_of(own * MOBA_BLOCK, MOBA_BLOCK)
    qt = q_ref[0].T
    km = km_ref[0]
    krow = lax.broadcasted_iota(I32, (MOBA_BLOCK, nr), 0)
    qcol = lax.broadcasted_iota(I32, (MOBA_BLOCK, nr), 1)
    causal = own0 + krow <= i * tq + (qcol & (tq - 1))
    brow = lax.broadcasted_iota(I32, (HEAD_DIM, nr), 0)
    qas_past, init = [], []
    for kvh in range(HKV_B):
        hs = slice(kvh * HEAD_DIM, (kvh + 1) * HEAD_DIM)
        ls = slice(kvh * LANES, (kvh + 1) * LANES)
        vs = slice(kvh * VT_ROWS, (kvh + 1) * VT_ROWS)
        q4t = jnp.concatenate([qt[(kvh * rep + r) * HEAD_DIM:(kvh * rep + r + 1) * HEAD_DIM] for r in range(rep)],
                              axis=1)
        bst = jnp.dot(km[:, hs], q4t, precision=lax.Precision.HIGHEST, preferred_element_type=F32)
        if nb < HEAD_DIM:
            bst = jnp.concatenate([bst, jnp.zeros((HEAD_DIM - nb, nr), F32)], axis=0)
        chosen = _top_blocks_t(bst, own, MOBA_TOPK)
        qs = q4t * LOG2E
        qas_past.append(jnp.concatenate([qs, jnp.where(chosen, 0.0, NEG_BIG)], axis=0).astype(BF16))
        qa = jnp.concatenate([qs, jnp.where(brow == own, 0.0, NEG_BIG)], axis=0).astype(BF16)
        st = jnp.dot(k_ref[0, pl.ds(own0, MOBA_BLOCK), ls], qa, preferred_element_type=F32)
        st = jnp.where(causal, st, -jnp.inf)
        m = jnp.max(st, axis=0, keepdims=True)
        p = jnp.exp2(st - m)
        init.append((m, jnp.dot(v_ref[0, vs, pl.ds(own0, MOBA_BLOCK)], p.astype(BF16), preferred_element_type=F32)))

    res = lax.fori_loop(0, (own + 1) // 2,
                        lambda n, carry: _flash_pair(k_ref, v_ref, qas_past, carry, n, None), tuple(init))
    heads = []
    for kvh in range(HKV_B):
        acc = res[kvh][1]
        ot = acc[0:HEAD_DIM] / acc[HEAD_DIM:HEAD_DIM + 1]
        heads += [ot[:, r * tq:(r + 1) * tq] for r in range(rep)]
    o_ref[0] = jnp.concatenate(heads, axis=0).T


def _moba_prompt(q, kaug16, vt16, kmean):
    b, t, _ = q.shape
    nb = t // MOBA_BLOCK
    o = pl.pallas_call(
        _moba_prompt_body,
        grid=(b, t // Q_TILE),
        in_specs=[pl.BlockSpec((1, Q_TILE, Q_B), lambda bb, i: (bb, i, 0)),
                  pl.BlockSpec((1, t, HKV_B * LANES), lambda bb, i: (bb, 0, 0)),
                  pl.BlockSpec((1, HKV_B * VT_ROWS, t), lambda bb, i: (bb, 0, 0)),
                  pl.BlockSpec((1, nb, KV_B), lambda bb, i: (bb, 0, 0))],
        out_specs=pl.BlockSpec((1, Q_TILE, OUT_B), lambda bb, i: (bb, i, 0)),
        out_shape=jax.ShapeDtypeStruct((b, t, OUT_B), F32),
        compiler_params=_params(("parallel", "arbitrary")),
        name="moba_prompt",
    )(q, kaug16, vt16, kmean)
    return o.reshape(b * t, OUT_B)


def _block_diag_queries(q, n_kv, rep):
    ts = q.shape[0]
    rows = []
    for kvh in range(n_kv):
        for r in range(rep):
            hd = (kvh * rep + r) * HEAD_DIM
            parts = []
            if kvh > 0:
                parts.append(jnp.zeros((ts, kvh * HEAD_DIM), q.dtype))
            parts.append(q[:, hd:hd + HEAD_DIM])
            if kvh < n_kv - 1:
                parts.append(jnp.zeros((ts, (n_kv - 1 - kvh) * HEAD_DIM), q.dtype))
            rows.append(jnp.concatenate(parts, axis=1))
    return jnp.concatenate(rows, axis=0)


def _unstack_block_diag(o_all, n_kv, rep, ts):
    parts = []
    for kvh in range(n_kv):
        for r in range(rep):
            r0 = (kvh * rep + r) * ts
            parts.append(o_all[r0:r0 + ts, kvh * HEAD_DIM:(kvh + 1) * HEAD_DIM])
    return jnp.concatenate(parts, axis=1)


def _pad_rows(x, rows):
    return jnp.concatenate([x, jnp.zeros((rows - x.shape[0], x.shape[1]), x.dtype)], axis=0)


SAMPLE_SEQS = 2


def _moba_sample_body(n_pages, pt_ref, q_ref, kn_ref, vn_ref, *refs):
    sb = q_ref.shape[0]
    o_ref = refs[2 * sb * n_pages]
    for s_i in range(sb):
        kp = refs[s_i * n_pages:(s_i + 1) * n_pages]
        vp = refs[(sb + s_i) * n_pages:(sb + s_i + 1) * n_pages]
        o_ref[s_i] = _moba_sample_one(q_ref[s_i], kn_ref[s_i], vn_ref[s_i], kp, vp)


def _moba_sample_one(q, kn, vn, kp, vp):
    n_pages = len(kp)
    rep = HQ_B // HKV_B
    ts = q.shape[0]
    nrows = HQ_B * ts
    ppb = MOBA_BLOCK // PAGE_SIZE
    n_past = n_pages // ppb
    qf = _block_diag_queries(q, HKV_B, rep)
    qf16 = qf.astype(BF16)
    lane_k = lax.broadcasted_iota(I32, (KV_B, LANES), 1)
    kmt = jnp.zeros((KV_B, LANES), F32)
    for n in range(n_past):
        tot = kp[n * ppb][0]
        for j in range(1, ppb):
            tot = tot + kp[n * ppb + j][0]
        kmt = jnp.where(lane_k == n, jnp.sum(tot, axis=1, keepdims=True) * (1.0 / MOBA_BLOCK), kmt)
    bs = jnp.dot(qf, kmt, precision=lax.Precision.HIGHEST, preferred_element_type=F32)
    chosen = _top_blocks(bs, n_past, MOBA_TOPK)
    selb = jnp.where(chosen, 0.0, NEG_BIG)
    bias = jnp.concatenate([jnp.broadcast_to(selb[:, n:n + 1], (nrows, MOBA_BLOCK)) for n in range(n_past)], axis=1)
    s_c = jnp.concatenate([jnp.dot(qf16, kp[p][0].astype(BF16), preferred_element_type=F32)
                           for p in range(n_pages)], axis=1) + bias
    kn16 = _pad_rows(kn, LANES).astype(BF16)
    vn16 = _pad_rows(vn, LANES).astype(BF16)
    s_n = lax.dot_general(qf16, kn16, _NT, preferred_element_type=F32)
    row = lax.broadcasted_iota(I32, (nrows, LANES), 0)
    col = lax.broadcasted_iota(I32, (nrows, LANES), 1)
    s_n = jnp.where(col <= (row & (ts - 1)), s_n, -jnp.inf)
    m = jnp.maximum(jnp.max(s_c, axis=1, keepdims=True), jnp.max(s_n, axis=1, keepdims=True))
    p_c = jnp.exp(s_c - m)
    p_n = jnp.exp(s_n - m)
    l = jnp.sum(p_c, axis=1, keepdims=True) + jnp.sum(p_n, axis=1, keepdims=True)
    acc = jnp.dot(p_n.astype(BF16), vn16, preferred_element_type=F32)
    p16 = p_c.astype(BF16)
    for p in range(n_pages):
        acc = acc + lax.dot_general(p16[:, p * PAGE_SIZE:(p + 1) * PAGE_SIZE], vp[p][0].astype(BF16), _NT,
                                    preferred_element_type=F32)
    return _unstack_block_diag(acc / l, HKV_B, rep, ts)


def _page_specs(n_pages, width, sb):
    return [pl.BlockSpec((1, width, PAGE_SIZE),
                         functools.partial(lambda s, p, i, pt: (pt[(i * sb + s) * n_pages + p], 0, 0), s, p))
            for s in range(sb) for p in range(n_pages)]


def _pages_minor(cache):
    n_pool = cache.shape[0]
    return jnp.moveaxis(cache, 1, -1).reshape(n_pool, -1, PAGE_SIZE)


def _moba_sample(q, kn, vn, cache_k, cache_v, page_table):
    db, ts, _ = q.shape
    n_pages = page_table.shape[1]
    ck = _pages_minor(cache_k)
    cv = _pages_minor(cache_v)
    sb = SAMPLE_SEQS
    seq = lambda w: pl.BlockSpec((sb, ts, w), lambda i, pt: (i, 0, 0))
    o = pl.pallas_call(
        functools.partial(_moba_sample_body, n_pages),
        grid_spec=pltpu.PrefetchScalarGridSpec(
            num_scalar_prefetch=1,
            grid=(db // sb,),
            in_specs=[seq(Q_B), seq(KV_B), seq(KV_B)] + _page_specs(n_pages, KV_B, sb) * 2,
            out_specs=seq(OUT_B),
        ),
        out_shape=jax.ShapeDtypeStruct((db, ts, OUT_B), F32),
        compiler_params=_params(("arbitrary",)),
        name="moba_sample",
    )(page_table.reshape(-1), q, kn, vn, *([ck] * (sb * n_pages)), *([cv] * (sb * n_pages)))
    return o.reshape(db * ts, OUT_B)


def _code_to_float(t):
    return pltpu.bitcast(jnp.where(t >= 0, t, t ^ 0x7FFFFFFF), F32)


def _threshold_of(t):
    real = t > INT_MIN
    return jnp.where(real, _code_to_float(t), -jnp.inf), real


def _kth_largest1(count_ge, shape, k):
    def step(b, t):
        cand = jnp.where(b == 0, jnp.zeros_like(t), t | jnp.left_shift(jnp.int32(1), 31 - b))
        return jnp.where(count_ge([_code_to_float(cand)])[0] >= k, cand, t)

    return _threshold_of(lax.fori_loop(0, 32, step, jnp.full(shape, INT_MIN, I32)))


def _kth_largest2(count_ge, shape, k):
    def pick(t, cands):
        for cd, cnt in zip(cands, count_ge([_code_to_float(cd) for cd in cands])):
            t = jnp.where(cnt >= k, cd, t)
        return t

    t = pick(jnp.full(shape, INT_MIN, I32),
             tuple(jnp.full(shape, v, I32) for v in (INT_MIN + (1 << 30), 0, 1 << 30)))

    def step(b, t):
        sh = 28 - 2 * b
        return pick(t, tuple(t | jnp.left_shift(jnp.int32(j), sh) for j in (1, 2, 3)))

    return _threshold_of(lax.fori_loop(0, 15, step, t))


def _index_scores(dots, wv, rows):
    sc = jnp.maximum(dots[0:rows], 0.0) * wv[:, 0:1]
    for h in range(1, N_IDX_HEADS):
        sc = sc + jnp.maximum(dots[h * rows:(h + 1) * rows], 0.0) * wv[:, h:h + 1]
    return sc


def _dsa_prompt_body(q_ref, k_ref, v_ref, qi_ref, w_ref, ki_ref, o_ref, key_ref, bias_ref, j_ref):
    i = pl.program_id(1)
    tq = Q_TILE
    ck = DSA_CHUNK
    rep = HQ_C // HKV_C
    t_all = key_ref.shape[0]
    nch = ((i + 1) * tq + ck - 1) // ck
    qpos = i * tq + lax.broadcasted_iota(I32, (1, tq), 1)
    qit = qi_ref[0].T
    qi8t = jnp.concatenate([qit[h * IDX_DIM:(h + 1) * IDX_DIM] for h in range(N_IDX_HEADS)], axis=1)
    qi8t = jnp.concatenate([qi8t, jnp.zeros_like(qi8t)], axis=0).astype(BF16)
    wt = w_ref[0].T[IDX_DIM:IDX_DIM + N_IDX_HEADS] * (IDX_DIM ** -0.5 * N_IDX_HEADS ** -0.5)
    row_c = lax.broadcasted_iota(I32, (ck, tq), 0)

    def scores(c, _):
        off = pl.multiple_of(c * ck, ck)
        kic = ki_ref[0, pl.ds(off, ck), :]
        sc = None
        for hp in range(N_IDX_HEADS // 2):
            d2 = jnp.dot(kic, qi8t[:, 2 * hp * tq:(2 * hp + 2) * tq], preferred_element_type=F32)
            for j in range(2):
                h = 2 * hp + j
                term = jnp.maximum(d2[:, j * tq:(j + 1) * tq], 0.0) * wt[h:h + 1]
                sc = term if sc is None else sc + term
        key_ref[pl.ds(off, ck), :] = jnp.where(off + row_c <= qpos, sc + 0.0, -jnp.inf)
        return 0

    lax.fori_loop(0, nch, scores, 0)

    def count(preds):
        def body(c, accs):
            off = pl.multiple_of(c * ck, ck)
            kx = key_ref[pl.ds(off, ck), :]
            idx = off + row_c
            return tuple(a + jnp.sum(p(kx, idx).astype(I32).reshape(ck // 8, 8, tq), axis=0)
                         for a, p in zip(accs, preds))
        accs = lax.fori_loop(0, nch, body, tuple(jnp.zeros((8, tq), I32) for _ in preds))
        return [jnp.sum(a, axis=0, keepdims=True) for a in accs]

    thr, real = _kth_largest1(
        lambda cands: count([functools.partial(lambda cd, kx, idx: kx >= cd, cd) for cd in cands]), (1, tq), DSA_TOPK)
    n_gt, n_tie = count([lambda kx, idx: kx > thr, lambda kx, idx: (kx == thr) & real])
    need = DSA_TOPK - n_gt
    j_ref[...] = jnp.full((1, tq), t_all, I32)

    @pl.when(jnp.max(jnp.where(n_tie > need, 1, 0)) > 0)
    def _():
        def step(b, jc):
            cand = jc + jnp.left_shift(jnp.int32(1), 13 - b)
            below, = count([lambda kx, idx: (kx == thr) & real & (idx < cand)])
            return jnp.where(below < need, cand, jc)
        j_ref[...] = lax.fori_loop(0, 14, step, jnp.zeros((1, tq), I32))

    jmax = j_ref[...]

    def write_bias(c, _):
        off = pl.multiple_of(c * ck, ck)
        kx = key_ref[pl.ds(off, ck), :]
        keep = (kx > thr) | ((kx == thr) & real & (off + row_c <= jmax))
        bias_ref[pl.ds(off, ck), :] = jnp.where(keep, 0.0, NEG_BIG)
        return 0

    lax.fori_loop(0, nch, write_bias, 0)

    assert ck == 2 * FLASH_BLOCK
    nr = rep * tq
    qt = q_ref[0].T
    qas = []
    for kvh in range(HKV_C):
        q4t = jnp.concatenate([qt[(kvh * rep + r) * HEAD_DIM:(kvh * rep + r + 1) * HEAD_DIM] for r in range(rep)],
                              axis=1)
        qas.append(jnp.concatenate([q4t * LOG2E, jnp.zeros((HEAD_DIM, nr), F32)], axis=0).astype(BF16))

    init = tuple((jnp.full((1, nr), NEG_BIG, F32), jnp.zeros((VT_ROWS, nr), F32)) for _ in range(HKV_C))
    res = lax.fori_loop(0, nch, lambda c, carry: _flash_pair(k_ref, v_ref, qas, carry, c, bias_ref), init)
    heads = []
    for kvh in range(HKV_C):
        acc = res[kvh][1]
        ot = acc[0:HEAD_DIM] / acc[HEAD_DIM:HEAD_DIM + 1]
        heads += [ot[:, r * tq:(r + 1) * tq] for r in range(rep)]
    o_ref[0] = jnp.concatenate(heads, axis=0).T


def _dsa_prompt(q, kaug16, vt16, qi, kiwi, kiwi16):
    b, t, _ = q.shape
    o = pl.pallas_call(
        _dsa_prompt_body,
        grid=(b, t // Q_TILE),
        in_specs=[pl.BlockSpec((1, Q_TILE, Q_C), lambda bb, i: (bb, i, 0)),
                  pl.BlockSpec((1, t, HKV_C * LANES), lambda bb, i: (bb, 0, 0)),
                  pl.BlockSpec((1, HKV_C * VT_ROWS, t), lambda bb, i: (bb, 0, 0)),
                  pl.BlockSpec((1, Q_TILE, N_IDX_HEADS * IDX_DIM), lambda bb, i: (bb, i, 0)),
                  pl.BlockSpec((1, Q_TILE, LANES), lambda bb, i: (bb, i, 0)),
                  pl.BlockSpec((1, t, LANES), lambda bb, i: (bb, 0, 0))],
        out_specs=pl.BlockSpec((1, Q_TILE, OUT_C), lambda bb, i: (bb, i, 0)),
        out_shape=jax.ShapeDtypeStruct((b, t, OUT_C), F32),
        scratch_shapes=[pltpu.VMEM((t, Q_TILE), F32), pltpu.VMEM((t, Q_TILE), F32), pltpu.VMEM((1, Q_TILE), I32)],
        compiler_params=_params(("parallel", "arbitrary")),
        name="dsa_prompt",
    )(q, kaug16, vt16, qi, kiwi, kiwi16)
    return o.reshape(b * t, OUT_C)


def _dsa_sample_body(n_pages, pt_ref, q_ref, kn_ref, vn_ref, qi_ref, kiwi_ref, *refs):
    sb = q_ref.shape[0]
    o_ref = refs[3 * sb * n_pages]
    j_ref = refs[3 * sb * n_pages + 1]
    pages = lambda kind, s_i: refs[(kind * sb + s_i) * n_pages:(kind * sb + s_i + 1) * n_pages]
    rep = HQ_C // HKV_C
    ts = q_ref.shape[1]
    n_cache = n_pages * PAGE_SIZE
    scs = []
    for s_i in range(sb):
        ip = pages(2, s_i)
        kiwi = kiwi_ref[s_i]
        qi = qi_ref[s_i].astype(BF16)
        qi8 = jnp.concatenate([qi[:, h * IDX_DIM:(h + 1) * IDX_DIM] for h in range(N_IDX_HEADS)], axis=0)
        wv = kiwi[:, IDX_DIM:IDX_DIM + N_IDX_HEADS] * (IDX_DIM ** -0.5 * N_IDX_HEADS ** -0.5)
        dots = jnp.concatenate(
            [jnp.dot(qi8, ip[p][0].astype(BF16), preferred_element_type=F32) for p in range(n_pages)]
            + [lax.dot_general(qi8, _pad_rows(kiwi[:, 0:IDX_DIM], LANES).astype(BF16), _NT,
                               preferred_element_type=F32)], axis=1)
        scs.append(_index_scores(dots, wv, ts))
    sc = jnp.concatenate(scs, axis=0)
    col = lax.broadcasted_iota(I32, sc.shape, 1)
    trow = lax.broadcasted_iota(I32, sc.shape, 0) & (ts - 1)
    adm = (col < n_cache) | (col - n_cache <= trow)
    keys = jnp.where(adm, sc + 0.0, -jnp.inf)

    def count(mask):
        return jnp.sum(mask.astype(I32), axis=1, keepdims=True)

    n_sel = min(DSA_TOPK, (n_cache + ts) // 4)
    thr, real = _kth_largest2(lambda cands: [count(keys >= cd) for cd in cands], (sb * ts, 1), n_sel)
    need = n_sel - count(keys > thr)
    tie = (keys == thr) & real

    j_ref[...] = jnp.full((sb * ts, 1), n_cache + LANES, I32)

    @pl.when(jnp.max(jnp.where(count(tie) > need, 1, 0)) > 0)
    def _():
        def step(b, jc):
            cand = jc + jnp.left_shift(jnp.int32(1), 13 - b)
            return jnp.where(count(tie & (col < cand)) < need, cand, jc)
        j_ref[...] = lax.fori_loop(0, 14, step, jnp.zeros((sb * ts, 1), I32))

    jmax = j_ref[...]
    keep = (keys > thr) | (tie & (col <= jmax))
    bias_all = jnp.where(keep, 0.0, NEG_BIG)

    for s_i in range(sb):
        kp, vp = pages(0, s_i), pages(1, s_i)
        bias = jnp.concatenate([bias_all[s_i * ts:(s_i + 1) * ts]] * HQ_C, axis=0)
        qf16 = _block_diag_queries(q_ref[s_i], HKV_C, rep).astype(BF16)
        kn16 = _pad_rows(kn_ref[s_i], LANES).astype(BF16)
        vn16 = _pad_rows(vn_ref[s_i], LANES).astype(BF16)
        s = jnp.concatenate([jnp.dot(qf16, kp[p][0].astype(BF16), preferred_element_type=F32)
                             for p in range(n_pages)]
                            + [lax.dot_general(qf16, kn16, _NT, preferred_element_type=F32)], axis=1) + bias
        m = jnp.max(s, axis=1, keepdims=True)
        p_all = jnp.exp(s - m)
        l = jnp.sum(p_all, axis=1, keepdims=True)
        p16 = p_all.astype(BF16)
        acc = jnp.dot(p16[:, n_cache:n_cache + LANES], vn16, preferred_element_type=F32)
        for p in range(n_pages):
            acc = acc + lax.dot_general(p16[:, p * PAGE_SIZE:(p + 1) * PAGE_SIZE], vp[p][0].astype(BF16), _NT,
                                        preferred_element_type=F32)
        o_ref[s_i] = _unstack_block_diag(acc / l, HKV_C, rep, ts)


def _dsa_sample(q, kn, vn, qi, kiwi, cache_k, cache_v, cache_ki, page_table):
    db, ts, _ = q.shape
    n_pages = page_table.shape[1]
    ck = _pages_minor(cache_k)
    cv = _pages_minor(cache_v)
    cache_ki = _pages_minor(cache_ki)
    sb = SAMPLE_SEQS
    seq = lambda w: pl.BlockSpec((sb, ts, w), lambda i, pt: (i, 0, 0))
    o = pl.pallas_call(
        functools.partial(_dsa_sample_body, n_pages),
        grid_spec=pltpu.PrefetchScalarGridSpec(
            num_scalar_prefetch=1,
            grid=(db // sb,),
            in_specs=[seq(Q_C), seq(KV_C), seq(KV_C), seq(N_IDX_HEADS * IDX_DIM), seq(LANES)]
            + _page_specs(n_pages, KV_C, sb) * 2 + _page_specs(n_pages, IDX_DIM, sb),
            out_specs=seq(OUT_C),
            scratch_shapes=[pltpu.VMEM((sb * ts, 1), I32)],
        ),
        out_shape=jax.ShapeDtypeStruct((db, ts, OUT_C), F32),
        compiler_params=_params(("arbitrary",)),
        name="dsa_sample",
    )(page_table.reshape(-1), q, kn, vn, qi, kiwi,
      *([ck] * (sb * n_pages)), *([cv] * (sb * n_pages)), *([cache_ki] * (sb * n_pages)))
    return o.reshape(db * ts, OUT_C)


def _rope_tables(pos):
    half = HEAD_DIM // 2
    inv_freq = ROPE_THETA ** (-jnp.arange(half, dtype=F32) / half)
    ang = pos.astype(F32)[:, None] * inv_freq[None, :]
    c, s = jnp.cos(ang), jnp.sin(ang)
    return jnp.tile(c, (1, LANES // half)), jnp.concatenate([-s, s] * (LANES // HEAD_DIM), axis=1)


def _gain128(g):
    return jnp.tile(g.astype(F32), LANES // HEAD_DIM).reshape(1, LANES)


def _layer_a(xp, xs, shp, shs, states, params, tabs_p, tabs_s):
    norm_g, w_in, q_gain, k_gain, w_out = params
    b, t = shp
    db, ts = shs
    w16 = w_in.astype(BF16)
    args = (norm_g.reshape(1, -1), w16)
    gains = (_gain128(q_gain), _gain128(k_gain))
    qp, kp, vp, kp16, vp16, sgp = _project("a", xp, *args, *tabs_p, *gains)
    qs, ks, vs, _, _, sgs = _project("a", xs, *args, *tabs_s, *gains)
    hw = HKV_A * HEAD_DIM
    grp_p, grp_s, new_state = [], [], []
    for g, (win, dil) in enumerate(DIL_GROUPS):
        grp_p.append(_dil_prompt(qp.reshape(b, t, Q_A), kp.reshape(b, t, KV_A), vp.reshape(b, t, KV_A), g, dil))
        slabs, ns = _dil_sample(qs.reshape(db, ts, Q_A), ks.reshape(db, ts, KV_A), vs.reshape(db, ts, KV_A),
                                states[g], g, states[g].shape[1], dil)
        grp_s.append(slabs)
        wk = min(win, t)
        kg = kp.reshape(b, t, KV_A)[:, t - wk:, g * hw:(g + 1) * hw].reshape(b, wk, 1, HKV_A, HEAD_DIM)
        vg = vp.reshape(b, t, KV_A)[:, t - wk:, g * hw:(g + 1) * hw].reshape(b, wk, 1, HKV_A, HEAD_DIM)
        new_state.append(jnp.concatenate([kg, vg], axis=2))
        new_state.append(ns)
    w_out16 = w_out.astype(BF16)
    return _out_project(grp_p, sgp, xp, w_out16), _out_project(grp_s, sgs, xs, w_out16), new_state


def _layer_b(xp, xs, shp, shs, states, params, tabs_p, tabs_s, page_table):
    norm_g, w_in, q_gain, k_gain, w_out = params
    cache_k, cache_v = states
    b, t = shp
    db, ts = shs
    w16 = w_in.astype(BF16)
    args = (norm_g.reshape(1, -1), w16)
    gains = (_gain128(q_gain), _gain128(k_gain))
    qp, kp, vp, _, _, sgp, kmean, kaug16, vt16 = _project("b", xp, *args, *tabs_p, *gains, aug=True)
    qs, ks, vs, _, _, sgs = _project("b", xs, *args, *tabs_s, *gains)
    o_p = _moba_prompt(qp.reshape(b, t, Q_B), kaug16.reshape(b, t, HKV_B * LANES), vt16,
                       kmean.reshape(b, t // MOBA_BLOCK, KV_B))
    o_s = _moba_sample(qs.reshape(db, ts, Q_B), ks.reshape(db, ts, KV_B), vs.reshape(db, ts, KV_B),
                       cache_k, cache_v, page_table)
    w_out16 = w_out.astype(BF16)
    new_state = [kp.reshape(b, t, HKV_B, HEAD_DIM), ks.reshape(db, ts, HKV_B, HEAD_DIM),
                 vp.reshape(b, t, HKV_B, HEAD_DIM), vs.reshape(db, ts, HKV_B, HEAD_DIM)]
    return _out_project([[o_p]], sgp, xp, w_out16), _out_project([[o_s]], sgs, xs, w_out16), new_state


def _layer_c(xp, xs, shp, shs, states, params, tabs_p, tabs_s, page_table):
    norm_g, w_in, q_gain, k_gain, w_out = params
    cache_k, cache_v, cache_ki = states
    b, t = shp
    db, ts = shs
    w16 = jnp.pad(w_in, ((0, 0), (0, IN_C_PAD - IN_C))).astype(BF16)
    args = (norm_g.reshape(1, -1), w16)
    gains = (_gain128(q_gain), _gain128(k_gain))
    qp, kp, vp, _, _, sgp, qip, kiwip, kiwip16, kaug16, vt16 = _project("c", xp, *args, *tabs_p, *gains, aug=True)
    qs, ks, vs, _, _, sgs, qis, kiwis, _ = _project("c", xs, *args, *tabs_s, *gains)
    o_p = _dsa_prompt(qp.reshape(b, t, Q_C), kaug16.reshape(b, t, HKV_C * LANES), vt16,
                      qip.reshape(b, t, -1), kiwip.reshape(b, t, LANES), kiwip16.reshape(b, t, LANES))
    o_s = _dsa_sample(qs.reshape(db, ts, Q_C), ks.reshape(db, ts, KV_C), vs.reshape(db, ts, KV_C),
                      qis.reshape(db, ts, -1), kiwis.reshape(db, ts, LANES), cache_k, cache_v, cache_ki, page_table)
    w_out16 = w_out.astype(BF16)
    new_state = [kp.reshape(b, t, HKV_C, HEAD_DIM), ks.reshape(db, ts, HKV_C, HEAD_DIM),
                 vp.reshape(b, t, HKV_C, HEAD_DIM), vs.reshape(db, ts, HKV_C, HEAD_DIM),
                 kiwip.reshape(b, t, LANES)[:, :, :IDX_DIM], kiwis.reshape(db, ts, LANES)[:, :, :IDX_DIM]]
    return _out_project([[o_p]], sgp, xp, w_out16), _out_project([[o_s]], sgs, xs, w_out16), new_state


def kernel(x_prompt, x_sample, state_l0_kv_w128, state_l0_kv_w512, state_l0_kv_w2048, cache_l1_k, cache_l1_v, cache_l2_k, cache_l2_v, cache_l2_kidx, state_l3_kv_w128, state_l3_kv_w512, state_l3_kv_w2048, page_table, l0_norm, l0_w_in, l0_q_norm, l0_k_norm, l0_w_out, l1_norm, l1_w_in, l1_q_norm, l1_k_norm, l1_w_out, l2_norm, l2_w_in, l2_q_norm, l2_k_norm, l2_w_out, l3_norm, l3_w_in, l3_q_norm, l3_k_norm, l3_w_out):
    b, t, _ = x_prompt.shape
    db, ts, _ = x_sample.shape
    assert t % (16 * Q_TILE) == 0 and t % DSA_CHUNK == 0 and (db * ts) % PROJ_ROWS == 0 and PROJ_ROWS % ts == 0
    tabs_p = _rope_tables(jnp.arange(t))
    tabs_s = _rope_tables(PAST_LEN + (jnp.arange(PROJ_ROWS) % ts))
    xp = x_prompt.reshape(b * t, D_MODEL)
    xs = x_sample.reshape(db * ts, D_MODEL)
    states = ((state_l0_kv_w128, state_l0_kv_w512, state_l0_kv_w2048), (cache_l1_k, cache_l1_v),
              (cache_l2_k, cache_l2_v, cache_l2_kidx), (state_l3_kv_w128, state_l3_kv_w512, state_l3_kv_w2048))
    params = ((l0_norm, l0_w_in, l0_q_norm, l0_k_norm, l0_w_out), (l1_norm, l1_w_in, l1_q_norm, l1_k_norm, l1_w_out),
              (l2_norm, l2_w_in, l2_q_norm, l2_k_norm, l2_w_out), (l3_norm, l3_w_in, l3_q_norm, l3_k_norm, l3_w_out))
    new_state = []
    for i in range(4):
        common = (xp, xs, (b, t), (db, ts), states[i], params[i], tabs_p, tabs_s)
        if i % 3 == 0:
            xp, xs, st = _layer_a(*common)
        elif i % 3 == 1:
            xp, xs, st = _layer_b(*common, page_table)
        else:
            xp, xs, st = _layer_c(*common, page_table)
        new_state.extend(st)
    return (xp.reshape(b, t, D_MODEL), xs.reshape(db, ts, D_MODEL), *new_state)
```

```python
import functools

import jax
import jax.numpy as jnp
from jax import lax
from jax.experimental import pallas as pl
from jax.experimental.pallas import tpu as pltpu

F32 = jnp.float32
BF16 = jnp.bfloat16
I32 = jnp.int32

D_MODEL = 1024
PAST_LEN = 2048
PAGE_SIZE = 128
HEAD_DIM = 64
ROPE_THETA = 10000.0
NORM_EPS = 1e-6

DIL_GROUPS = ((128, 1), (512, 4), (2048, 16))
HQ_A, HKV_A = 8, 2
Q_A, KV_A, OUT_A = 1536, 384, 512

HQ_B, HKV_B = 16, 4
MOBA_BLOCK, MOBA_TOPK = 256, 3
Q_B, KV_B, OUT_B = 1024, 256, 1024

HQ_C, HKV_C = 16, 4
N_IDX_HEADS, IDX_DIM, DSA_TOPK = 8, 64, 256
Q_C, KV_C, OUT_C = 1024, 256, 1024
IN_C = Q_C + 2 * KV_C + OUT_C + N_IDX_HEADS * IDX_DIM + IDX_DIM + N_IDX_HEADS
IN_C_PAD = 3200

LANES = 128
PROJ_ROWS = 256
Q_TILE = 128
DSA_CHUNK = 512
VMEM_LIMIT = 56 * 1024 * 1024
NEG_BIG = -1e30
LOG2E = 1.4426950408889634
VT_ROWS = 80
INT_MIN = -2147483648

_NT = (((1,), (1,)), ((), ()))


def _params(sem):
    return pltpu.CompilerParams(dimension_semantics=sem, vmem_limit_bytes=VMEM_LIMIT)


def _group_sum_matrix():
    r = lax.broadcasted_iota(I32, (LANES, LANES), 0) // HEAD_DIM
    c = lax.broadcasted_iota(I32, (LANES, LANES), 1) // HEAD_DIM
    return (r == c).astype(BF16)


def _rope(y, cos, sin):
    lane = lax.broadcasted_iota(I32, y.shape, 1)
    first_half = (lane & (HEAD_DIM // 2)) == 0
    partner = jnp.where(first_half, pltpu.roll(y, LANES - HEAD_DIM // 2, 1), pltpu.roll(y, HEAD_DIM // 2, 1))
    return y * cos + partner * sin


def _head_norm(x, gain, gmat):
    ss = x * x
    hi = ss.astype(BF16)
    lo = (ss - hi.astype(F32)).astype(BF16)
    gs = jnp.dot(hi, gmat, preferred_element_type=F32) + jnp.dot(lo, gmat, preferred_element_type=F32)
    return x * lax.rsqrt(gs * (1.0 / HEAD_DIM) + NORM_EPS) * gain


def _proj_body(kind, aug_tiles, x_ref, g_ref, w_ref, cos_ref, sin_ref, qg_ref, kg_ref, *outs):
    x = x_ref[...]
    ms = jnp.mean(x * x, axis=-1, keepdims=True)
    h = (x * lax.rsqrt(ms + NORM_EPS) * g_ref[...]).astype(BF16)
    cos = cos_ref[...]
    sin = sin_ref[...]
    qg = qg_ref[...]
    kg = kg_ref[...]
    gmat = _group_sum_matrix()
    qw, kvw, gw = {"a": (Q_A, KV_A, OUT_A), "b": (Q_B, KV_B, OUT_B), "c": (Q_C, KV_C, OUT_C)}[kind]
    q_ref, k_ref, v_ref, k16_ref, v16_ref, sg_ref = outs[:6]
    rest = outs[6:]
    moba = aug_tiles and kind == "b"
    if aug_tiles:
        kaug_ref, vt_ref = rest[-2:]

    def seg(start, width):
        return jnp.dot(h, w_ref[:, start:start + width], preferred_element_type=F32)

    zq = seg(0, qw)
    for c in range(qw // LANES):
        y = _rope(_head_norm(zq[:, c * LANES:(c + 1) * LANES], qg, gmat), cos, sin)
        q_ref[:, c * LANES:(c + 1) * LANES] = y * (HEAD_DIM ** -0.5)
    zk = seg(qw, kvw)
    ksum = []
    for c in range(kvw // LANES):
        y = _rope(_head_norm(zk[:, c * LANES:(c + 1) * LANES], kg, gmat), cos, sin)
        k_ref[:, c * LANES:(c + 1) * LANES] = y
        k16_ref[:, c * LANES:(c + 1) * LANES] = y.astype(BF16)
        if moba:
            ksum.append(jnp.sum(y, axis=0, keepdims=True) * (1.0 / MOBA_BLOCK))
        if aug_tiles:
            lane = lax.broadcasted_iota(I32, y.shape, 1)
            hot = HEAD_DIM + pl.program_id(0) % aug_tiles if moba else -1
            onehot = (lane == hot).astype(F32)
            kaug_ref[:, 2 * c * LANES:(2 * c + 1) * LANES] = jnp.where(lane < HEAD_DIM, y, onehot).astype(BF16)
            kaug_ref[:, (2 * c + 1) * LANES:(2 * c + 2) * LANES] = jnp.where(
                lane < HEAD_DIM, pltpu.roll(y, HEAD_DIM, 1), onehot).astype(BF16)
    zv = seg(qw + kvw, kvw)
    v_ref[...] = zv
    v16_ref[...] = zv.astype(BF16)
    if aug_tiles:
        zvt = zv.T
        tail = (lax.broadcasted_iota(I32, (VT_ROWS - HEAD_DIM, zvt.shape[1]), 0) == 0).astype(BF16)
        for kvh in range(kvw // HEAD_DIM):
            vt_ref[0, kvh * VT_ROWS:kvh * VT_ROWS + HEAD_DIM, :] = zvt[kvh * HEAD_DIM:(kvh + 1) * HEAD_DIM].astype(BF16)
            vt_ref[0, kvh * VT_ROWS + HEAD_DIM:(kvh + 1) * VT_ROWS, :] = tail
    zg = seg(qw + 2 * kvw, gw)
    sg_ref[...] = zg / (1.0 + jnp.exp(-zg))
    nrest = 0
    if kind == "c":
        qi_ref, kiwi_ref, kiwi16_ref = rest[:3]
        nrest = 3
        base = qw + 2 * kvw + gw
        zi = seg(base, N_IDX_HEADS * IDX_DIM)
        for c in range(N_IDX_HEADS * IDX_DIM // LANES):
            qi_ref[:, c * LANES:(c + 1) * LANES] = _rope(zi[:, c * LANES:(c + 1) * LANES], cos, sin)
        zz = seg(base + N_IDX_HEADS * IDX_DIM, LANES)
        lane = lax.broadcasted_iota(I32, zz.shape, 1)
        kiwi = jnp.where(lane < IDX_DIM, _rope(zz, cos, sin), zz)
        kiwi_ref[...] = kiwi
        kiwi16_ref[...] = kiwi.astype(BF16)
    if moba:
        km_ref = rest[0]
        for c in range(kvw // LANES):
            km_ref[0, :, c * LANES:(c + 1) * LANES] = ksum[c]


def _project(kind, x, gain, w16, cos, sin, qg, kg, aug=False):
    n = x.shape[0]
    tm = PROJ_ROWS
    nt = cos.shape[0] // tm
    qw, kvw, gw = {"a": (Q_A, KV_A, OUT_A), "b": (Q_B, KV_B, OUT_B), "c": (Q_C, KV_C, OUT_C)}[kind]
    wp = w16.shape[1]

    def rows(width):
        return pl.BlockSpec((tm, width), lambda i: (i, 0))

    def const(shape):
        return pl.BlockSpec(shape, lambda i: (0,) * len(shape))

    out_shape = [jax.ShapeDtypeStruct((n, qw), F32), jax.ShapeDtypeStruct((n, kvw), F32),
                 jax.ShapeDtypeStruct((n, kvw), F32), jax.ShapeDtypeStruct((n, kvw), BF16),
                 jax.ShapeDtypeStruct((n, kvw), BF16), jax.ShapeDtypeStruct((n, gw), F32)]
    out_specs = [rows(qw), rows(kvw), rows(kvw), rows(kvw), rows(kvw), rows(gw)]
    if kind == "c":
        out_shape += [jax.ShapeDtypeStruct((n, N_IDX_HEADS * IDX_DIM), F32),
                      jax.ShapeDtypeStruct((n, LANES), F32), jax.ShapeDtypeStruct((n, LANES), BF16)]
        out_specs += [rows(N_IDX_HEADS * IDX_DIM), rows(LANES), rows(LANES)]
    if aug:
        n_kv = kvw // HEAD_DIM
        if kind == "b":
            out_shape.append(jax.ShapeDtypeStruct((n // tm, 1, kvw), F32))
            out_specs.append(pl.BlockSpec((1, 1, kvw), lambda i: (i, 0, 0)))
        out_shape += [jax.ShapeDtypeStruct((n, n_kv * LANES), BF16),
                      jax.ShapeDtypeStruct((n // (nt * tm), n_kv * VT_ROWS, nt * tm), BF16)]
        out_specs += [rows(n_kv * LANES), pl.BlockSpec((1, n_kv * VT_ROWS, tm), lambda i: (i // nt, 0, i % nt))]
    return pl.pallas_call(
        functools.partial(_proj_body, kind, nt if aug else 0),
        grid=(n // tm,),
        in_specs=[rows(D_MODEL), const((1, D_MODEL)), const((D_MODEL, wp)),
                  pl.BlockSpec((tm, LANES), lambda i: (i % nt, 0)),
                  pl.BlockSpec((tm, LANES), lambda i: (i % nt, 0)),
                  const((1, LANES)), const((1, LANES))],
        out_specs=out_specs,
        out_shape=out_shape,
        compiler_params=_params(("parallel",)),
        name=f"proj_{kind}",
    )(x, gain, w16, cos, sin, qg, kg)


LSE_REP = LANES // HQ_A
O_SLABS = OUT_A // LANES


def _outproj_body(n_groups, *refs):
    per = O_SLABS + 1
    sg_ref, x_ref, w_ref, y_ref = refs[-4:]
    if n_groups > 1:
        lses = [refs[g * per + O_SLABS][...] for g in range(n_groups)]
        m = functools.reduce(jnp.maximum, lses)
        es = [jnp.exp(l - m) for l in lses]
        tot = functools.reduce(lambda a, b: a + b, es)
        lane = lax.broadcasted_iota(I32, m.shape, 1)
        slabs = []
        for c in range(O_SLABS):
            acc = None
            for g in range(n_groups):
                wg = es[g] / tot
                lo = wg[:, 2 * c * LSE_REP:2 * c * LSE_REP + 1]
                hi = wg[:, (2 * c + 1) * LSE_REP:(2 * c + 1) * LSE_REP + 1]
                term = jnp.where(lane < HEAD_DIM, lo, hi) * refs[g * per + c][...]
                acc = term if acc is None else acc + term
            slabs.append(acc)
        o = jnp.concatenate(slabs, axis=1)
    else:
        o = refs[0][...]
    g16 = (o * sg_ref[...]).astype(BF16)
    y_ref[...] = x_ref[...] + jnp.dot(g16, w_ref[...], preferred_element_type=F32)


def _out_project(groups, sg, x, w16):
    n = x.shape[0]
    f = sg.shape[1]
    tm = PROJ_ROWS
    ins = [a for grp in groups for a in grp]
    in_specs = [pl.BlockSpec((tm, a.shape[1]), lambda i: (i, 0)) for a in ins]
    in_specs += [pl.BlockSpec((tm, f), lambda i: (i, 0)), pl.BlockSpec((tm, D_MODEL), lambda i: (i, 0)),
                 pl.BlockSpec((f, D_MODEL), lambda i: (0, 0))]
    return pl.pallas_call(
        functools.partial(_outproj_body, len(groups)),
        grid=(n // tm,),
        in_specs=in_specs,
        out_specs=pl.BlockSpec((tm, D_MODEL), lambda i: (i, 0)),
        out_shape=jax.ShapeDtypeStruct((n, D_MODEL), F32),
        compiler_params=_params(("parallel",)),
        name="out_proj_merge" if len(groups) > 1 else "out_proj",
    )(*ins, sg, x, w16)


def _stack_heads(q, kvh, rep):
    return jnp.concatenate([q[:, (kvh * rep + r) * HEAD_DIM:(kvh * rep + r + 1) * HEAD_DIM] for r in range(rep)],
                           axis=0)


FLASH_BLOCK = 256


def _flash_pair(k_ref, v_ref, qas, carry, n, bias_ref):
    carry = list(carry)
    units = [(j, kvh) for j in range(2) for kvh in range(len(qas))]
    offs = [pl.multiple_of((2 * n + j) * FLASH_BLOCK, FLASH_BLOCK) for j in range(2)]
    rep = qas[0].shape[1] // Q_TILE
    scores = []
    for j, kvh in units:
        st = jnp.dot(k_ref[0, pl.ds(offs[j], FLASH_BLOCK), kvh * LANES:(kvh + 1) * LANES], qas[kvh],
                     preferred_element_type=F32)
        if bias_ref is not None:
            st = st + jnp.concatenate([bias_ref[pl.ds(offs[j], FLASH_BLOCK), :]] * rep, axis=1)
        scores.append(st)
    for (j, kvh), st in zip(units, scores):
        m, acc = carry[kvh]
        m_new = jnp.maximum(m, jnp.max(st, axis=0, keepdims=True))
        p = jnp.exp2(st - m_new)
        pv = jnp.dot(v_ref[0, kvh * VT_ROWS:(kvh + 1) * VT_ROWS, pl.ds(offs[j], FLASH_BLOCK)], p.astype(BF16),
                     preferred_element_type=F32)
        carry[kvh] = (m_new, jnp.exp2(m - m_new) * acc + pv)
    return tuple(carry)


def _lse_slab(lses, rows):
    return jnp.concatenate([jnp.broadcast_to(l, (rows, LSE_REP)) for l in lses], axis=1)


def _dil_prompt_body(dil, *refs):
    q_refs = refs[:O_SLABS]
    kp_ref, kc_ref, vp_ref, vc_ref = refs[O_SLABS:O_SLABS + 4]
    o_refs = refs[O_SLABS + 4:2 * O_SLABS + 4]
    lse_ref = refs[2 * O_SLABS + 4]
    j = pl.program_id(1)
    tq = Q_TILE
    rep = HQ_A // HKV_A
    row = lax.broadcasted_iota(I32, (tq, 2 * tq), 0)
    col = lax.broadcasted_iota(I32, (tq, 2 * tq), 1)
    dist = row + tq - col
    valid = (dist >= 0) & (dist <= tq) & ((col >= tq) | (j > 0))
    valid = jnp.concatenate([valid] * rep, axis=0)

    def residue(r, _):
        rows = pl.ds(r, tq, stride=dil) if dil > 1 else pl.ds(0, tq)
        q = jnp.concatenate([qr[0, rows, :] for qr in q_refs], axis=1).astype(BF16)
        kk = jnp.concatenate([kp_ref[0, rows, :], kc_ref[0, rows, :]], axis=0).astype(BF16)
        vv = jnp.concatenate([vp_ref[0, rows, :], vc_ref[0, rows, :]], axis=0).astype(BF16)
        outs, lses = [], []
        for kvh in range(HKV_A):
            kh = kk[:, kvh * HEAD_DIM:(kvh + 1) * HEAD_DIM]
            vh = vv[:, kvh * HEAD_DIM:(kvh + 1) * HEAD_DIM]
            qh = _stack_heads(q, kvh, rep)
            s = lax.dot_general(qh, kh, _NT, preferred_element_type=F32)
            s = jnp.where(valid, s, -jnp.inf)
            m = jnp.max(s, axis=1, keepdims=True)
            p = jnp.exp(s - m)
            l = jnp.sum(p, axis=1, keepdims=True)
            o = jnp.dot(p.astype(BF16), vh, preferred_element_type=F32) / l
            lse = m + jnp.log(l)
            for h in range(rep):
                outs.append(o[h * tq:(h + 1) * tq])
                lses.append(lse[h * tq:(h + 1) * tq])
        for c in range(O_SLABS):
            o_refs[c][0, rows, :] = jnp.concatenate(outs[2 * c:2 * c + 2], axis=1)
        lse_ref[0, rows, :] = _lse_slab(lses, tq)
        return 0

    if dil == 1:
        residue(0, 0)
    else:
        lax.fori_loop(0, dil, residue, 0)


def _dil_prompt(q, k, v, g, dil):
    b, t, _ = q.shape
    blk = Q_TILE * dil
    hw = HKV_A * HEAD_DIM
    cur = lambda bb, j: (bb, j, g)
    prev = lambda bb, j: (bb, jnp.maximum(j - 1, 0), g)
    slab = pl.BlockSpec((1, blk, LANES), lambda bb, j: (bb, j, 0))
    outs = pl.pallas_call(
        functools.partial(_dil_prompt_body, dil),
        grid=(b, t // blk),
        in_specs=[pl.BlockSpec((1, blk, LANES), functools.partial(lambda c, bb, j: (bb, j, g * O_SLABS + c), c))
                  for c in range(O_SLABS)]
        + [pl.BlockSpec((1, blk, hw), prev), pl.BlockSpec((1, blk, hw), cur),
           pl.BlockSpec((1, blk, hw), prev), pl.BlockSpec((1, blk, hw), cur)],
        out_specs=[slab] * (O_SLABS + 1),
        out_shape=[jax.ShapeDtypeStruct((b, t, LANES), F32)] * (O_SLABS + 1),
        compiler_params=_params(("parallel", "arbitrary")),
        name=f"dil_prompt_g{g}",
    )(*([q] * O_SLABS), k, k, v, v)
    return [a.reshape(b * t, LANES) for a in outs]


def _dil_sample_body(win, dil, sb, q_ref, kn_ref, vn_ref, st_ref, *out_refs):
    o_refs = out_refs[:O_SLABS]
    lse_ref, ns_ref = out_refs[O_SLABS:]
    rep = HQ_A // HKV_A
    hw = HKV_A * HEAD_DIM
    ts = q_ref.shape[1]
    row = lax.broadcasted_iota(I32, (rep * ts, win), 0)
    col = lax.broadcasted_iota(I32, (rep * ts, win), 1)
    dist = win + (row & (ts - 1)) - col
    valid_c = (dist <= win) & ((dist & (dil - 1)) == 0)
    rown = lax.broadcasted_iota(I32, (rep * ts, LANES), 0)
    coln = lax.broadcasted_iota(I32, (rep * ts, LANES), 1)
    distn = (rown & (ts - 1)) - coln
    valid_n = (distn >= 0) & ((distn & (dil - 1)) == 0)
    lane_w = lax.broadcasted_iota(I32, (2 * hw, LANES), 1)
    for s_i in range(sb):
        st = st_ref[s_i]
        knp = _pad_rows(kn_ref[s_i], LANES)
        vnp = _pad_rows(vn_ref[s_i], LANES)
        shifted = pltpu.roll(st, win - ts, 1)
        newcols = pltpu.roll(jnp.concatenate([knp.T, vnp.T], axis=0), LANES - ts, 1)
        if win > LANES:
            ns_ref[s_i, :, 0:win - LANES] = shifted[:, 0:win - LANES]
        ns_ref[s_i, :, win - LANES:win] = jnp.where(lane_w >= LANES - ts, newcols, shifted[:, win - LANES:win])
        q = q_ref[s_i].astype(BF16)
        kn16 = knp.astype(BF16)
        vn16 = vnp.astype(BF16)
        outs, lses = [], []
        for kvh in range(HKV_A):
            hs = slice(kvh * HEAD_DIM, (kvh + 1) * HEAD_DIM)
            kt = st[kvh * HEAD_DIM:(kvh + 1) * HEAD_DIM].astype(BF16)
            vt = st[hw + kvh * HEAD_DIM:hw + (kvh + 1) * HEAD_DIM].astype(BF16)
            qh = _stack_heads(q, kvh, rep)
            s_c = jnp.where(valid_c, jnp.dot(qh, kt, preferred_element_type=F32), -jnp.inf)
            s_n = jnp.where(valid_n, lax.dot_general(qh, kn16[:, hs], _NT, preferred_element_type=F32), -jnp.inf)
            m = jnp.maximum(jnp.max(s_c, axis=1, keepdims=True), jnp.max(s_n, axis=1, keepdims=True))
            p_c = jnp.exp(s_c - m)
            p_n = jnp.exp(s_n - m)
            l = jnp.sum(p_c, axis=1, keepdims=True) + jnp.sum(p_n, axis=1, keepdims=True)
            o = (lax.dot_general(p_c.astype(BF16), vt, _NT, preferred_element_type=F32)
                 + jnp.dot(p_n.astype(BF16), vn16[:, hs], preferred_element_type=F32)) / l
            lse = m + jnp.log(l)
            for r in range(rep):
                outs.append(o[r * ts:(r + 1) * ts])
                lses.append(lse[r * ts:(r + 1) * ts])
        for c in range(O_SLABS):
            o_refs[c][s_i] = jnp.concatenate(outs[2 * c:2 * c + 2], axis=1)
        lse_ref[s_i] = _lse_slab(lses, ts)


def _dil_sample(q, kn, vn, state, g, win, dil):
    db, ts, _ = q.shape
    hw = HKV_A * HEAD_DIM
    st = state.transpose(0, 2, 3, 4, 1).reshape(db, 2 * hw, win)
    sb = max(1, min(8, 1024 // win))
    slab = pl.BlockSpec((sb, ts, LANES), lambda i: (i, 0, 0))
    *slabs, ns = pl.pallas_call(
        functools.partial(_dil_sample_body, win, dil, sb),
        grid=(db // sb,),
        in_specs=[pl.BlockSpec((sb, ts, OUT_A), lambda i: (i, 0, g)),
                  pl.BlockSpec((sb, ts, hw), lambda i: (i, 0, g)),
                  pl.BlockSpec((sb, ts, hw), lambda i: (i, 0, g)),
                  pl.BlockSpec((sb, 2 * hw, win), lambda i: (i, 0, 0))],
        out_specs=[slab] * (O_SLABS + 1) + [pl.BlockSpec((sb, 2 * hw, win), lambda i: (i, 0, 0))],
        out_shape=[jax.ShapeDtypeStruct((db, ts, LANES), F32)] * (O_SLABS + 1)
        + [jax.ShapeDtypeStruct((db, 2 * hw, win), F32)],
        compiler_params=_params(("parallel",)),
        name=f"dil_sample_g{g}",
    )(q, kn, vn, st)
    ns = ns.reshape(db, 2, HKV_A, HEAD_DIM, win).transpose(0, 4, 1, 2, 3)
    return [a.reshape(db * ts, LANES) for a in slabs], ns


def _top_blocks(bs, n_valid, topk):
    lane = lax.broadcasted_iota(I32, bs.shape, 1)
    nb = bs.shape[1]
    work = jnp.where(lane < n_valid, bs, -jnp.inf)
    chosen = jnp.zeros(bs.shape, jnp.bool_)
    for _ in range(topk):
        m = jnp.max(work, axis=1, keepdims=True)
        idx = jnp.min(jnp.where(work == m, lane, nb), axis=1, keepdims=True)
        pick = lane == idx
        chosen = chosen | pick
        work = jnp.where(pick, -jnp.inf, work)
    return chosen & (lane < n_valid)


def _top_blocks_t(bs, n_valid, topk):
    row = lax.broadcasted_iota(I32, bs.shape, 0)
    rowf = row.astype(F32)
    work = jnp.where(row < n_valid, bs, -jnp.inf)
    chosen = jnp.zeros(bs.shape, jnp.bool_)
    for _ in range(topk):
        m = jnp.max(work, axis=0, keepdims=True)
        idx = jnp.min(jnp.where(work == m, rowf, float(bs.shape[0])), axis=0, keepdims=True)
        pick = rowf == idx
        chosen = chosen | pick
        work = jnp.where(pick, -jnp.inf, work)
    return chosen & (row < n_valid)


def _moba_prompt_body(q_ref, k_ref, v_ref, km_ref, o_ref):
    i = pl.program_id(1)
    tq = Q_TILE
    rep = HQ_B // HKV_B
    nb = km_ref.shape[1]
    nr = rep * tq
    own = (i * tq) // MOBA_BLOCK
    own0 = pl.multiple_of(own * MOBA_BLOCK, MOBA_BLOCK)
    qt = q_ref[0].T
    km = km_ref[0]
    krow = lax.broadcasted_iota(I32, (MOBA_BLOCK, nr), 0)
    qcol = lax.broadcasted_iota(I32, (MOBA_BLOCK, nr), 1)
    causal = own0 + krow <= i * tq + (qcol & (tq - 1))
    brow = lax.broadcasted_iota(I32, (HEAD_DIM, nr), 0)
    qas_past, init = [], []
    for kvh in range(HKV_B):
        hs = slice(kvh * HEAD_DIM, (kvh + 1) * HEAD_DIM)
        ls = slice(kvh * LANES, (kvh + 1) * LANES)
        vs = slice(kvh * VT_ROWS, (kvh + 1) * VT_ROWS)
        q4t = jnp.concatenate([qt[(kvh * rep + r) * HEAD_DIM:(kvh * rep + r + 1) * HEAD_DIM] for r in range(rep)],
                              axis=1)
        bst = jnp.dot(km[:, hs], q4t, precision=lax.Precision.HIGHEST, preferred_element_type=F32)
        if nb < HEAD_DIM:
            bst = jnp.concatenate([bst, jnp.zeros((HEAD_DIM - nb, nr), F32)], axis=0)
        chosen = _top_blocks_t(bst, own, MOBA_TOPK)
        qs = q4t * LOG2E
        qas_past.append(jnp.concatenate([qs, jnp.where(chosen, 0.0, NEG_BIG)], axis=0).astype(BF16))
        qa = jnp.concatenate([qs, jnp.where(brow == own, 0.0, NEG_BIG)], axis=0).astype(BF16)
        st = jnp.dot(k_ref[0, pl.ds(own0, MOBA_BLOCK), ls], qa, preferred_element_type=F32)
        st = jnp.where(causal, st, -jnp.inf)
        m = jnp.max(st, axis=0, keepdims=True)
        p = jnp.exp2(st - m)
        init.append((m, jnp.dot(v_ref[0, vs, pl.ds(own0, MOBA_BLOCK)], p.astype(BF16), preferred_element_type=F32)))

    res = lax.fori_loop(0, (own + 1) // 2,
                        lambda n, carry: _flash_pair(k_ref, v_ref, qas_past, carry, n, None), tuple(init))
    heads = []
    for kvh in range(HKV_B):
        acc = res[kvh][1]
        ot = acc[0:HEAD_DIM] / acc[HEAD_DIM:HEAD_DIM + 1]
        heads += [ot[:, r * tq:(r + 1) * tq] for r in range(rep)]
    o_ref[0] = jnp.concatenate(heads, axis=0).T


def _moba_prompt(q, kaug16, vt16, kmean):
    b, t, _ = q.shape
    nb = t // MOBA_BLOCK
    o = pl.pallas_call(
        _moba_prompt_body,
        grid=(b, t // Q_TILE),
        in_specs=[pl.BlockSpec((1, Q_TILE, Q_B), lambda bb, i: (bb, i, 0)),
                  pl.BlockSpec((1, t, HKV_B * LANES), lambda bb, i: (bb, 0, 0)),
                  pl.BlockSpec((1, HKV_B * VT_ROWS, t), lambda bb, i: (bb, 0, 0)),
                  pl.BlockSpec((1, nb, KV_B), lambda bb, i: (bb, 0, 0))],
        out_specs=pl.BlockSpec((1, Q_TILE, OUT_B), lambda bb, i: (bb, i, 0)),
        out_shape=jax.ShapeDtypeStruct((b, t, OUT_B), F32),
        compiler_params=_params(("parallel", "arbitrary")),
        name="moba_prompt",
    )(q, kaug16, vt16, kmean)
    return o.reshape(b * t, OUT_B)


def _block_diag_queries(q, n_kv, rep):
    ts = q.shape[0]
    rows = []
    for kvh in range(n_kv):
        for r in range(rep):
            hd = (kvh * rep + r) * HEAD_DIM
            parts = []
            if kvh > 0:
                parts.append(jnp.zeros((ts, kvh * HEAD_DIM), q.dtype))
            parts.append(q[:, hd:hd + HEAD_DIM])
            if kvh < n_kv - 1:
                parts.append(jnp.zeros((ts, (n_kv - 1 - kvh) * HEAD_DIM), q.dtype))
            rows.append(jnp.concatenate(parts, axis=1))
    return jnp.concatenate(rows, axis=0)


def _unstack_block_diag(o_all, n_kv, rep, ts):
    parts = []
    for kvh in range(n_kv):
        for r in range(rep):
            r0 = (kvh * rep + r) * ts
            parts.append(o_all[r0:r0 + ts, kvh * HEAD_DIM:(kvh + 1) * HEAD_DIM])
    return jnp.concatenate(parts, axis=1)


def _pad_rows(x, rows):
    return jnp.concatenate([x, jnp.zeros((rows - x.shape[0], x.shape[1]), x.dtype)], axis=0)


SAMPLE_SEQS = 4


def _moba_sample_body(n_pages, pt_ref, q_ref, kn_ref, vn_ref, *refs):
    sb = q_ref.shape[0]
    o_ref = refs[2 * sb * n_pages]
    for s_i in range(sb):
        kp = refs[s_i * n_pages:(s_i + 1) * n_pages]
        vp = refs[(sb + s_i) * n_pages:(sb + s_i + 1) * n_pages]
        o_ref[s_i] = _moba_sample_one(q_ref[s_i], kn_ref[s_i], vn_ref[s_i], kp, vp)


def _moba_sample_one(q, kn, vn, kp, vp):
    n_pages = len(kp)
    rep = HQ_B // HKV_B
    ts = q.shape[0]
    nrows = HQ_B * ts
    ppb = MOBA_BLOCK // PAGE_SIZE
    n_past = n_pages // ppb
    qf = _block_diag_queries(q, HKV_B, rep)
    qf16 = qf.astype(BF16)
    lane_k = lax.broadcasted_iota(I32, (KV_B, LANES), 1)
    kmt = jnp.zeros((KV_B, LANES), F32)
    for n in range(n_past):
        tot = kp[n * ppb][0]
        for j in range(1, ppb):
            tot = tot + kp[n * ppb + j][0]
        kmt = jnp.where(lane_k == n, jnp.sum(tot, axis=1, keepdims=True) * (1.0 / MOBA_BLOCK), kmt)
    bs = jnp.dot(qf, kmt, precision=lax.Precision.HIGHEST, preferred_element_type=F32)
    chosen = _top_blocks(bs, n_past, MOBA_TOPK)
    selb = jnp.where(chosen, 0.0, NEG_BIG)
    bias = jnp.concatenate([jnp.broadcast_to(selb[:, n:n + 1], (nrows, MOBA_BLOCK)) for n in range(n_past)], axis=1)
    s_c = jnp.concatenate([jnp.dot(qf16, kp[p][0].astype(BF16), preferred_element_type=F32)
                           for p in range(n_pages)], axis=1) + bias
    kn16 = _pad_rows(kn, LANES).astype(BF16)
    vn16 = _pad_rows(vn, LANES).astype(BF16)
    s_n = lax.dot_general(qf16, kn16, _NT, preferred_element_type=F32)
    row = lax.broadcasted_iota(I32, (nrows, LANES), 0)
    col = lax.broadcasted_iota(I32, (nrows, LANES), 1)
    s_n = jnp.where(col <= (row & (ts - 1)), s_n, -jnp.inf)
    m = jnp.maximum(jnp.max(s_c, axis=1, keepdims=True), jnp.max(s_n, axis=1, keepdims=True))
    p_c = jnp.exp(s_c - m)
    p_n = jnp.exp(s_n - m)
    l = jnp.sum(p_c, axis=1, keepdims=True) + jnp.sum(p_n, axis=1, keepdims=True)
    acc = jnp.dot(p_n.astype(BF16), vn16, preferred_element_type=F32)
    p16 = p_c.astype(BF16)
    for p in range(n_pages):
        acc = acc + lax.dot_general(p16[:, p * PAGE_SIZE:(p + 1) * PAGE_SIZE], vp[p][0].astype(BF16), _NT,
                                    preferred_element_type=F32)
    return _unstack_block_diag(acc / l, HKV_B, rep, ts)


def _page_specs(n_pages, width, sb):
    return [pl.BlockSpec((1, width, PAGE_SIZE),
                         functools.partial(lambda s, p, i, pt: (pt[(i * sb + s) * n_pages + p], 0, 0), s, p))
            for s in range(sb) for p in range(n_pages)]


def _pages_minor(cache):
    n_pool = cache.shape[0]
    return jnp.moveaxis(cache, 1, -1).reshape(n_pool, -1, PAGE_SIZE)


def _moba_sample(q, kn, vn, cache_k, cache_v, page_table):
    db, ts, _ = q.shape
    n_pages = page_table.shape[1]
    ck = _pages_minor(cache_k)
    cv = _pages_minor(cache_v)
    sb = SAMPLE_SEQS
    seq = lambda w: pl.BlockSpec((sb, ts, w), lambda i, pt: (i, 0, 0))
    o = pl.pallas_call(
        functools.partial(_moba_sample_body, n_pages),
        grid_spec=pltpu.PrefetchScalarGridSpec(
            num_scalar_prefetch=1,
            grid=(db // sb,),
            in_specs=[seq(Q_B), seq(KV_B), seq(KV_B)] + _page_specs(n_pages, KV_B, sb) * 2,
            out_specs=seq(OUT_B),
        ),
        out_shape=jax.ShapeDtypeStruct((db, ts, OUT_B), F32),
        compiler_params=_params(("arbitrary",)),
        name="moba_sample",
    )(page_table.reshape(-1), q, kn, vn, *([ck] * (sb * n_pages)), *([cv] * (sb * n_pages)))
    return o.reshape(db * ts, OUT_B)


def _code_to_float(t):
    return pltpu.bitcast(jnp.where(t >= 0, t, t ^ 0x7FFFFFFF), F32)


def _threshold_of(t):
    real = t > INT_MIN
    return jnp.where(real, _code_to_float(t), -jnp.inf), real


def _kth_largest1(count_ge, shape, k):
    def step(b, t):
        cand = jnp.where(b == 0, jnp.zeros_like(t), t | jnp.left_shift(jnp.int32(1), 31 - b))
        return jnp.where(count_ge([_code_to_float(cand)])[0] >= k, cand, t)

    return _threshold_of(lax.fori_loop(0, 32, step, jnp.full(shape, INT_MIN, I32)))


def _kth_largest2(count_ge, shape, k):
    def pick(t, cands):
        for cd, cnt in zip(cands, count_ge([_code_to_float(cd) for cd in cands])):
            t = jnp.where(cnt >= k, cd, t)
        return t

    t = pick(jnp.full(shape, INT_MIN, I32),
             tuple(jnp.full(shape, v, I32) for v in (INT_MIN + (1 << 30), 0, 1 << 30)))

    def step(b, t):
        sh = 28 - 2 * b
        return pick(t, tuple(t | jnp.left_shift(jnp.int32(j), sh) for j in (1, 2, 3)))

    return _threshold_of(lax.fori_loop(0, 15, step, t))


def _index_scores(dots, wv, rows):
    sc = jnp.maximum(dots[0:rows], 0.0) * wv[:, 0:1]
    for h in range(1, N_IDX_HEADS):
        sc = sc + jnp.maximum(dots[h * rows:(h + 1) * rows], 0.0) * wv[:, h:h + 1]
    return sc


def _dsa_prompt_body(q_ref, k_ref, v_ref, qi_ref, w_ref, ki_ref, o_ref, key_ref, bias_ref, j_ref):
    i = pl.program_id(1)
    tq = Q_TILE
    ck = DSA_CHUNK
    rep = HQ_C // HKV_C
    t_all = key_ref.shape[0]
    nch = ((i + 1) * tq + ck - 1) // ck
    qpos = i * tq + lax.broadcasted_iota(I32, (1, tq), 1)
    qit = qi_ref[0].T
    qi8t = jnp.concatenate([qit[h * IDX_DIM:(h + 1) * IDX_DIM] for h in range(N_IDX_HEADS)], axis=1)
    qi8t = jnp.concatenate([qi8t, jnp.zeros_like(qi8t)], axis=0).astype(BF16)
    wt = w_ref[0].T[IDX_DIM:IDX_DIM + N_IDX_HEADS] * (IDX_DIM ** -0.5 * N_IDX_HEADS ** -0.5)
    row_c = lax.broadcasted_iota(I32, (ck, tq), 0)

    def scores(c, _):
        off = pl.multiple_of(c * ck, ck)
        kic = ki_ref[0, pl.ds(off, ck), :]
        sc = None
        for hp in range(N_IDX_HEADS // 2):
            d2 = jnp.dot(kic, qi8t[:, 2 * hp * tq:(2 * hp + 2) * tq], preferred_element_type=F32)
            for j in range(2):
                h = 2 * hp + j
                term = jnp.maximum(d2[:, j * tq:(j + 1) * tq], 0.0) * wt[h:h + 1]
                sc = term if sc is None else sc + term
        key_ref[pl.ds(off, ck), :] = jnp.where(off + row_c <= qpos, sc + 0.0, -jnp.inf)
        return 0

    lax.fori_loop(0, nch, scores, 0)

    def count(preds):
        def body(c, accs):
            off = pl.multiple_of(c * ck, ck)
            kx = key_ref[pl.ds(off, ck), :]
            idx = off + row_c
            return tuple(a + jnp.sum(p(kx, idx).astype(I32).reshape(ck // 8, 8, tq), axis=0)
                         for a, p in zip(accs, preds))
        accs = lax.fori_loop(0, nch, body, tuple(jnp.zeros((8, tq), I32) for _ in preds))
        return [jnp.sum(a, axis=0, keepdims=True) for a in accs]

    thr, real = _kth_largest1(
        lambda cands: count([functools.partial(lambda cd, kx, idx: kx >= cd, cd) for cd in cands]), (1, tq), DSA_TOPK)
    n_gt, n_tie = count([lambda kx, idx: kx > thr, lambda kx, idx: (kx == thr) & real])
    need = DSA_TOPK - n_gt
    j_ref[...] = jnp.full((1, tq), t_all, I32)

    @pl.when(jnp.max(jnp.where(n_tie > need, 1, 0)) > 0)
    def _():
        def step(b, jc):
            cand = jc + jnp.left_shift(jnp.int32(1), 13 - b)
            below, = count([lambda kx, idx: (kx == thr) & real & (idx < cand)])
            return jnp.where(below < need, cand, jc)
        j_ref[...] = lax.fori_loop(0, 14, step, jnp.zeros((1, tq), I32))

    jmax = j_ref[...]

    def write_bias(c, _):
        off = pl.multiple_of(c * ck, ck)
        kx = key_ref[pl.ds(off, ck), :]
        keep = (kx > thr) | ((kx == thr) & real & (off + row_c <= jmax))
        bias_ref[pl.ds(off, ck), :] = jnp.where(keep, 0.0, NEG_BIG)
        return 0

    lax.fori_loop(0, nch, write_bias, 0)

    assert ck == 2 * FLASH_BLOCK
    nr = rep * tq
    qt = q_ref[0].T
    qas = []
    for kvh in range(HKV_C):
        q4t = jnp.concatenate([qt[(kvh * rep + r) * HEAD_DIM:(kvh * rep + r + 1) * HEAD_DIM] for r in range(rep)],
                              axis=1)
        qas.append(jnp.concatenate([q4t * LOG2E, jnp.zeros((HEAD_DIM, nr), F32)], axis=0).astype(BF16))

    init = tuple((jnp.full((1, nr), NEG_BIG, F32), jnp.zeros((VT_ROWS, nr), F32)) for _ in range(HKV_C))
    res = lax.fori_loop(0, nch, lambda c, carry: _flash_pair(k_ref, v_ref, qas, carry, c, bias_ref), init)
    heads = []
    for kvh in range(HKV_C):
        acc = res[kvh][1]
        ot = acc[0:HEAD_DIM] / acc[HEAD_DIM:HEAD_DIM + 1]
        heads += [ot[:, r * tq:(r + 1) * tq] for r in range(rep)]
    o_ref[0] = jnp.concatenate(heads, axis=0).T


def _dsa_prompt(q, kaug16, vt16, qi, kiwi, kiwi16):
    b, t, _ = q.shape
    o = pl.pallas_call(
        _dsa_prompt_body,
        grid=(b, t // Q_TILE),
        in_specs=[pl.BlockSpec((1, Q_TILE, Q_C), lambda bb, i: (bb, i, 0)),
                  pl.BlockSpec((1, t, HKV_C * LANES), lambda bb, i: (bb, 0, 0)),
                  pl.BlockSpec((1, HKV_C * VT_ROWS, t), lambda bb, i: (bb, 0, 0)),
                  pl.BlockSpec((1, Q_TILE, N_IDX_HEADS * IDX_DIM), lambda bb, i: (bb, i, 0)),
                  pl.BlockSpec((1, Q_TILE, LANES), lambda bb, i: (bb, i, 0)),
                  pl.BlockSpec((1, t, LANES), lambda bb, i: (bb, 0, 0))],
        out_specs=pl.BlockSpec((1, Q_TILE, OUT_C), lambda bb, i: (bb, i, 0)),
        out_shape=jax.ShapeDtypeStruct((b, t, OUT_C), F32),
        scratch_shapes=[pltpu.VMEM((t, Q_TILE), F32), pltpu.VMEM((t, Q_TILE), F32), pltpu.VMEM((1, Q_TILE), I32)],
        compiler_params=_params(("parallel", "arbitrary")),
        name="dsa_prompt",
    )(q, kaug16, vt16, qi, kiwi, kiwi16)
    return o.reshape(b * t, OUT_C)


def _dsa_sample_body(n_pages, pt_ref, q_ref, kn_ref, vn_ref, qi_ref, kiwi_ref, *refs):
    sb = q_ref.shape[0]
    o_ref = refs[3 * sb * n_pages]
    j_ref = refs[3 * sb * n_pages + 1]
    pages = lambda kind, s_i: refs[(kind * sb + s_i) * n_pages:(kind * sb + s_i + 1) * n_pages]
    rep = HQ_C // HKV_C
    ts = q_ref.shape[1]
    n_cache = n_pages * PAGE_SIZE
    scs = []
    for s_i in range(sb):
        ip = pages(2, s_i)
        kiwi = kiwi_ref[s_i]
        qi = qi_ref[s_i].astype(BF16)
        qi8 = jnp.concatenate([qi[:, h * IDX_DIM:(h + 1) * IDX_DIM] for h in range(N_IDX_HEADS)], axis=0)
        wv = kiwi[:, IDX_DIM:IDX_DIM + N_IDX_HEADS] * (IDX_DIM ** -0.5 * N_IDX_HEADS ** -0.5)
        dots = jnp.concatenate(
            [jnp.dot(qi8, ip[p][0].astype(BF16), preferred_element_type=F32) for p in range(n_pages)]
            + [lax.dot_general(qi8, _pad_rows(kiwi[:, 0:IDX_DIM], LANES).astype(BF16), _NT,
                               preferred_element_type=F32)], axis=1)
        scs.append(_index_scores(dots, wv, ts))
    sc = jnp.concatenate(scs, axis=0)
    col = lax.broadcasted_iota(I32, sc.shape, 1)
    trow = lax.broadcasted_iota(I32, sc.shape, 0) & (ts - 1)
    adm = (col < n_cache) | (col - n_cache <= trow)
    keys = jnp.where(adm, sc + 0.0, -jnp.inf)

    def count(mask):
        return jnp.sum(mask.astype(I32), axis=1, keepdims=True)

    n_sel = min(DSA_TOPK, (n_cache + ts) // 4)
    thr, real = _kth_largest2(lambda cands: [count(keys >= cd) for cd in cands], (sb * ts, 1), n_sel)
    need = n_sel - count(keys > thr)
    tie = (keys == thr) & real

    j_ref[...] = jnp.full((sb * ts, 1), n_cache + LANES, I32)

    @pl.when(jnp.max(jnp.where(count(tie) > need, 1, 0)) > 0)
    def _():
        def step(b, jc):
            cand = jc + jnp.left_shift(jnp.int32(1), 13 - b)
            return jnp.where(count(tie & (col < cand)) < need, cand, jc)
        j_ref[...] = lax.fori_loop(0, 14, step, jnp.zeros((sb * ts, 1), I32))

    jmax = j_ref[...]
    keep = (keys > thr) | (tie & (col <= jmax))
    bias_all = jnp.where(keep, 0.0, NEG_BIG)

    for s_i in range(sb):
        kp, vp = pages(0, s_i), pages(1, s_i)
        bias = jnp.concatenate([bias_all[s_i * ts:(s_i + 1) * ts]] * HQ_C, axis=0)
        qf16 = _block_diag_queries(q_ref[s_i], HKV_C, rep).astype(BF16)
        kn16 = _pad_rows(kn_ref[s_i], LANES).astype(BF16)
        vn16 = _pad_rows(vn_ref[s_i], LANES).astype(BF16)
        s = jnp.concatenate([jnp.dot(qf16, kp[p][0].astype(BF16), preferred_element_type=F32)
                             for p in range(n_pages)]
                            + [lax.dot_general(qf16, kn16, _NT, preferred_element_type=F32)], axis=1) + bias
        m = jnp.max(s, axis=1, keepdims=True)
        p_all = jnp.exp(s - m)
        l = jnp.sum(p_all, axis=1, keepdims=True)
        p16 = p_all.astype(BF16)
        acc = jnp.dot(p16[:, n_cache:n_cache + LANES], vn16, preferred_element_type=F32)
        for p in range(n_pages):
            acc = acc + lax.dot_general(p16[:, p * PAGE_SIZE:(p + 1) * PAGE_SIZE], vp[p][0].astype(BF16), _NT,
                                        preferred_element_type=F32)
        o_ref[s_i] = _unstack_block_diag(acc / l, HKV_C, rep, ts)


def _dsa_sample(q, kn, vn, qi, kiwi, cache_k, cache_v, cache_ki, page_table):
    db, ts, _ = q.shape
    n_pages = page_table.shape[1]
    ck = _pages_minor(cache_k)
    cv = _pages_minor(cache_v)
    cache_ki = _pages_minor(cache_ki)
    sb = SAMPLE_SEQS
    seq = lambda w: pl.BlockSpec((sb, ts, w), lambda i, pt: (i, 0, 0))
    o = pl.pallas_call(
        functools.partial(_dsa_sample_body, n_pages),
        grid_spec=pltpu.PrefetchScalarGridSpec(
            num_scalar_prefetch=1,
            grid=(db // sb,),
            in_specs=[seq(Q_C), seq(KV_C), seq(KV_C), seq(N_IDX_HEADS * IDX_DIM), seq(LANES)]
            + _page_specs(n_pages, KV_C, sb) * 2 + _page_specs(n_pages, IDX_DIM, sb),
            out_specs=seq(OUT_C),
            scratch_shapes=[pltpu.VMEM((sb * ts, 1), I32)],
        ),
        out_shape=jax.ShapeDtypeStruct((db, ts, OUT_C), F32),
        compiler_params=_params(("arbitrary",)),
        name="dsa_sample",
    )(page_table.reshape(-1), q, kn, vn, qi, kiwi,
      *([ck] * (sb * n_pages)), *([cv] * (sb * n_pages)), *([cache_ki] * (sb * n_pages)))
    return o.reshape(db * ts, OUT_C)


def _rope_tables(pos):
    half = HEAD_DIM // 2
    inv_freq = ROPE_THETA ** (-jnp.arange(half, dtype=F32) / half)
    ang = pos.astype(F32)[:, None] * inv_freq[None, :]
    c, s = jnp.cos(ang), jnp.sin(ang)
    return jnp.tile(c, (1, LANES // half)), jnp.concatenate([-s, s] * (LANES // HEAD_DIM), axis=1)


def _gain128(g):
    return jnp.tile(g.astype(F32), LANES // HEAD_DIM).reshape(1, LANES)


def _layer_a(xp, xs, shp, shs, states, params, tabs_p, tabs_s):
    norm_g, w_in, q_gain, k_gain, w_out = params
    b, t = shp
    db, ts = shs
    w16 = w_in.astype(BF16)
    args = (norm_g.reshape(1, -1), w16)
    gains = (_gain128(q_gain), _gain128(k_gain))
    qp, kp, vp, kp16, vp16, sgp = _project("a", xp, *args, *tabs_p, *gains)
    qs, ks, vs, _, _, sgs = _project("a", xs, *args, *tabs_s, *gains)
    hw = HKV_A * HEAD_DIM
    grp_p, grp_s, new_state = [], [], []
    for g, (win, dil) in enumerate(DIL_GROUPS):
        grp_p.append(_dil_prompt(qp.reshape(b, t, Q_A), kp.reshape(b, t, KV_A), vp.reshape(b, t, KV_A), g, dil))
        slabs, ns = _dil_sample(qs.reshape(db, ts, Q_A), ks.reshape(db, ts, KV_A), vs.reshape(db, ts, KV_A),
                                states[g], g, states[g].shape[1], dil)
        grp_s.append(slabs)
        wk = min(win, t)
        kg = kp.reshape(b, t, KV_A)[:, t - wk:, g * hw:(g + 1) * hw].reshape(b, wk, 1, HKV_A, HEAD_DIM)
        vg = vp.reshape(b, t, KV_A)[:, t - wk:, g * hw:(g + 1) * hw].reshape(b, wk, 1, HKV_A, HEAD_DIM)
        new_state.append(jnp.concatenate([kg, vg], axis=2))
        new_state.append(ns)
    w_out16 = w_out.astype(BF16)
    return _out_project(grp_p, sgp, xp, w_out16), _out_project(grp_s, sgs, xs, w_out16), new_state


def _layer_b(xp, xs, shp, shs, states, params, tabs_p, tabs_s, page_table):
    norm_g, w_in, q_gain, k_gain, w_out = params
    cache_k, cache_v = states
    b, t = shp
    db, ts = shs
    w16 = w_in.astype(BF16)
    args = (norm_g.reshape(1, -1), w16)
    gains = (_gain128(q_gain), _gain128(k_gain))
    qp, kp, vp, _, _, sgp, kmean, kaug16, vt16 = _project("b", xp, *args, *tabs_p, *gains, aug=True)
    qs, ks, vs, _, _, sgs = _project("b", xs, *args, *tabs_s, *gains)
    o_p = _moba_prompt(qp.reshape(b, t, Q_B), kaug16.reshape(b, t, HKV_B * LANES), vt16,
                       kmean.reshape(b, t // MOBA_BLOCK, KV_B))
    o_s = _moba_sample(qs.reshape(db, ts, Q_B), ks.reshape(db, ts, KV_B), vs.reshape(db, ts, KV_B),
                       cache_k, cache_v, page_table)
    w_out16 = w_out.astype(BF16)
    new_state = [kp.reshape(b, t, HKV_B, HEAD_DIM), ks.reshape(db, ts, HKV_B, HEAD_DIM),
                 vp.reshape(b, t, HKV_B, HEAD_DIM), vs.reshape(db, ts, HKV_B, HEAD_DIM)]
    return _out_project([[o_p]], sgp, xp, w_out16), _out_project([[o_s]], sgs, xs, w_out16), new_state


def _layer_c(xp, xs, shp, shs, states, params, tabs_p, tabs_s, page_table):
    norm_g, w_in, q_gain, k_gain, w_out = params
    cache_k, cache_v, cache_ki = states
    b, t = shp
    db, ts = shs
    w16 = jnp.pad(w_in, ((0, 0), (0, IN_C_PAD - IN_C))).astype(BF16)
    args = (norm_g.reshape(1, -1), w16)
    gains = (_gain128(q_gain), _gain128(k_gain))
    qp, kp, vp, _, _, sgp, qip, kiwip, kiwip16, kaug16, vt16 = _project("c", xp, *args, *tabs_p, *gains, aug=True)
    qs, ks, vs, _, _, sgs, qis, kiwis, _ = _project("c", xs, *args, *tabs_s, *gains)
    o_p = _dsa_prompt(qp.reshape(b, t, Q_C), kaug16.reshape(b, t, HKV_C * LANES), vt16,
                      qip.reshape(b, t, -1), kiwip.reshape(b, t, LANES), kiwip16.reshape(b, t, LANES))
    o_s = _dsa_sample(qs.reshape(db, ts, Q_C), ks.reshape(db, ts, KV_C), vs.reshape(db, ts, KV_C),
                      qis.reshape(db, ts, -1), kiwis.reshape(db, ts, LANES), cache_k, cache_v, cache_ki, page_table)
    w_out16 = w_out.astype(BF16)
    new_state = [kp.reshape(b, t, HKV_C, HEAD_DIM), ks.reshape(db, ts, HKV_C, HEAD_DIM),
                 vp.reshape(b, t, HKV_C, HEAD_DIM), vs.reshape(db, ts, HKV_C, HEAD_DIM),
                 kiwip.reshape(b, t, LANES)[:, :, :IDX_DIM], kiwis.reshape(db, ts, LANES)[:, :, :IDX_DIM]]
    return _out_project([[o_p]], sgp, xp, w_out16), _out_project([[o_s]], sgs, xs, w_out16), new_state


def kernel(x_prompt, x_sample, state_l0_kv_w128, state_l0_kv_w512, state_l0_kv_w2048, cache_l1_k, cache_l1_v, cache_l2_k, cache_l2_v, cache_l2_kidx, state_l3_kv_w128, state_l3_kv_w512, state_l3_kv_w2048, page_table, l0_norm, l0_w_in, l0_q_norm, l0_k_norm, l0_w_out, l1_norm, l1_w_in, l1_q_norm, l1_k_norm, l1_w_out, l2_norm, l2_w_in, l2_q_norm, l2_k_norm, l2_w_out, l3_norm, l3_w_in, l3_q_norm, l3_k_norm, l3_w_out):
    b, t, _ = x_prompt.shape
    db, ts, _ = x_sample.shape
    assert t % (16 * Q_TILE) == 0 and t % DSA_CHUNK == 0 and (db * ts) % PROJ_ROWS == 0 and PROJ_ROWS % ts == 0
    tabs_p = _rope_tables(jnp.arange(t))
    tabs_s = _rope_tables(PAST_LEN + (jnp.arange(PROJ_ROWS) % ts))
    xp = x_prompt.reshape(b * t, D_MODEL)
    xs = x_sample.reshape(db * ts, D_MODEL)
    states = ((state_l0_kv_w128, state_l0_kv_w512, state_l0_kv_w2048), (cache_l1_k, cache_l1_v),
              (cache_l2_k, cache_l2_v, cache_l2_kidx), (state_l3_kv_w128, state_l3_kv_w512, state_l3_kv_w2048))
    params = ((l0_norm, l0_w_in, l0_q_norm, l0_k_norm, l0_w_out), (l1_norm, l1_w_in, l1_q_norm, l1_k_norm, l1_w_out),
              (l2_norm, l2_w_in, l2_q_norm, l2_k_norm, l2_w_out), (l3_norm, l3_w_in, l3_q_norm, l3_k_norm, l3_w_out))
    new_state = []
    for i in range(4):
        common = (xp, xs, (b, t), (db, ts), states[i], params[i], tabs_p, tabs_s)
        if i % 3 == 0:
            xp, xs, st = _layer_a(*common)
        elif i % 3 == 1:
            xp, xs, st = _layer_b(*common, page_table)
        else:
            xp, xs, st = _layer_c(*common, page_table)
        new_state.extend(st)
    return (xp.reshape(b, t, D_MODEL), xs.reshape(db, ts, D_MODEL), *new_state)
```
